```python
import math
import jax
import jax.numpy as jnp
from jax import lax
import numpy as np

D_MODEL = 1024
BATCH = 8
SEQ = 8192
DEPTH = 1

EPS = 1e-5
GLA_HEADS = 4
GLA_DK = 128
GLA_DV = 256
GLA_RANK = 16
GLA_TAU = 16.0
GLA_CHUNK = 64
GLA_QK_DIM = GLA_HEADS * GLA_DK
GLA_V_DIM = GLA_HEADS * GLA_DV
SSD_DINNER = 2 * D_MODEL
SSD_HEADDIM = 64
SSD_HEADS = SSD_DINNER // SSD_HEADDIM
SSD_GROUPS = 4
SSD_HPG = SSD_HEADS // SSD_GROUPS
SSD_STATE = 128
SSD_CONV = 4
SSD_CONV_PAD = (1, 2)
SSD_CHUNK = 64
SSD_BC_DIM = SSD_GROUPS * SSD_STATE
SSD_CONV_DIM = SSD_DINNER + 2 * SSD_BC_DIM
N_EXPERTS = 32
TOP_K = 4
D_FF = D_MODEL
SWIGLU_LIMIT = 7.0
SWIGLU_ALPHA = 1.702
MOE_BLOCK = 128
IN_WIDTHS = (GLA_QK_DIM, GLA_QK_DIM, GLA_V_DIM, GLA_V_DIM, GLA_RANK, GLA_RANK,
             SSD_DINNER, SSD_CONV_DIM, 2 * SSD_HEADS, D_MODEL, D_MODEL)
D_IN_PROJ = sum(IN_WIDTHS)

kernel_name = 'hybrid_gla_ssd_moe_block'


def _split_points():
    pts, acc = [], 0
    for w in IN_WIDTHS[:-1]:
        acc += w
        pts.append(acc)
    return pts


def rmsnorm(t, g):
    t32 = t.astype(jnp.float32)
    t32 = t32 * lax.rsqrt(jnp.mean(t32 * t32, axis=-1, keepdims=True) + EPS)
    return t32.astype(t.dtype) * g


def _flip(t):
    return jnp.flip(t, axis=1)


def _chunks(t, c):
    b, s = t.shape[:2]
    return jnp.moveaxis(t.reshape((b, s // c, c) + t.shape[2:]), 1, 0)


def _unchunks(t):
    t = jnp.moveaxis(t, 0, 1)
    return t.reshape((t.shape[0], t.shape[1] * t.shape[2]) + t.shape[3:])


def gla_scan(q, k, v, log_a):
    c = GLA_CHUNK
    mask = jnp.tril(jnp.ones((c, c), dtype=bool))

    def step(state, inp):
        qc, kc, vc, gc = inp
        b = jnp.cumsum(gc, axis=1)
        b_mid = b[:, c // 2][:, None]
        qs = qc * jnp.exp(b - b_mid)
        ks = kc * jnp.exp(b_mid - b)
        s = jnp.where(mask, jnp.einsum('bihk,bjhk->bhij', qs, ks), 0.0)
        o = jnp.einsum('bhij,bjhv->bihv', s, vc)
        o = o + jnp.einsum('bihk,bhkv->bihv', qc * jnp.exp(b), state)
        b_last = b[:, -1]
        kd = kc * jnp.exp(b_last[:, None] - b)
        state = state * jnp.exp(b_last)[..., None] + jnp.einsum('bjhk,bjhv->bhkv', kd, vc)
        return state, o

    state0 = jnp.zeros((q.shape[0], GLA_HEADS, GLA_DK, GLA_DV), jnp.float32)
    _, o = lax.scan(step, state0, (_chunks(q, c), _chunks(k, c), _chunks(v, c), _chunks(log_a, c)))
    return _unchunks(o)


def ssd_scan(x, dt, a, bm, cm):
    c = SSD_CHUNK
    mask = jnp.tril(jnp.ones((c, c), dtype=bool))[:, :, None, None]

    def step(state, inp):
        xc, dtc, bc, cc = inp
        cum = jnp.cumsum(dtc * a, axis=1)
        seg = cum[:, :, None] - cum[:, None, :]
        decay = jnp.exp(jnp.where(mask, seg, -jnp.inf))
        cb = jnp.einsum('bign,bjgn->bijg', cc, bc)
        w = cb[..., None] * decay * dtc[:, None]
        y = jnp.einsum('bijgr,bjgrp->bigrp', w, xc)
        y = y + jnp.einsum('bign,bgrpn->bigrp', cc, state) * jnp.exp(cum)[..., None]
        to_end = jnp.exp(cum[:, -1:] - cum) * dtc
        state = state * jnp.exp(cum[:, -1])[..., None, None] + jnp.einsum('bjgn,bjgr,bjgrp->bgrpn', bc, to_end, xc)
        return state, y

    state0 = jnp.zeros((x.shape[0], SSD_GROUPS, SSD_HPG, SSD_HEADDIM, SSD_STATE), jnp.float32)
    _, y = lax.scan(step, state0, (_chunks(x, c), _chunks(dt, c), _chunks(bm, c), _chunks(cm, c)))
    return _unchunks(y)


def dwconv(t, w, b):
    y = lax.conv_general_dilated(t, w[:, None, :], window_strides=(1,), padding=[SSD_CONV_PAD],
                                 dimension_numbers=('NWC', 'WIO', 'NWC'), feature_group_count=t.shape[-1])
    return y + b


def moe(xn, w_router, b_router, w1, b1, w2, b2):
    bsz, s, d = xn.shape
    t = bsz * s
    n_assign = t * TOP_K
    xt = xn.reshape(t, d)
    logits = (xt @ w_router + b_router).astype(jnp.float32)
    top_v, top_i = lax.top_k(logits, TOP_K)
    gates = jax.nn.softmax(top_v, axis=-1)
    flat_e = top_i.reshape(-1)
    flat_tok = jnp.arange(n_assign, dtype=jnp.int32) // TOP_K
    flat_g = gates.reshape(-1)
    order = jnp.argsort(flat_e)
    e_sorted = flat_e[order]
    counts = jnp.bincount(flat_e, length=N_EXPERTS)
    padded = (counts + MOE_BLOCK - 1) // MOE_BLOCK * MOE_BLOCK
    start = jnp.cumsum(counts) - counts
    pend = jnp.cumsum(padded)
    pstart = pend - padded
    rank = jnp.arange(n_assign, dtype=jnp.int32) - start[e_sorted]
    dest = pstart[e_sorted] + rank
    n_pad = n_assign + N_EXPERTS * MOE_BLOCK
    n_blocks = n_pad // MOE_BLOCK
    tok_buf = jnp.full((n_pad,), t, jnp.int32).at[dest].set(flat_tok[order])
    gate_buf = jnp.zeros((n_pad,), jnp.float32).at[dest].set(flat_g[order])
    blk_e = jnp.minimum(jnp.searchsorted(pend, jnp.arange(n_blocks, dtype=jnp.int32) * MOE_BLOCK, side='right'),
                        N_EXPERTS - 1)
    x_pad = jnp.concatenate([xt, jnp.zeros((1, d), xt.dtype)], axis=0)
    xb = x_pad[tok_buf].reshape(n_blocks, MOE_BLOCK, d)

    def expert_block(args):
        xblk, e = args
        hdn = xblk @ w1[e] + b1[e]
        gate = jnp.minimum(hdn[..., :D_FF], SWIGLU_LIMIT)
        lin = jnp.clip(hdn[..., D_FF:], -SWIGLU_LIMIT, SWIGLU_LIMIT)
        act = gate * jax.nn.sigmoid(SWIGLU_ALPHA * gate) * (lin + 1.0)
        return act @ w2[e] + b2[e]

    yb = lax.map(expert_block, (xb, blk_e)).reshape(n_pad, d)
    yb = yb * gate_buf[:, None].astype(yb.dtype)
    out = jnp.zeros((t + 1, d), yb.dtype).at[tok_buf].add(yb)[:t]
    return out.reshape(bsz, s, d)


def hybrid_layer(h, norm_mix_g, w_in, gla_fw2_f, gla_fb_f, gla_fw2_b, gla_fb_b, gla_norm_g, conv_w, conv_b,
                 dt_bias_f, dt_bias_b, a_log_f, a_log_b, ssd_d, ssd_norm_g, w_up_gla, w_up_ssd, w_out,
                 norm_ffn_g, w_router, b_router, w1, b1, w2, b2):
    f32 = jnp.float32
    bsz, s, _ = h.shape
    xn = rmsnorm(h, norm_mix_g)
    proj = xn @ w_in
    q, k, v, r, lr_f, lr_b, z, xbc, dt_raw, g_gla, g_ssd = jnp.split(proj, _split_points(), axis=-1)

    q = (q.reshape(bsz, s, GLA_HEADS, GLA_DK) * GLA_DK ** -0.5).astype(f32)
    k = k.reshape(bsz, s, GLA_HEADS, GLA_DK).astype(f32)
    v = v.reshape(bsz, s, GLA_HEADS, GLA_DV).astype(f32)
    log_a_f = (jax.nn.log_sigmoid((lr_f @ gla_fw2_f + gla_fb_f).astype(f32)) / GLA_TAU).reshape(bsz, s, GLA_HEADS, GLA_DK)
    log_a_b = (jax.nn.log_sigmoid((lr_b @ gla_fw2_b + gla_fb_b).astype(f32)) / GLA_TAU).reshape(bsz, s, GLA_HEADS, GLA_DK)
    o = gla_scan(q, k, v, log_a_f) + _flip(gla_scan(_flip(q), _flip(k), _flip(v), _flip(log_a_b)))
    o = rmsnorm(o, gla_norm_g).reshape(bsz, s, GLA_V_DIM).astype(h.dtype) * jax.nn.silu(r)

    xbc = jax.nn.silu(dwconv(xbc, conv_w, conv_b))
    xs, bm, cm = jnp.split(xbc, [SSD_DINNER, SSD_DINNER + SSD_BC_DIM], axis=-1)
    xs = xs.reshape(bsz, s, SSD_GROUPS, SSD_HPG, SSD_HEADDIM).astype(f32)
    bm = bm.reshape(bsz, s, SSD_GROUPS, SSD_STATE).astype(f32)
    cm = cm.reshape(bsz, s, SSD_GROUPS, SSD_STATE).astype(f32)
    dt_f = jax.nn.softplus((dt_raw[..., :SSD_HEADS] + dt_bias_f).astype(f32)).reshape(bsz, s, SSD_GROUPS, SSD_HPG)
    dt_b = jax.nn.softplus((dt_raw[..., SSD_HEADS:] + dt_bias_b).astype(f32)).reshape(bsz, s, SSD_GROUPS, SSD_HPG)
    a_f = -jnp.exp(a_log_f.astype(f32)).reshape(SSD_GROUPS, SSD_HPG)
    a_b = -jnp.exp(a_log_b.astype(f32)).reshape(SSD_GROUPS, SSD_HPG)
    y = (ssd_scan(xs, dt_f, a_f, bm, cm)
         + _flip(ssd_scan(_flip(xs), _flip(dt_b), a_b, _flip(bm), _flip(cm)))
         + ssd_d.astype(f32).reshape(SSD_GROUPS, SSD_HPG, 1) * xs)
    y = y.reshape(bsz, s, SSD_DINNER) * jax.nn.silu(z.astype(f32))
    y = rmsnorm(y.reshape(bsz, s, SSD_GROUPS, SSD_DINNER // SSD_GROUPS),
                ssd_norm_g.reshape(SSD_GROUPS, SSD_DINNER // SSD_GROUPS))
    y = y.reshape(bsz, s, SSD_DINNER).astype(h.dtype)

    mix = jax.nn.sigmoid(g_gla) * (o @ w_up_gla) + jax.nn.sigmoid(g_ssd) * (y @ w_up_ssd)
    h = h + mix @ w_out

    h = h + moe(rmsnorm(h, norm_ffn_g), w_router, b_router, w1, b1, w2, b2)
    return h


def setup_inputs(seed: int = 0) -> dict:
    key = jax.random.key(seed)
    ks = iter(jax.random.split(key, 32))
    nrm = lambda shape, scale: jax.random.normal(next(ks), shape, jnp.float32) * scale
    gain = lambda shape: 1.0 + nrm(shape, 0.02)
    L = DEPTH
    u = jax.random.uniform(next(ks), (2, L, SSD_HEADS), jnp.float32)
    dt0 = jnp.exp(u * (math.log(0.1) - math.log(0.001)) + math.log(0.001))
    dt_bias = dt0 + jnp.log(-jnp.expm1(-dt0))
    a_log = jnp.log(jax.random.uniform(next(ks), (2, L, SSD_HEADS), jnp.float32, 1.0, 16.0))
    return {
        'x': nrm((BATCH, SEQ, D_MODEL), 1.0),
        'norm_mix_g': gain((L, D_MODEL)),
        'w_in': nrm((L, D_MODEL, D_IN_PROJ), D_MODEL ** -0.5),
        'gla_fw2_f': nrm((L, GLA_RANK, GLA_QK_DIM), GLA_RANK ** -0.5),
        'gla_fb_f': nrm((L, GLA_QK_DIM), 0.1),
        'gla_fw2_b': nrm((L, GLA_RANK, GLA_QK_DIM), GLA_RANK ** -0.5),
        'gla_fb_b': nrm((L, GLA_QK_DIM), 0.1),
        'gla_norm_g': gain((L, GLA_DV)),
        'conv_w': nrm((L, SSD_CONV, SSD_CONV_DIM), SSD_CONV ** -0.5),
        'conv_b': nrm((L, SSD_CONV_DIM), 0.02),
        'dt_bias_f': dt_bias[0],
        'dt_bias_b': dt_bias[1],
        'a_log_f': a_log[0],
        'a_log_b': a_log[1],
        'ssd_d': 1.0 + nrm((L, SSD_HEADS), 0.1),
        'ssd_norm_g': gain((L, SSD_DINNER)),
        'w_up_gla': nrm((L, GLA_V_DIM, D_MODEL), GLA_V_DIM ** -0.5),
        'w_up_ssd': nrm((L, SSD_DINNER, D_MODEL), SSD_DINNER ** -0.5),
        'w_out': nrm((L, D_MODEL, D_MODEL), D_MODEL ** -0.5),
        'norm_ffn_g': gain((L, D_MODEL)),
        'w_router': nrm((L, D_MODEL, N_EXPERTS), D_MODEL ** -0.5),
        'b_router': nrm((L, N_EXPERTS), 0.01),
        'w1': nrm((L, N_EXPERTS, D_MODEL, 2 * D_FF), D_MODEL ** -0.5),
        'b1': nrm((L, N_EXPERTS, 2 * D_FF), 0.01),
        'w2': nrm((L, N_EXPERTS, D_FF, D_MODEL), D_FF ** -0.5),
        'b2': nrm((L, N_EXPERTS, D_MODEL), 0.01),
        'norm_final_g': gain((D_MODEL,)),
    }


def reference(x, norm_mix_g, w_in, gla_fw2_f, gla_fb_f, gla_fw2_b, gla_fb_b, gla_norm_g, conv_w, conv_b,
              dt_bias_f, dt_bias_b, a_log_f, a_log_b, ssd_d, ssd_norm_g, w_up_gla, w_up_ssd, w_out,
              norm_ffn_g, w_router, b_router, w1, b1, w2, b2, norm_final_g):
    h = x
    for l in range(DEPTH):
        h = hybrid_layer(h, norm_mix_g[l], w_in[l], gla_fw2_f[l], gla_fb_f[l], gla_fw2_b[l], gla_fb_b[l],
                         gla_norm_g[l], conv_w[l], conv_b[l], dt_bias_f[l], dt_bias_b[l], a_log_f[l], a_log_b[l],
                         ssd_d[l], ssd_norm_g[l], w_up_gla[l], w_up_ssd[l], w_out[l], norm_ffn_g[l],
                         w_router[l], b_router[l], w1[l], b1[l], w2[l], b2[l])
    return rmsnorm(h, norm_final_g)
```

```python
import functools

import jax
import jax.numpy as jnp
from jax import lax
from jax.experimental import pallas as pl
from jax.experimental.pallas import tpu as pltpu

F32 = jnp.float32
BF16 = jnp.bfloat16
I32 = jnp.int32
U32 = jnp.uint32

D_MODEL = 1024
EPS = 1e-5
GLA_HEADS = 4
GLA_DK = 128
GLA_DV = 256
GLA_RANK = 16
GLA_TAU = 16.0
GLA_QK = GLA_HEADS * GLA_DK
GLA_V = GLA_HEADS * GLA_DV
SSD_DINNER = 2048
SSD_HEADDIM = 64
SSD_HEADS = 32
SSD_GROUPS = 4
SSD_HPG = 8
SSD_STATE = 128
SSD_CONV = 4
SSD_BC = SSD_GROUPS * SSD_STATE
SSD_GW = SSD_HPG * SSD_HEADDIM
N_EXPERTS = 32
TOP_K = 4
D_FF = 1024
SWIGLU_LIMIT = 7.0
SWIGLU_ALPHA = 1.702
CHUNK = 64

C_Z, C_XS, C_Q, C_K, C_V, C_R, C_GG, C_GS, C_B, C_C = 0, 2048, 4096, 4608, 5120, 6144, 7168, 8192, 9216, 9728
N_MAIN = 10240
N_SMALL = 128
S_LRF, S_LRB, S_DTF, S_DTB = 0, 16, 32, 64

VMEM_LIMIT = 56 * 1024 * 1024
MOE_BM = 256


def _dot(a, b):
    return jnp.dot(a, b, preferred_element_type=F32)


def _dot_nt(a, b):
    return lax.dot_general(a, b, (((1,), (1,)), ((), ())), preferred_element_type=F32)


def _dot_tn(a, b):
    return lax.dot_general(a, b, (((0,), (0,)), ((), ())), preferred_element_type=F32)


def _split_bf16(x, n):
    parts = []
    r = x
    for _ in range(n):
        p = r.astype(BF16)
        parts.append(p)
        r = r - p.astype(F32)
    return parts


def _dot_exact_lhs(m_bf16, x, n):
    acc = None
    for p in _split_bf16(x, n):
        t = _dot(m_bf16, p)
        acc = t if acc is None else acc + t
    return acc


def _dot_exact_rhs(x, m_bf16, n):
    acc = None
    for p in _split_bf16(x, n):
        t = _dot(p, m_bf16)
        acc = t if acc is None else acc + t
    return acc


def _sigmoid(x):
    return 1.0 / (1.0 + jnp.exp(-x))


def _silu(x):
    return x * _sigmoid(x)


def _pack_bf16_pair(lo, hi):
    lo_b = lax.bitcast_convert_type(lo.astype(BF16).astype(F32), U32)
    hi_b = lax.bitcast_convert_type(hi.astype(BF16).astype(F32), U32)
    return (lo_b >> 16) | (hi_b & jnp.uint32(0xFFFF0000))


def _unpack_bf16_pair(u):
    lo = lax.bitcast_convert_type(u << 16, F32)
    hi = lax.bitcast_convert_type(u & jnp.uint32(0xFFFF0000), F32)
    return lo, hi


def _inproj_kernel(x_ref, g_ref, w_ref, ws_ref, o_ref, os_ref, xn_ref):
    @pl.when(pl.program_id(1) == 0)
    def _():
        x = x_ref[...]
        ms = jnp.mean(x * x, axis=-1, keepdims=True)
        xb = ((x * lax.rsqrt(ms + EPS)) * g_ref[...]).astype(BF16)
        xn_ref[...] = xb
        os_ref[...] = _dot(xb, ws_ref[...])

    o_ref[...] = _dot(xn_ref[...], w_ref[...]).astype(BF16)


def _inproj(x2, g, w_main, w_small, tm, tn):
    t = x2.shape[0]
    return pl.pallas_call(
        _inproj_kernel,
        grid=(t // tm, N_MAIN // tn),
        in_specs=[
            pl.BlockSpec((tm, D_MODEL), lambda i, j: (i, 0)),
            pl.BlockSpec((1, D_MODEL), lambda i, j: (0, 0)),
            pl.BlockSpec((D_MODEL, tn), lambda i, j: (0, j)),
            pl.BlockSpec((D_MODEL, N_SMALL), lambda i, j: (0, 0)),
        ],
        out_specs=[
            pl.BlockSpec((tm, tn), lambda i, j: (i, j)),
            pl.BlockSpec((tm, N_SMALL), lambda i, j: (i, 0)),
        ],
        out_shape=[
            jax.ShapeDtypeStruct((t, N_MAIN), BF16),
            jax.ShapeDtypeStruct((t, N_SMALL), F32),
        ],
        scratch_shapes=[pltpu.VMEM((tm, D_MODEL), BF16)],
        compiler_params=pltpu.CompilerParams(
            dimension_semantics=("arbitrary", "arbitrary"), vmem_limit_bytes=VMEM_LIMIT),
        name="inproj",
    )(x2, g, w_main, w_small)


def _tri_masks(n):
    r = lax.broadcasted_iota(I32, (n, n), 0)
    c = lax.broadcasted_iota(I32, (n, n), 1)
    return r >= c, c >= r


def _gla_chunk(c0, q_ref, k_ref, v_ref, sm_ref, fw2_ref, fb_ref, o_ref, st_ref, mask, lr_off, mid_row, last_row):
    rows = pl.ds(c0, CHUNK)
    tri = jnp.where(mask, 1.0, 0.0).astype(BF16)
    lr = sm_ref[rows, lr_off:lr_off + GLA_RANK].astype(BF16)
    xg = _dot(lr, fw2_ref[...]) + fb_ref[...]
    g = (jnp.minimum(xg, 0.0) - jnp.log(1.0 + jnp.exp(-jnp.abs(xg)))) * (1.0 / GLA_TAU)
    b = _dot_exact_lhs(tri, g, 2)
    b_mid = b[mid_row:mid_row + 1, :]
    b_last = b[last_row:last_row + 1, :]
    q = q_ref[rows, :].astype(F32) * (GLA_DK ** -0.5)
    k = k_ref[rows, :].astype(F32)
    v = v_ref[rows, :]
    qs = (q * jnp.exp(b - b_mid)).astype(BF16)
    ks = (k * jnp.exp(b_mid - b)).astype(BF16)
    qe = (q * jnp.exp(b)).astype(BF16)
    kd = (k * jnp.exp(b_last - b)).astype(BF16)
    e_last = jnp.exp(b_last)
    for h in range(GLA_HEADS):
        ks_ = slice(h * GLA_DK, (h + 1) * GLA_DK)
        vs_ = slice(h * GLA_DV, (h + 1) * GLA_DV)
        s = jnp.where(mask, _dot_nt(qs[:, ks_], ks[:, ks_]), 0.0).astype(BF16)
        st = st_ref[h]
        o = _dot(s, v[:, vs_]) + _dot(qe[:, ks_], st.astype(BF16))
        o_ref[rows, vs_] = o.astype(o_ref.dtype)
        e_col = jnp.transpose(jnp.broadcast_to(e_last[:, ks_], (8, GLA_DK)))[:, 0:1]
        st_ref[h] = st * e_col + _dot_tn(kd[:, ks_], v[:, vs_])


def _gla_kernel(qf, kf, vf, smf, qb, kb, vb, smb, fw2f, fbf, fw2b, fbb, of_ref, ob_ref, stf, stb, *, n_chunks):
    @pl.when(pl.program_id(1) == 0)
    def _():
        stf[...] = jnp.zeros_like(stf)
        stb[...] = jnp.zeros_like(stb)

    lower, upper = _tri_masks(CHUNK)

    def body(i, carry):
        cf = pl.multiple_of(i * CHUNK, CHUNK)
        cb = pl.multiple_of((n_chunks - 1 - i) * CHUNK, CHUNK)
        _gla_chunk(cf, qf, kf, vf, smf, fw2f, fbf, of_ref, stf, lower, S_LRF, CHUNK // 2, CHUNK - 1)
        _gla_chunk(cb, qb, kb, vb, smb, fw2b, fbb, ob_ref, stb, upper, S_LRB, CHUNK // 2 - 1, 0)
        return carry

    lax.fori_loop(0, n_chunks, body, 0)


def _gla(proj, small, fw2f, fbf, fw2b, fbb, bsz, seq, lb):
    t = bsz * seq
    nb = seq // lb

    def fwd(w, col):
        return pl.BlockSpec((lb, w), lambda b, n: (b * nb + n, col))

    def bwd(w, col):
        return pl.BlockSpec((lb, w), lambda b, n: (b * nb + nb - 1 - n, col))

    const = lambda shape: pl.BlockSpec(shape, lambda b, n: (0, 0))
    return pl.pallas_call(
        functools.partial(_gla_kernel, n_chunks=lb // CHUNK),
        grid=(bsz, nb),
        in_specs=[
            fwd(GLA_QK, C_Q // GLA_QK), fwd(GLA_QK, C_K // GLA_QK), fwd(GLA_V, C_V // GLA_V), fwd(N_SMALL, 0),
            bwd(GLA_QK, C_Q // GLA_QK), bwd(GLA_QK, C_K // GLA_QK), bwd(GLA_V, C_V // GLA_V), bwd(N_SMALL, 0),
            const((GLA_RANK, GLA_QK)), const((1, GLA_QK)), const((GLA_RANK, GLA_QK)), const((1, GLA_QK)),
        ],
        out_specs=[fwd(GLA_V, 0), bwd(GLA_V, 0)],
        out_shape=[jax.ShapeDtypeStruct((t, GLA_V), BF16)] * 2,
        scratch_shapes=[pltpu.VMEM((GLA_HEADS, GLA_DK, GLA_DV), F32)] * 2,
        compiler_params=pltpu.CompilerParams(
            dimension_semantics=("arbitrary", "arbitrary"), vmem_limit_bytes=VMEM_LIMIT),
        name="gla_scan",
    )(proj, proj, proj, small, proj, proj, proj, small, fw2f, fbf, fw2b, fbb)


HALO = 16


def _conv_one(x_ref, p_ref, n_ref, w_ref, b_ref, o_ref, has_prev, has_next):
    x = x_ref[...].astype(F32)
    rb = x.shape[0]
    row = lax.broadcasted_iota(I32, x.shape, 0)
    prev = jnp.where(has_prev, p_ref[HALO - 1:HALO, :].astype(F32), 0.0)
    nxt = jnp.where(has_next, n_ref[0:2, :].astype(F32), 0.0)
    xm1 = jnp.where(row == 0, prev, pltpu.roll(x, 1, 0))
    xp1 = jnp.where(row == rb - 1, nxt[0:1, :], pltpu.roll(x, rb - 1, 0))
    xp2 = jnp.where(row == rb - 2, nxt[0:1, :], jnp.where(row == rb - 1, nxt[1:2, :], pltpu.roll(x, rb - 2, 0)))
    w = w_ref[...]
    y = xm1 * w[0:1, :] + x * w[1:2, :] + xp1 * w[2:3, :] + xp2 * w[3:4, :] + b_ref[...]
    o_ref[...] = _silu(y).astype(o_ref.dtype)


def _conv_kernel(xs, xsp, xsn, bm, bmp, bmn, cm, cmp_, cmn, wx, bx, wb, bb, wc, bc, oxs, obm, ocm, *, rb, seq):
    t0 = pl.program_id(0) * rb
    has_prev = (t0 % seq) != 0
    has_next = ((t0 + rb) % seq) != 0
    _conv_one(xs, xsp, xsn, wx, bx, oxs, has_prev, has_next)
    _conv_one(bm, bmp, bmn, wb, bb, obm, has_prev, has_next)
    _conv_one(cm, cmp_, cmn, wc, bc, ocm, has_prev, has_next)


def _conv(proj, conv_w, conv_b, seq, rb):
    t = proj.shape[0]
    nh = t // HALO
    per = rb // HALO

    def trio(w, col):
        cb = col // w
        return [
            pl.BlockSpec((rb, w), lambda i: (i, cb)),
            pl.BlockSpec((HALO, w), lambda i: (jnp.maximum(i * per - 1, 0), cb)),
            pl.BlockSpec((HALO, w), lambda i: (jnp.minimum((i + 1) * per, nh - 1), cb)),
        ]

    def wspecs(w):
        return [pl.BlockSpec((SSD_CONV, w), lambda i: (0, 0)), pl.BlockSpec((1, w), lambda i: (0, 0))]

    wx, wb, wc = conv_w[:, :SSD_DINNER], conv_w[:, SSD_DINNER:SSD_DINNER + SSD_BC], conv_w[:, SSD_DINNER + SSD_BC:]
    bx, bb, bc = (conv_b[None, :SSD_DINNER], conv_b[None, SSD_DINNER:SSD_DINNER + SSD_BC],
                  conv_b[None, SSD_DINNER + SSD_BC:])
    return pl.pallas_call(
        functools.partial(_conv_kernel, rb=rb, seq=seq),
        grid=(t // rb,),
        in_specs=trio(SSD_DINNER, C_XS) + trio(SSD_BC, C_B) + trio(SSD_BC, C_C)
        + wspecs(SSD_DINNER) + wspecs(SSD_BC) + wspecs(SSD_BC),
        out_specs=[pl.BlockSpec((rb, SSD_DINNER), lambda i: (i, 0)),
                   pl.BlockSpec((rb, SSD_BC), lambda i: (i, 0)),
                   pl.BlockSpec((rb, SSD_BC), lambda i: (i, 0))],
        out_shape=[jax.ShapeDtypeStruct((t, SSD_DINNER), BF16),
                   jax.ShapeDtypeStruct((t, SSD_BC), BF16),
                   jax.ShapeDtypeStruct((t, SSD_BC), BF16)],
        compiler_params=pltpu.CompilerParams(dimension_semantics=("arbitrary",), vmem_limit_bytes=VMEM_LIMIT),
        name="ssd_conv",
    )(proj, proj, proj, proj, proj, proj, proj, proj, proj, wx, bx, wb, bb, wc, bc)


def _softplus(x):
    return jnp.maximum(x, 0.0) + jnp.log(1.0 + jnp.exp(-jnp.abs(x)))


def _ssd_chunk(c0, xs_ref, bm_ref, cm_ref, sm_ref, dtb_row, alog_row, y_ref, st_ref, reverse, dt_off, last_row):
    rows = pl.ds(c0, CHUNK)
    lower, upper = _tri_masks(CHUNK)
    mask = upper if reverse else lower
    tri = jnp.where(mask, 1.0, 0.0).astype(BF16)
    a_row = -jnp.exp(alog_row[...])
    dt = _softplus(sm_ref[rows, dt_off:dt_off + SSD_HEADS] + dtb_row[...])
    cum = _dot_exact_lhs(tri, dt * a_row, 3)
    cum_t = jnp.transpose(cum)
    total = cum[last_row:last_row + 1, :]
    to_end = jnp.exp(total - cum) * dt
    hh = lax.broadcasted_iota(I32, (SSD_HEADS, SSD_DINNER), 0)
    cc = lax.broadcasted_iota(I32, (SSD_HEADS, SSD_DINNER), 1)
    expand = jnp.where(cc // SSD_HEADDIM == hh, 1.0, 0.0).astype(BF16)
    stacked = jnp.concatenate([dt, to_end, jnp.exp(cum), jnp.broadcast_to(jnp.exp(total), (8, SSD_HEADS))], axis=0)
    ex = _dot_exact_rhs(stacked, expand, 2)
    dt_x, te_x, ec_x, et_x = ex[0:CHUNK], ex[CHUNK:2 * CHUNK], ex[2 * CHUNK:3 * CHUNK], ex[3 * CHUNK:3 * CHUNK + 1]
    x = xs_ref[rows, :].astype(F32)
    x_dt = (x * dt_x).astype(BF16)
    x_te = (x * te_x).astype(BF16)
    lane = lax.broadcasted_iota(I32, (CHUNK, 2 * SSD_HEADDIM), 1)
    row2 = lax.broadcasted_iota(I32, (CHUNK, 2 * SSD_HEADDIM), 0)
    left = lane < SSD_HEADDIM
    col2 = jnp.where(left, lane, lane - SSD_HEADDIM)
    mask2 = (col2 >= row2) if reverse else (row2 >= col2)
    for g in range(SSD_GROUPS):
        ns = slice(g * SSD_STATE, (g + 1) * SSD_STATE)
        cs = slice(g * SSD_GW, (g + 1) * SSD_GW)
        bg = bm_ref[rows, ns]
        cg = cm_ref[rows, ns]
        cb = _dot_nt(cg, bg)
        cb2 = jnp.concatenate([cb, cb], axis=1)
        st = st_ref[g]
        y_inter = _dot(cg, st.astype(BF16)) * ec_x[:, cs]
        parts = []
        for p in range(SSD_HPG // 2):
            h0 = g * SSD_HPG + 2 * p
            ps = slice(h0 * SSD_HEADDIM, (h0 + 2) * SSD_HEADDIM)
            col = jnp.where(left, cum[:, h0:h0 + 1], cum[:, h0 + 1:h0 + 2])
            rowv = jnp.concatenate([cum_t[h0:h0 + 1, :], cum_t[h0 + 1:h0 + 2, :]], axis=1)
            decay = jnp.exp(jnp.where(mask2, col - rowv, -jnp.inf))
            w = (cb2 * decay).astype(BF16)
            xp = x_dt[:, ps]
            zero = jnp.zeros_like(xp)
            xbd = jnp.concatenate([jnp.where(left, xp, zero), jnp.where(left, zero, xp)], axis=0)
            parts.append(_dot(w, xbd))
        y = jnp.concatenate(parts, axis=1) + y_inter
        y_ref[rows, cs] = y.astype(y_ref.dtype)
        st_ref[g] = st * et_x[:, cs] + _dot_tn(bg, x_te[:, cs])


def _ssd_kernel(xf, bf, cf, smf, xb, bb, cb, smb, dbf_r, dbb_r, alf_r, alb_r, yf_ref, yb_ref, stf, stb, *, n_chunks):
    @pl.when(pl.program_id(1) == 0)
    def _():
        stf[...] = jnp.zeros_like(stf)
        stb[...] = jnp.zeros_like(stb)

    def body(i, carry):
        c_f = pl.multiple_of(i * CHUNK, CHUNK)
        c_b = pl.multiple_of((n_chunks - 1 - i) * CHUNK, CHUNK)
        _ssd_chunk(c_f, xf, bf, cf, smf, dbf_r, alf_r, yf_ref, stf, False, S_DTF, CHUNK - 1)
        _ssd_chunk(c_b, xb, bb, cb, smb, dbb_r, alb_r, yb_ref, stb, True, S_DTB, 0)
        return carry

    lax.fori_loop(0, n_chunks, body, 0)


def _ssd(xs_c, bm_c, cm_c, small, dtb_f, dtb_b, alog_f, alog_b, bsz, seq, lb):
    t = bsz * seq
    nb = seq // lb
    fi = lambda b, n: b * nb + n
    bi = lambda b, n: b * nb + nb - 1 - n

    def specs(idx):
        return [
            pl.BlockSpec((lb, SSD_DINNER), lambda b, n: (idx(b, n), 0)),
            pl.BlockSpec((lb, SSD_BC), lambda b, n: (idx(b, n), 0)),
            pl.BlockSpec((lb, SSD_BC), lambda b, n: (idx(b, n), 0)),
            pl.BlockSpec((lb, N_SMALL), lambda b, n: (idx(b, n), 0)),
        ]

    row = pl.BlockSpec((1, SSD_HEADS), lambda b, n: (0, 0))
    args = (xs_c, bm_c, cm_c, small)
    return pl.pallas_call(
        functools.partial(_ssd_kernel, n_chunks=lb // CHUNK),
        grid=(bsz, nb),
        in_specs=specs(fi) + specs(bi) + [row] * 4,
        out_specs=[pl.BlockSpec((lb, SSD_DINNER), lambda b, n: (fi(b, n), 0)),
                   pl.BlockSpec((lb, SSD_DINNER), lambda b, n: (bi(b, n), 0))],
        out_shape=[jax.ShapeDtypeStruct((t, SSD_DINNER), BF16)] * 2,
        scratch_shapes=[pltpu.VMEM((SSD_GROUPS, SSD_STATE, SSD_GW), F32)] * 2,
        compiler_params=pltpu.CompilerParams(
            dimension_semantics=("arbitrary", "arbitrary"), vmem_limit_bytes=VMEM_LIMIT),
        name="ssd_scan",
    )(*args, *args, dtb_f[None, :], dtb_b[None, :], alog_f[None, :], alog_b[None, :])


def _merge_kernel(x_ref, of_ref, ob_ref, r_ref, gg_ref, yf_ref, yb_ref, xs_ref, z_ref, gs_ref,
                  gng_ref, dsk_ref, sng_ref, wug_ref, wus_ref, wo_ref, nfg_ref, wr_ref, br_ref,
                  h_ref, xp_ref, route_ref, gate_ref, cnt_out_ref, cnt_ref):
    @pl.when(pl.program_id(0) == 0)
    def _():
        cnt_ref[...] = jnp.zeros_like(cnt_ref)

    tm = x_ref.shape[0]
    o = of_ref[...].astype(F32) + ob_ref[...].astype(F32)
    gng = gng_ref[...]
    o_parts = []
    for h in range(GLA_HEADS):
        oh = o[:, h * GLA_DV:(h + 1) * GLA_DV]
        oh = oh * lax.rsqrt(jnp.mean(oh * oh, axis=-1, keepdims=True) + EPS)
        o_parts.append(oh * gng)
    o = jnp.concatenate(o_parts, axis=1) * _silu(r_ref[...].astype(F32))
    up_g = _dot(o.astype(BF16), wug_ref[...])
    y = yf_ref[...].astype(F32) + yb_ref[...].astype(F32) + dsk_ref[...] * xs_ref[...].astype(F32)
    y = y * _silu(z_ref[...].astype(F32))
    sng = sng_ref[...]
    y_parts = []
    for g in range(SSD_GROUPS):
        yg = y[:, g * SSD_GW:(g + 1) * SSD_GW]
        yg = yg * lax.rsqrt(jnp.mean(yg * yg, axis=-1, keepdims=True) + EPS)
        y_parts.append(yg * sng[:, g * SSD_GW:(g + 1) * SSD_GW])
    y = jnp.concatenate(y_parts, axis=1)
    up_s = _dot(y.astype(BF16), wus_ref[...])
    mix = _sigmoid(gg_ref[...].astype(F32)) * up_g + _sigmoid(gs_ref[...].astype(F32)) * up_s
    h = x_ref[...] + _dot(mix.astype(BF16), wo_ref[...])
    h_ref[...] = h
    hn = (h * lax.rsqrt(jnp.mean(h * h, axis=-1, keepdims=True) + EPS)) * nfg_ref[...]
    xp_ref[...] = _pack_bf16_pair(hn[:, :D_MODEL // 2], hn[:, D_MODEL // 2:])
    logits = jnp.dot(hn, wr_ref[...], preferred_element_type=F32, precision=lax.Precision.HIGHEST) + br_ref[...]
    lane = lax.broadcasted_iota(I32, (tm, N_EXPERTS), 1).astype(F32)
    work = logits
    idxs, vals = [], []
    for _ in range(TOP_K):
        m = jnp.max(work, axis=-1, keepdims=True)
        idx = jnp.min(jnp.where(work == m, lane, float(N_EXPERTS)), axis=-1, keepdims=True)
        idxs.append(idx)
        vals.append(m)
        work = jnp.where(lane == idx, -jnp.inf, work)
    exps = [jnp.exp(v - vals[0]) for v in vals]
    denom = exps[0] + exps[1] + exps[2] + exps[3]
    gates = [e / denom for e in exps]
    sel = jnp.zeros((tm, N_EXPERTS), F32)
    for idx in idxs:
        sel = sel + jnp.where(lane == idx, 1.0, 0.0)
    rr = lax.broadcasted_iota(I32, (tm, tm), 0)
    cc = lax.broadcasted_iota(I32, (tm, tm), 1)
    strict = jnp.where(rr > cc, 1.0, 0.0).astype(BF16)
    pos = _dot(strict, sel.astype(BF16)) + cnt_ref[0:1, :]
    ranks = [jnp.sum(jnp.where(lane == idx, pos, 0.0), axis=-1, keepdims=True).astype(I32) for idx in idxs]
    cnt_new = cnt_ref[0:1, :] + jnp.sum(sel, axis=0, keepdims=True)
    cnt_ref[...] = jnp.broadcast_to(cnt_new, cnt_ref.shape)
    cnt_out_ref[...] = jnp.broadcast_to(cnt_new, cnt_ref.shape).astype(I32)
    lane128 = lax.broadcasted_iota(I32, (tm, 128), 1)
    route = jnp.zeros((tm, 128), I32)
    gate_o = jnp.zeros((tm, 128), F32)
    for k in range(TOP_K):
        route = jnp.where(lane128 == k, idxs[k].astype(I32), route)
        route = jnp.where(lane128 == TOP_K + k, ranks[k], route)
        gate_o = jnp.where(lane128 == k, gates[k], gate_o)
    route_ref[...] = route
    gate_ref[...] = gate_o


def _merge(x2, o_f, o_b, proj, y_f, y_b, xs_c, gla_norm_g, ssd_d, ssd_norm_g, w_up_gla, w_up_ssd, w_out,
           norm_ffn_g, w_router, b_router, tm):
    t = x2.shape[0]
    rowblk = lambda w, col=0: pl.BlockSpec((tm, w), lambda i: (i, col))
    const = lambda a: pl.BlockSpec(a.shape, lambda i: (0,) * a.ndim)
    d_skip = jnp.repeat(ssd_d, SSD_HEADDIM)[None, :]
    params = [gla_norm_g[None, :], d_skip, ssd_norm_g[None, :], w_up_gla.astype(BF16), w_up_ssd.astype(BF16),
              w_out.astype(BF16), norm_ffn_g[None, :], w_router, b_router[None, :]]
    return pl.pallas_call(
        _merge_kernel,
        grid=(t // tm,),
        in_specs=[rowblk(D_MODEL), rowblk(GLA_V), rowblk(GLA_V), rowblk(GLA_V, C_R // GLA_V),
                  rowblk(D_MODEL, C_GG // D_MODEL), rowblk(SSD_DINNER), rowblk(SSD_DINNER), rowblk(SSD_DINNER),
                  rowblk(SSD_DINNER, C_Z // SSD_DINNER), rowblk(D_MODEL, C_GS // D_MODEL)]
        + [const(p) for p in params],
        out_specs=[rowblk(D_MODEL), rowblk(D_MODEL // 2), rowblk(128), rowblk(128),
                   pl.BlockSpec((8, N_EXPERTS), lambda i: (0, 0))],
        out_shape=[jax.ShapeDtypeStruct((t, D_MODEL), F32), jax.ShapeDtypeStruct((t, D_MODEL // 2), U32),
                   jax.ShapeDtypeStruct((t, 128), I32), jax.ShapeDtypeStruct((t, 128), F32),
                   jax.ShapeDtypeStruct((8, N_EXPERTS), I32)],
        scratch_shapes=[pltpu.VMEM((8, N_EXPERTS), F32)],
        compiler_params=pltpu.CompilerParams(dimension_semantics=("arbitrary",), vmem_limit_bytes=VMEM_LIMIT),
        name="merge_router",
    )(x2, o_f, o_b, proj, proj, y_f, y_b, xs_c, proj, proj, *params)


def _dispatch_kernel(pstart_ref, cnt_ref, x_ref, route_ref, xb_ref, zero_ref, sem, *, tm, bm):
    def issue(r, carry):
        for k in range(TOP_K):
            e = route_ref[r * 8 + k]
            d = pstart_ref[e] + route_ref[r * 8 + TOP_K + k]
            pltpu.make_async_copy(x_ref.at[pl.ds(r, 1)], xb_ref.at[pl.ds(d, 1)], sem).start()
        return carry

    lax.fori_loop(0, tm, issue, 0)

    def drain(r, carry):
        pltpu.make_async_copy(x_ref.at[pl.ds(0, 1)], xb_ref.at[pl.ds(0, 1)], sem).wait()
        return carry

    lax.fori_loop(0, tm * TOP_K, drain, 0)

    @pl.when(pl.program_id(0) == pl.num_programs(0) - 1)
    def _():
        zero_ref[...] = jnp.zeros_like(zero_ref)

        def per_expert(e, total):
            n = cnt_ref[e]
            n_pad = (n + bm - 1) // bm * bm
            base = pstart_ref[e]

            def fill(r, c):
                pltpu.make_async_copy(zero_ref.at[pl.ds(0, 1)], xb_ref.at[pl.ds(base + r, 1)], sem).start()
                return c

            lax.fori_loop(n, n_pad, fill, 0)
            return total + (n_pad - n)

        n_fill = lax.fori_loop(0, N_EXPERTS, per_expert, 0)

        def drain_fill(r, c):
            pltpu.make_async_copy(zero_ref.at[pl.ds(0, 1)], xb_ref.at[pl.ds(0, 1)], sem).wait()
            return c

        lax.fori_loop(0, n_fill, drain_fill, 0)


def _dispatch(xp, route_flat, pstart, counts, n_rows, tm):
    t = xp.shape[0]
    return pl.pallas_call(
        functools.partial(_dispatch_kernel, tm=tm, bm=MOE_BM),
        grid_spec=pltpu.PrefetchScalarGridSpec(
            num_scalar_prefetch=2,
            grid=(t // tm,),
            in_specs=[pl.BlockSpec((tm, D_MODEL // 2), lambda i, ps, cn: (i, 0)),
                      pl.BlockSpec((tm * 8,), lambda i, ps, cn: (i,), memory_space=pltpu.SMEM)],
            out_specs=pl.BlockSpec(memory_space=pl.ANY),
            scratch_shapes=[pltpu.VMEM((8, D_MODEL // 2), U32), pltpu.SemaphoreType.DMA(())],
        ),
        out_shape=jax.ShapeDtypeStruct((n_rows, D_MODEL // 2), U32),
        compiler_params=pltpu.CompilerParams(dimension_semantics=("arbitrary",)),
        name="moe_dispatch",
    )(pstart, counts, xp, route_flat)


def _expert_kernel(blk_e_ref, nused_ref, x_ref, w1_ref, b1_ref, w2_ref, b2_ref, o_ref):
    i = pl.program_id(0)

    @pl.when(i < nused_ref[0])
    def _():
        lo, hi = _unpack_bf16_pair(x_ref[...])
        x = jnp.concatenate([lo, hi], axis=1).astype(BF16)
        hdn = _dot(x, w1_ref[0]) + b1_ref[0]
        gate = jnp.minimum(hdn[:, :D_FF], SWIGLU_LIMIT)
        lin = jnp.clip(hdn[:, D_FF:], -SWIGLU_LIMIT, SWIGLU_LIMIT)
        act = gate * _sigmoid(SWIGLU_ALPHA * gate) * (lin + 1.0)
        y = _dot(act.astype(BF16), w2_ref[0]) + b2_ref[0]
        o_ref[...] = _pack_bf16_pair(y[:, :D_MODEL // 2], y[:, D_MODEL // 2:])

    @pl.when(i >= nused_ref[0])
    def _():
        o_ref[...] = jnp.zeros_like(o_ref)


def _experts(xb, blk_e, n_used, w1, b1, w2, b2):
    n_rows = xb.shape[0]
    n_blocks = n_rows // MOE_BM
    xidx = lambda i, be, nu: (jnp.minimum(i, nu[0] - 1), 0)
    eidx = lambda i, be, nu: (be[i], 0, 0)
    return pl.pallas_call(
        _expert_kernel,
        grid_spec=pltpu.PrefetchScalarGridSpec(
            num_scalar_prefetch=2,
            grid=(n_blocks,),
            in_specs=[pl.BlockSpec((MOE_BM, D_MODEL // 2), xidx),
                      pl.BlockSpec((1, D_MODEL, 2 * D_FF), eidx),
                      pl.BlockSpec((1, 1, 2 * D_FF), eidx),
                      pl.BlockSpec((1, D_FF, D_MODEL), eidx),
                      pl.BlockSpec((1, 1, D_MODEL), eidx)],
            out_specs=pl.BlockSpec((MOE_BM, D_MODEL // 2), lambda i, be, nu: (i, 0)),
        ),
        out_shape=jax.ShapeDtypeStruct((n_rows, D_MODEL // 2), U32),
        compiler_params=pltpu.CompilerParams(dimension_semantics=("arbitrary",), vmem_limit_bytes=VMEM_LIMIT),
        name="moe_experts",
    )(blk_e, n_used, xb, w1, b1, w2, b2)


def _combine_kernel(pstart_ref, h_ref, gate_ref, route_ref, g_ref, yb_ref, o_ref, buf_ref, sem, *, tm):
    def issue(r, carry):
        for k in range(TOP_K):
            e = route_ref[r * 8 + k]
            d = pstart_ref[e] + route_ref[r * 8 + TOP_K + k]
            pltpu.make_async_copy(yb_ref.at[pl.ds(d, 1)], buf_ref.at[k, pl.ds(r, 1)], sem).start()
        return carry

    lax.fori_loop(0, tm, issue, 0)

    def drain(r, carry):
        pltpu.make_async_copy(yb_ref.at[pl.ds(0, 1)], buf_ref.at[0, pl.ds(0, 1)], sem).wait()
        return carry

    lax.fori_loop(0, tm * TOP_K, drain, 0)

    gates = gate_ref[...]
    acc_lo = jnp.zeros((tm, D_MODEL // 2), F32)
    acc_hi = jnp.zeros((tm, D_MODEL // 2), F32)
    for k in range(TOP_K):
        lo, hi = _unpack_bf16_pair(buf_ref[k])
        gk = gates[:, k:k + 1]
        acc_lo = acc_lo + gk * lo
        acc_hi = acc_hi + gk * hi
    h = h_ref[...] + jnp.concatenate([acc_lo, acc_hi], axis=1)
    o_ref[...] = (h * lax.rsqrt(jnp.mean(h * h, axis=-1, keepdims=True) + EPS)) * g_ref[...]


def _combine(h, gates, route_flat, pstart, yb, norm_final_g, tm):
    t = h.shape[0]
    return pl.pallas_call(
        functools.partial(_combine_kernel, tm=tm),
        grid_spec=pltpu.PrefetchScalarGridSpec(
            num_scalar_prefetch=1,
            grid=(t // tm,),
            in_specs=[pl.BlockSpec((tm, D_MODEL), lambda i, ps: (i, 0)),
                      pl.BlockSpec((tm, 128), lambda i, ps: (i, 0)),
                      pl.BlockSpec((tm * 8,), lambda i, ps: (i,), memory_space=pltpu.SMEM),
                      pl.BlockSpec((1, D_MODEL), lambda i, ps: (0, 0)),
                      pl.BlockSpec(memory_space=pl.ANY)],
            out_specs=pl.BlockSpec((tm, D_MODEL), lambda i, ps: (i, 0)),
            scratch_shapes=[pltpu.VMEM((TOP_K, tm, D_MODEL // 2), U32), pltpu.SemaphoreType.DMA(())],
        ),
        out_shape=jax.ShapeDtypeStruct((t, D_MODEL), F32),
        compiler_params=pltpu.CompilerParams(dimension_semantics=("arbitrary",), vmem_limit_bytes=VMEM_LIMIT),
        name="moe_combine",
    )(pstart, h, gates, route_flat, norm_final_g[None, :], yb)


def _pick(n, pref):
    b = min(pref, n)
    while n % b:
        b -= CHUNK
    return b


def _prep_w_in(w_in):
    widths = (GLA_QK, GLA_QK, GLA_V, GLA_V, GLA_RANK, GLA_RANK, SSD_DINNER, SSD_DINNER, SSD_BC, SSD_BC,
              2 * SSD_HEADS, D_MODEL, D_MODEL)
    pts, acc = [], 0
    for w in widths[:-1]:
        acc += w
        pts.append(acc)
    q, k, v, r, lrf, lrb, z, xs, bm, cm, dtr, gg, gs = jnp.split(w_in, pts, axis=1)
    main = jnp.concatenate([z, xs, q, k, v, r, gg, gs, bm, cm], axis=1).astype(BF16)
    pad = jnp.zeros((D_MODEL, N_SMALL - 2 * GLA_RANK - 2 * SSD_HEADS), w_in.dtype)
    small = jnp.concatenate([lrf, lrb, dtr, pad], axis=1).astype(BF16)
    return main, small


def _layer(h, norm_mix_g, w_in, gla_fw2_f, gla_fb_f, gla_fw2_b, gla_fb_b, gla_norm_g, conv_w, conv_b,
           dt_bias_f, dt_bias_b, a_log_f, a_log_b, ssd_d, ssd_norm_g, w_up_gla, w_up_ssd, w_out,
           norm_ffn_g, w_router, b_router, w1, b1, w2, b2, out_norm_g):
    bsz, seq, _ = h.shape
    t = bsz * seq
    x2 = h.reshape(t, D_MODEL)
    w_main, w_small = _prep_w_in(w_in)
    proj, small = _inproj(x2, norm_mix_g[None, :], w_main, w_small, tm=_pick(t, 1024), tn=1024)
    lb = _pick(seq, 512)
    o_f, o_b = _gla(proj, small, gla_fw2_f.astype(BF16), gla_fb_f[None, :], gla_fw2_b.astype(BF16),
                    gla_fb_b[None, :], bsz, seq, lb)
    xs_c, bm_c, cm_c = _conv(proj, conv_w, conv_b, seq, _pick(seq, 512))
    y_f, y_b = _ssd(xs_c, bm_c, cm_c, small, dt_bias_f, dt_bias_b, a_log_f, a_log_b, bsz, seq, lb)
    hres, xp, route, gates, counts8 = _merge(x2, o_f, o_b, proj, y_f, y_b, xs_c, gla_norm_g, ssd_d, ssd_norm_g,
                                             w_up_gla, w_up_ssd, w_out, norm_ffn_g, w_router, b_router,
                                             tm=_pick(t, 256))
    counts = counts8[0]
    padded = (counts + MOE_BM - 1) // MOE_BM * MOE_BM
    pend = jnp.cumsum(padded)
    pstart = (pend - padded).astype(I32)
    n_rows = t * TOP_K + N_EXPERTS * MOE_BM
    n_blocks = n_rows // MOE_BM
    blk_e = jnp.minimum(jnp.searchsorted(pend, jnp.arange(n_blocks, dtype=I32) * MOE_BM, side='right'),
                        N_EXPERTS - 1).astype(I32)
    n_used = (pend[-1:] // MOE_BM).astype(I32)
    route_flat = route[:, :8].reshape(t * 8)
    xb = _dispatch(xp, route_flat, pstart, counts, n_rows, tm=_pick(t, 256))
    yb = _experts(xb, blk_e, n_used, w1.astype(BF16), b1[:, None, :], w2.astype(BF16), b2[:, None, :])
    out = _combine(hres, gates, route_flat, pstart, yb, out_norm_g, tm=_pick(t, 128))
    return out.reshape(bsz, seq, D_MODEL)


def kernel(x, norm_mix_g, w_in, gla_fw2_f, gla_fb_f, gla_fw2_b, gla_fb_b, gla_norm_g, conv_w, conv_b, dt_bias_f,
           dt_bias_b, a_log_f, a_log_b, ssd_d, ssd_norm_g, w_up_gla, w_up_ssd, w_out, norm_ffn_g, w_router,
           b_router, w1, b1, w2, b2, norm_final_g):
    assert x.shape[-1] == D_MODEL and norm_mix_g.shape[0] == 1
    return _layer(x, norm_mix_g[0], w_in[0], gla_fw2_f[0], gla_fb_f[0], gla_fw2_b[0], gla_fb_b[0], gla_norm_g[0],
                  conv_w[0], conv_b[0], dt_bias_f[0], dt_bias_b[0], a_log_f[0], a_log_b[0], ssd_d[0],
                  ssd_norm_g[0], w_up_gla[0], w_up_ssd[0], w_out[0], norm_ffn_g[0], w_router[0], b_router[0],
                  w1[0], b1[0], w2[0], b2[0], norm_final_g)
```

```python
import functools

import jax
import jax.numpy as jnp
from jax import lax
from jax.experimental import pallas as pl
from jax.experimental.pallas import tpu as pltpu
from jax.experimental.pallas import tpu_sc as plsc

F32 = jnp.float32
BF16 = jnp.bfloat16
I32 = jnp.int32
U32 = jnp.uint32

D_MODEL = 1024
EPS = 1e-5
GLA_HEADS = 4
GLA_DK = 128
GLA_DV = 256
GLA_RANK = 16
GLA_TAU = 16.0
GLA_QK = GLA_HEADS * GLA_DK
GLA_V = GLA_HEADS * GLA_DV
SSD_DINNER = 2048
SSD_HEADDIM = 64
SSD_HEADS = 32
SSD_GROUPS = 4
SSD_HPG = 8
SSD_STATE = 128
SSD_CONV = 4
SSD_BC = SSD_GROUPS * SSD_STATE
SSD_GW = SSD_HPG * SSD_HEADDIM
N_EXPERTS = 32
TOP_K = 4
D_FF = 1024
SWIGLU_LIMIT = 7.0
SWIGLU_ALPHA = 1.702
CHUNK = 64

C_Z, C_XS, C_Q, C_K, C_V, C_R, C_GG, C_GS, C_B, C_C = 0, 2048, 4096, 4608, 5120, 6144, 7168, 8192, 9216, 9728
N_MAIN = 10240
N_SMALL = 128
S_LRF, S_LRB, S_DTF, S_DTB = 0, 16, 32, 64

VMEM_LIMIT = 56 * 1024 * 1024
MOE_BM = 256
SC_WIN = 128
SC_W = D_MODEL // 4


def _dot(a, b):
    return jnp.dot(a, b, preferred_element_type=F32)


def _dot_nt(a, b):
    return lax.dot_general(a, b, (((1,), (1,)), ((), ())), preferred_element_type=F32)


def _dot_tn(a, b):
    return lax.dot_general(a, b, (((0,), (0,)), ((), ())), preferred_element_type=F32)


def _split_bf16(x, n):
    parts = []
    r = x
    for _ in range(n):
        p = r.astype(BF16)
        parts.append(p)
        r = r - p.astype(F32)
    return parts


def _dot_exact_lhs(m_bf16, x, n):
    acc = None
    for p in _split_bf16(x, n):
        t = _dot(m_bf16, p)
        acc = t if acc is None else acc + t
    return acc


def _dot_exact_rhs(x, m_bf16, n):
    acc = None
    for p in _split_bf16(x, n):
        t = _dot(p, m_bf16)
        acc = t if acc is None else acc + t
    return acc


def _sigmoid(x):
    return 1.0 / (1.0 + jnp.exp(-x))


def _silu(x):
    return x * _sigmoid(x)


def _pack_bf16_pair(lo, hi):
    lo_b = lax.bitcast_convert_type(lo.astype(BF16).astype(F32), U32)
    hi_b = lax.bitcast_convert_type(hi.astype(BF16).astype(F32), U32)
    return (lo_b >> 16) | (hi_b & jnp.uint32(0xFFFF0000))


def _unpack_bf16_pair(u):
    lo = lax.bitcast_convert_type(u << 16, F32)
    hi = lax.bitcast_convert_type(u & jnp.uint32(0xFFFF0000), F32)
    return lo, hi


def _inproj_kernel(x_ref, g_ref, w_ref, ws_ref, o_ref, os_ref, xn_ref):
    @pl.when(pl.program_id(1) == 0)
    def _():
        x = x_ref[...]
        ms = jnp.mean(x * x, axis=-1, keepdims=True)
        xb = ((x * lax.rsqrt(ms + EPS)) * g_ref[...]).astype(BF16)
        xn_ref[...] = xb
        os_ref[...] = _dot(xb, ws_ref[...])

    o_ref[...] = _dot(xn_ref[...], w_ref[...]).astype(BF16)


def _inproj(x2, g, w_main, w_small, tm, tn):
    t = x2.shape[0]
    return pl.pallas_call(
        _inproj_kernel,
        grid=(t // tm, N_MAIN // tn),
        in_specs=[
            pl.BlockSpec((tm, D_MODEL), lambda i, j: (i, 0)),
            pl.BlockSpec((1, D_MODEL), lambda i, j: (0, 0)),
            pl.BlockSpec((D_MODEL, tn), lambda i, j: (0, j)),
            pl.BlockSpec((D_MODEL, N_SMALL), lambda i, j: (0, 0)),
        ],
        out_specs=[
            pl.BlockSpec((tm, tn), lambda i, j: (i, j)),
            pl.BlockSpec((tm, N_SMALL), lambda i, j: (i, 0)),
        ],
        out_shape=[
            jax.ShapeDtypeStruct((t, N_MAIN), BF16),
            jax.ShapeDtypeStruct((t, N_SMALL), F32),
        ],
        scratch_shapes=[pltpu.VMEM((tm, D_MODEL), BF16)],
        compiler_params=pltpu.CompilerParams(
            dimension_semantics=("arbitrary", "arbitrary"), vmem_limit_bytes=VMEM_LIMIT),
        name="inproj",
    )(x2, g, w_main, w_small)


def _tri_masks(n):
    r = lax.broadcasted_iota(I32, (n, n), 0)
    c = lax.broadcasted_iota(I32, (n, n), 1)
    return r >= c, c >= r


def _gla_chunk(c0, q_ref, k_ref, v_ref, sm_ref, fw2_ref, fb_ref, o_ref, st_ref, mask, lr_off, mid_row, last_row):
    rows = pl.ds(c0, CHUNK)
    tri = jnp.where(mask, 1.0, 0.0).astype(BF16)
    lr = sm_ref[rows, lr_off:lr_off + GLA_RANK].astype(BF16)
    xg = _dot(lr, fw2_ref[...]) + fb_ref[...]
    g = (jnp.minimum(xg, 0.0) - jnp.log(1.0 + jnp.exp(-jnp.abs(xg)))) * (1.0 / GLA_TAU)
    b = _dot_exact_lhs(tri, g, 2)
    b_mid = b[mid_row:mid_row + 1, :]
    b_last = b[last_row:last_row + 1, :]
    q = q_ref[rows, :].astype(F32) * (GLA_DK ** -0.5)
    k = k_ref[rows, :].astype(F32)
    v = v_ref[rows, :]
    qs = (q * jnp.exp(b - b_mid)).astype(BF16)
    ks = (k * jnp.exp(b_mid - b)).astype(BF16)
    qe = (q * jnp.exp(b)).astype(BF16)
    kd = (k * jnp.exp(b_last - b)).astype(BF16)
    e_last = jnp.exp(b_last)
    for h in range(GLA_HEADS):
        ks_ = slice(h * GLA_DK, (h + 1) * GLA_DK)
        vs_ = slice(h * GLA_DV, (h + 1) * GLA_DV)
        s = jnp.where(mask, _dot_nt(qs[:, ks_], ks[:, ks_]), 0.0).astype(BF16)
        st = st_ref[h]
        o = _dot(s, v[:, vs_]) + _dot(qe[:, ks_], st.astype(BF16))
        o_ref[rows, vs_] = o.astype(o_ref.dtype)
        e_col = jnp.transpose(jnp.broadcast_to(e_last[:, ks_], (8, GLA_DK)))[:, 0:1]
        st_ref[h] = st * e_col + _dot_tn(kd[:, ks_], v[:, vs_])


def _gla_kernel(qf, kf, vf, smf, qb, kb, vb, smb, fw2f, fbf, fw2b, fbb, of_ref, ob_ref, stf, stb, *, n_chunks):
    @pl.when(pl.program_id(1) == 0)
    def _():
        stf[...] = jnp.zeros_like(stf)
        stb[...] = jnp.zeros_like(stb)

    lower, upper = _tri_masks(CHUNK)

    def body(i, carry):
        cf = pl.multiple_of(i * CHUNK, CHUNK)
        cb = pl.multiple_of((n_chunks - 1 - i) * CHUNK, CHUNK)
        _gla_chunk(cf, qf, kf, vf, smf, fw2f, fbf, of_ref, stf, lower, S_LRF, CHUNK // 2, CHUNK - 1)
        _gla_chunk(cb, qb, kb, vb, smb, fw2b, fbb, ob_ref, stb, upper, S_LRB, CHUNK // 2 - 1, 0)
        return carry

    lax.fori_loop(0, n_chunks, body, 0)


def _gla(proj, small, fw2f, fbf, fw2b, fbb, bsz, seq, lb):
    t = bsz * seq
    nb = seq // lb

    def fwd(w, col):
        return pl.BlockSpec((lb, w), lambda b, n: (b * nb + n, col))

    def bwd(w, col):
        return pl.BlockSpec((lb, w), lambda b, n: (b * nb + nb - 1 - n, col))

    const = lambda shape: pl.BlockSpec(shape, lambda b, n: (0, 0))
    return pl.pallas_call(
        functools.partial(_gla_kernel, n_chunks=lb // CHUNK),
        grid=(bsz, nb),
        in_specs=[
            fwd(GLA_QK, C_Q // GLA_QK), fwd(GLA_QK, C_K // GLA_QK), fwd(GLA_V, C_V // GLA_V), fwd(N_SMALL, 0),
            bwd(GLA_QK, C_Q // GLA_QK), bwd(GLA_QK, C_K // GLA_QK), bwd(GLA_V, C_V // GLA_V), bwd(N_SMALL, 0),
            const((GLA_RANK, GLA_QK)), const((1, GLA_QK)), const((GLA_RANK, GLA_QK)), const((1, GLA_QK)),
        ],
        out_specs=[fwd(GLA_V, 0), bwd(GLA_V, 0)],
        out_shape=[jax.ShapeDtypeStruct((t, GLA_V), BF16)] * 2,
        scratch_shapes=[pltpu.VMEM((GLA_HEADS, GLA_DK, GLA_DV), F32)] * 2,
        compiler_params=pltpu.CompilerParams(
            dimension_semantics=("arbitrary", "arbitrary"), vmem_limit_bytes=VMEM_LIMIT),
        name="gla_scan",
    )(proj, proj, proj, small, proj, proj, proj, small, fw2f, fbf, fw2b, fbb)


HALO = 16


def _conv_one(x_ref, p_ref, n_ref, w_ref, b_ref, o_ref, has_prev, has_next):
    x = x_ref[...].astype(F32)
    rb = x.shape[0]
    row = lax.broadcasted_iota(I32, x.shape, 0)
    prev = jnp.where(has_prev, p_ref[HALO - 1:HALO, :].astype(F32), 0.0)
    nxt = jnp.where(has_next, n_ref[0:2, :].astype(F32), 0.0)
    xm1 = jnp.where(row == 0, prev, pltpu.roll(x, 1, 0))
    xp1 = jnp.where(row == rb - 1, nxt[0:1, :], pltpu.roll(x, rb - 1, 0))
    xp2 = jnp.where(row == rb - 2, nxt[0:1, :], jnp.where(row == rb - 1, nxt[1:2, :], pltpu.roll(x, rb - 2, 0)))
    w = w_ref[...]
    y = xm1 * w[0:1, :] + x * w[1:2, :] + xp1 * w[2:3, :] + xp2 * w[3:4, :] + b_ref[...]
    o_ref[...] = _silu(y).astype(o_ref.dtype)


def _conv_kernel(xs, xsp, xsn, bm, bmp, bmn, cm, cmp_, cmn, wx, bx, wb, bb, wc, bc, oxs, obm, ocm, *, rb, seq):
    t0 = pl.program_id(0) * rb
    has_prev = (t0 % seq) != 0
    has_next = ((t0 + rb) % seq) != 0
    _conv_one(xs, xsp, xsn, wx, bx, oxs, has_prev, has_next)
    _conv_one(bm, bmp, bmn, wb, bb, obm, has_prev, has_next)
    _conv_one(cm, cmp_, cmn, wc, bc, ocm, has_prev, has_next)


def _conv(proj, conv_w, conv_b, seq, rb):
    t = proj.shape[0]
    nh = t // HALO
    per = rb // HALO

    def trio(w, col):
        cb = col // w
        return [
            pl.BlockSpec((rb, w), lambda i: (i, cb)),
            pl.BlockSpec((HALO, w), lambda i: (jnp.maximum(i * per - 1, 0), cb)),
            pl.BlockSpec((HALO, w), lambda i: (jnp.minimum((i + 1) * per, nh - 1), cb)),
        ]

    def wspecs(w):
        return [pl.BlockSpec((SSD_CONV, w), lambda i: (0, 0)), pl.BlockSpec((1, w), lambda i: (0, 0))]

    wx, wb, wc = conv_w[:, :SSD_DINNER], conv_w[:, SSD_DINNER:SSD_DINNER + SSD_BC], conv_w[:, SSD_DINNER + SSD_BC:]
    bx, bb, bc = (conv_b[None, :SSD_DINNER], conv_b[None, SSD_DINNER:SSD_DINNER + SSD_BC],
                  conv_b[None, SSD_DINNER + SSD_BC:])
    return pl.pallas_call(
        functools.partial(_conv_kernel, rb=rb, seq=seq),
        grid=(t // rb,),
        in_specs=trio(SSD_DINNER, C_XS) + trio(SSD_BC, C_B) + trio(SSD_BC, C_C)
        + wspecs(SSD_DINNER) + wspecs(SSD_BC) + wspecs(SSD_BC),
        out_specs=[pl.BlockSpec((rb, SSD_DINNER), lambda i: (i, 0)),
                   pl.BlockSpec((rb, SSD_BC), lambda i: (i, 0)),
                   pl.BlockSpec((rb, SSD_BC), lambda i: (i, 0))],
        out_shape=[jax.ShapeDtypeStruct((t, SSD_DINNER), BF16),
                   jax.ShapeDtypeStruct((t, SSD_BC), BF16),
                   jax.ShapeDtypeStruct((t, SSD_BC), BF16)],
        compiler_params=pltpu.CompilerParams(dimension_semantics=("arbitrary",), vmem_limit_bytes=VMEM_LIMIT),
        name="ssd_conv",
    )(proj, proj, proj, proj, proj, proj, proj, proj, proj, wx, bx, wb, bb, wc, bc)


def _softplus(x):
    return jnp.maximum(x, 0.0) + jnp.log(1.0 + jnp.exp(-jnp.abs(x)))


def _ssd_chunk(c0, xs_ref, bm_ref, cm_ref, sm_ref, dtb_row, alog_row, y_ref, st_ref, reverse, dt_off, last_row):
    rows = pl.ds(c0, CHUNK)
    lower, upper = _tri_masks(CHUNK)
    mask = upper if reverse else lower
    tri = jnp.where(mask, 1.0, 0.0).astype(BF16)
    a_row = -jnp.exp(alog_row[...])
    dt = _softplus(sm_ref[rows, dt_off:dt_off + SSD_HEADS] + dtb_row[...])
    cum = _dot_exact_lhs(tri, dt * a_row, 3)
    cum_t = jnp.transpose(cum)
    dt_t = jnp.transpose(dt)
    total = cum[last_row:last_row + 1, :]
    to_end = jnp.exp(total - cum) * dt
    fac = jnp.concatenate([to_end, jnp.exp(cum)], axis=0).astype(BF16)
    e_tot = jnp.broadcast_to(jnp.exp(total), (8, SSD_HEADS))
    hh = lax.broadcasted_iota(I32, (SSD_HEADS, SSD_GW), 0)
    cc = lax.broadcasted_iota(I32, (SSD_HEADS, SSD_GW), 1)
    lane = lax.broadcasted_iota(I32, (CHUNK, 2 * SSD_HEADDIM), 1)
    row2 = lax.broadcasted_iota(I32, (CHUNK, 2 * SSD_HEADDIM), 0)
    left = lane < SSD_HEADDIM
    col2 = jnp.where(left, lane, lane - SSD_HEADDIM)
    mask2 = (col2 >= row2) if reverse else (row2 >= col2)
    for g in range(SSD_GROUPS):
        ns = slice(g * SSD_STATE, (g + 1) * SSD_STATE)
        cs = slice(g * SSD_GW, (g + 1) * SSD_GW)
        expand = jnp.where(cc // SSD_HEADDIM + g * SSD_HPG == hh, 1.0, 0.0).astype(BF16)
        fac_x = _dot(fac, expand)
        te_x, ec_x = fac_x[0:CHUNK].astype(BF16), fac_x[CHUNK:2 * CHUNK]
        et_x = _dot_exact_rhs(e_tot, expand, 2)[0:1]
        bg = bm_ref[rows, ns]
        cg = cm_ref[rows, ns]
        cb = _dot_nt(cg, bg)
        cb2 = jnp.concatenate([cb, cb], axis=1)
        st = st_ref[g]
        y_inter = _dot(cg, st.astype(BF16)) * ec_x
        parts = []
        for p in range(SSD_HPG // 2):
            h0 = g * SSD_HPG + 2 * p
            ps = slice(h0 * SSD_HEADDIM, (h0 + 2) * SSD_HEADDIM)
            col = jnp.where(left, cum[:, h0:h0 + 1], cum[:, h0 + 1:h0 + 2])
            rowv = jnp.concatenate([cum_t[h0:h0 + 1, :], cum_t[h0 + 1:h0 + 2, :]], axis=1)
            dtv = jnp.concatenate([dt_t[h0:h0 + 1, :], dt_t[h0 + 1:h0 + 2, :]], axis=1)
            decay = jnp.exp(jnp.where(mask2, col - rowv, -jnp.inf))
            w = (cb2 * decay * dtv).astype(BF16)
            xp = xs_ref[rows, ps]
            zero = jnp.zeros_like(xp)
            xbd = jnp.concatenate([jnp.where(left, xp, zero), jnp.where(left, zero, xp)], axis=0)
            parts.append(_dot(w, xbd))
        y = jnp.concatenate(parts, axis=1) + y_inter
        y_ref[rows, cs] = y.astype(y_ref.dtype)
        st_ref[g] = st * et_x + _dot_tn(bg, xs_ref[rows, cs] * te_x)


def _ssd_kernel(xf, bf, cf, smf, xb, bb, cb, smb, dbf_r, dbb_r, alf_r, alb_r, yf_ref, yb_ref, stf, stb, *, n_chunks):
    @pl.when(pl.program_id(1) == 0)
    def _():
        stf[...] = jnp.zeros_like(stf)
        stb[...] = jnp.zeros_like(stb)

    def body(i, carry):
        c_f = pl.multiple_of(i * CHUNK, CHUNK)
        c_b = pl.multiple_of((n_chunks - 1 - i) * CHUNK, CHUNK)
        _ssd_chunk(c_f, xf, bf, cf, smf, dbf_r, alf_r, yf_ref, stf, False, S_DTF, CHUNK - 1)
        _ssd_chunk(c_b, xb, bb, cb, smb, dbb_r, alb_r, yb_ref, stb, True, S_DTB, 0)
        return carry

    lax.fori_loop(0, n_chunks, body, 0)


def _ssd(xs_c, bm_c, cm_c, small, dtb_f, dtb_b, alog_f, alog_b, bsz, seq, lb):
    t = bsz * seq
    nb = seq // lb
    fi = lambda b, n: b * nb + n
    bi = lambda b, n: b * nb + nb - 1 - n

    def specs(idx):
        return [
            pl.BlockSpec((lb, SSD_DINNER), lambda b, n: (idx(b, n), 0)),
            pl.BlockSpec((lb, SSD_BC), lambda b, n: (idx(b, n), 0)),
            pl.BlockSpec((lb, SSD_BC), lambda b, n: (idx(b, n), 0)),
            pl.BlockSpec((lb, N_SMALL), lambda b, n: (idx(b, n), 0)),
        ]

    row = pl.BlockSpec((1, SSD_HEADS), lambda b, n: (0, 0))
    args = (xs_c, bm_c, cm_c, small)
    return pl.pallas_call(
        functools.partial(_ssd_kernel, n_chunks=lb // CHUNK),
        grid=(bsz, nb),
        in_specs=specs(fi) + specs(bi) + [row] * 4,
        out_specs=[pl.BlockSpec((lb, SSD_DINNER), lambda b, n: (fi(b, n), 0)),
                   pl.BlockSpec((lb, SSD_DINNER), lambda b, n: (bi(b, n), 0))],
        out_shape=[jax.ShapeDtypeStruct((t, SSD_DINNER), BF16)] * 2,
        scratch_shapes=[pltpu.VMEM((SSD_GROUPS, SSD_STATE, SSD_GW), F32)] * 2,
        compiler_params=pltpu.CompilerParams(
            dimension_semantics=("arbitrary", "arbitrary"), vmem_limit_bytes=VMEM_LIMIT),
        name="ssd_scan",
    )(*args, *args, dtb_f[None, :], dtb_b[None, :], alog_f[None, :], alog_b[None, :])


def _merge_kernel(x_ref, of_ref, ob_ref, r_ref, gg_ref, yf_ref, yb_ref, xs_ref, z_ref, gs_ref,
                  gng_ref, dsk_ref, sng_ref, wug_ref, wus_ref, wo_ref, nfg_ref, wr_ref, br_ref,
                  h_ref, xp_ref, route_ref, gate_ref, cnt_out_ref, cnt_ref):
    @pl.when(pl.program_id(0) == 0)
    def _():
        cnt_ref[...] = jnp.zeros_like(cnt_ref)

    tm = x_ref.shape[0]
    o = of_ref[...].astype(F32) + ob_ref[...].astype(F32)
    gng = gng_ref[...]
    o_parts = []
    for h in range(GLA_HEADS):
        oh = o[:, h * GLA_DV:(h + 1) * GLA_DV]
        oh = oh * lax.rsqrt(jnp.mean(oh * oh, axis=-1, keepdims=True) + EPS)
        o_parts.append(oh * gng)
    o = jnp.concatenate(o_parts, axis=1) * _silu(r_ref[...].astype(F32))
    up_g = _dot(o.astype(BF16), wug_ref[...])
    y = yf_ref[...].astype(F32) + yb_ref[...].astype(F32) + dsk_ref[...] * xs_ref[...].astype(F32)
    y = y * _silu(z_ref[...].astype(F32))
    sng = sng_ref[...]
    y_parts = []
    for g in range(SSD_GROUPS):
        yg = y[:, g * SSD_GW:(g + 1) * SSD_GW]
        yg = yg * lax.rsqrt(jnp.mean(yg * yg, axis=-1, keepdims=True) + EPS)
        y_parts.append(yg * sng[:, g * SSD_GW:(g + 1) * SSD_GW])
    y = jnp.concatenate(y_parts, axis=1)
    up_s = _dot(y.astype(BF16), wus_ref[...])
    mix = _sigmoid(gg_ref[...].astype(F32)) * up_g + _sigmoid(gs_ref[...].astype(F32)) * up_s
    h = x_ref[...] + _dot(mix.astype(BF16), wo_ref[...])
    h_ref[...] = h
    hn = (h * lax.rsqrt(jnp.mean(h * h, axis=-1, keepdims=True) + EPS)) * nfg_ref[...]
    words = _pack_bf16_pair(hn[:, :D_MODEL // 2], hn[:, D_MODEL // 2:])
    xp_ref[0] = words[:, :SC_W]
    xp_ref[1] = words[:, SC_W:]
    logits = jnp.dot(hn, wr_ref[...], preferred_element_type=F32, precision=lax.Precision.HIGHEST) + br_ref[...]
    lane = lax.broadcasted_iota(I32, (tm, N_EXPERTS), 1).astype(F32)
    work = logits
    idxs, vals = [], []
    for _ in range(TOP_K):
        m = jnp.max(work, axis=-1, keepdims=True)
        idx = jnp.min(jnp.where(work == m, lane, float(N_EXPERTS)), axis=-1, keepdims=True)
        idxs.append(idx)
        vals.append(m)
        work = jnp.where(lane == idx, -jnp.inf, work)
    exps = [jnp.exp(v - vals[0]) for v in vals]
    denom = exps[0] + exps[1] + exps[2] + exps[3]
    gates = [e / denom for e in exps]
    sel = jnp.zeros((tm, N_EXPERTS), F32)
    for idx in idxs:
        sel = sel + jnp.where(lane == idx, 1.0, 0.0)
    rr = lax.broadcasted_iota(I32, (tm, tm), 0)
    cc = lax.broadcasted_iota(I32, (tm, tm), 1)
    strict = jnp.where(rr > cc, 1.0, 0.0).astype(BF16)
    pos = _dot(strict, sel.astype(BF16)) + cnt_ref[0:1, :]
    ranks = [jnp.sum(jnp.where(lane == idx, pos, 0.0), axis=-1, keepdims=True).astype(I32) for idx in idxs]
    cnt_new = cnt_ref[0:1, :] + jnp.sum(sel, axis=0, keepdims=True)
    cnt_ref[...] = jnp.broadcast_to(cnt_new, cnt_ref.shape)
    cnt_out_ref[...] = jnp.broadcast_to(cnt_new, cnt_ref.shape).astype(I32)
    lane128 = lax.broadcasted_iota(I32, (tm, 128), 1)
    route = jnp.zeros((tm, 128), I32)
    gate_o = jnp.zeros((tm, 128), F32)
    for k in range(TOP_K):
        route = jnp.where(lane128 == k, idxs[k].astype(I32), route)
        route = jnp.where(lane128 == TOP_K + k, ranks[k], route)
        gate_o = jnp.where(lane128 == k, gates[k], gate_o)
    route_ref[...] = route
    gate_ref[...] = gate_o


def _merge(x2, o_f, o_b, proj, y_f, y_b, xs_c, gla_norm_g, ssd_d, ssd_norm_g, w_up_gla, w_up_ssd, w_out,
           norm_ffn_g, w_router, b_router, tm):
    t = x2.shape[0]
    rowblk = lambda w, col=0: pl.BlockSpec((tm, w), lambda i: (i, col))
    const = lambda a: pl.BlockSpec(a.shape, lambda i: (0,) * a.ndim)
    d_skip = jnp.repeat(ssd_d, SSD_HEADDIM)[None, :]
    params = [gla_norm_g[None, :], d_skip, ssd_norm_g[None, :], w_up_gla.astype(BF16), w_up_ssd.astype(BF16),
              w_out.astype(BF16), norm_ffn_g[None, :], w_router, b_router[None, :]]
    return pl.pallas_call(
        _merge_kernel,
        grid=(t // tm,),
        in_specs=[rowblk(D_MODEL), rowblk(GLA_V), rowblk(GLA_V), rowblk(GLA_V, C_R // GLA_V),
                  rowblk(D_MODEL, C_GG // D_MODEL), rowblk(SSD_DINNER), rowblk(SSD_DINNER), rowblk(SSD_DINNER),
                  rowblk(SSD_DINNER, C_Z // SSD_DINNER), rowblk(D_MODEL, C_GS // D_MODEL)]
        + [const(p) for p in params],
        out_specs=[rowblk(D_MODEL), pl.BlockSpec((2, tm, SC_W), lambda i: (0, i, 0)), rowblk(128), rowblk(128),
                   pl.BlockSpec((8, N_EXPERTS), lambda i: (0, 0))],
        out_shape=[jax.ShapeDtypeStruct((t, D_MODEL), F32), jax.ShapeDtypeStruct((2, t, SC_W), U32),
                   jax.ShapeDtypeStruct((t, 128), I32), jax.ShapeDtypeStruct((t, 128), F32),
                   jax.ShapeDtypeStruct((8, N_EXPERTS), I32)],
        scratch_shapes=[pltpu.VMEM((8, N_EXPERTS), F32)],
        compiler_params=pltpu.CompilerParams(dimension_semantics=("arbitrary",), vmem_limit_bytes=VMEM_LIMIT),
        name="merge_router",
    )(x2, o_f, o_b, proj, proj, y_f, y_b, xs_c, proj, proj, *params)


def _sc_mesh():
    return plsc.VectorSubcoreMesh(core_axis_name="c", subcore_axis_name="s")


def _sc_scatter_rows(x, idx, n_out):
    n, m = x.shape[0], idx.shape[1]
    n_win = n // SC_WIN

    @functools.partial(pl.kernel, out_type=jax.ShapeDtypeStruct((n_out, SC_W), x.dtype), mesh=_sc_mesh())
    def scatter(x_hbm, i_hbm, o_hbm):
        def body(x_vmem, i_vmem):
            pltpu.sync_copy(x_vmem, o_hbm.at[i_vmem.at[0]])

        pltpu.emit_pipeline(
            body, grid=(m // SC_WIN,),
            in_specs=[pl.BlockSpec((SC_WIN, SC_W), lambda i: (i % n_win, 0)),
                      pl.BlockSpec((1, SC_WIN), lambda i: (0, i))],
            out_specs=[], core_axis_name=("c", "s"), dimension_semantics=(pltpu.PARALLEL,),
        )(x_hbm, i_hbm)

    return scatter(x, idx)


def _sc_gather_rows(table, idx):
    m = idx.shape[1]

    @functools.partial(pl.kernel, out_type=jax.ShapeDtypeStruct((m, SC_W), table.dtype), mesh=_sc_mesh())
    def gather(t_hbm, i_hbm, o_hbm):
        def body(i_vmem, o_vmem):
            pltpu.sync_copy(t_hbm.at[i_vmem.at[0]], o_vmem)

        pltpu.emit_pipeline(
            body, grid=(m // SC_WIN,),
            in_specs=[pl.BlockSpec((1, SC_WIN), lambda i: (0, i))],
            out_specs=[pl.BlockSpec((SC_WIN, SC_W), lambda i: (i, 0))],
            core_axis_name=("c", "s"), dimension_semantics=(pltpu.PARALLEL,),
        )(i_hbm, o_hbm)

    return gather(table, idx)


def _expert_kernel(blk_e_ref, nvalid_ref, nused_ref, x_ref, w1_ref, b1_ref, w2_ref, b2_ref, o_ref):
    del blk_e_ref, nused_ref
    i = pl.program_id(0)
    nvalid = nvalid_ref[i]

    @pl.when(nvalid > 0)
    def _():
        live = lax.broadcasted_iota(I32, (MOE_BM, SC_W), 0) < nvalid
        lo0, hi0 = _unpack_bf16_pair(jnp.where(live, x_ref[0], jnp.uint32(0)))
        lo1, hi1 = _unpack_bf16_pair(jnp.where(live, x_ref[1], jnp.uint32(0)))
        x = jnp.concatenate([lo0, lo1, hi0, hi1], axis=1).astype(BF16)
        hdn = _dot(x, w1_ref[0]) + b1_ref[0]
        gate = jnp.minimum(hdn[:, :D_FF], SWIGLU_LIMIT)
        lin = jnp.clip(hdn[:, D_FF:], -SWIGLU_LIMIT, SWIGLU_LIMIT)
        act = gate * _sigmoid(SWIGLU_ALPHA * gate) * (lin + 1.0)
        y = _dot(act.astype(BF16), w2_ref[0]) + b2_ref[0]
        words = _pack_bf16_pair(y[:, :D_MODEL // 2], y[:, D_MODEL // 2:])
        o_ref[0] = words[:, :SC_W]
        o_ref[1] = words[:, SC_W:]

    @pl.when(nvalid == 0)
    def _():
        o_ref[...] = jnp.zeros_like(o_ref)


def _experts(xb, blk_e, nvalid, n_used, w1, b1, w2, b2):
    n_rows = xb.shape[1]
    n_blocks = n_rows // MOE_BM
    xidx = lambda i, be, nv, nu: (0, jnp.minimum(i, nu[0] - 1), 0)
    eidx = lambda i, be, nv, nu: (be[i], 0, 0)
    return pl.pallas_call(
        _expert_kernel,
        grid_spec=pltpu.PrefetchScalarGridSpec(
            num_scalar_prefetch=3,
            grid=(n_blocks,),
            in_specs=[pl.BlockSpec((2, MOE_BM, SC_W), xidx),
                      pl.BlockSpec((1, D_MODEL, 2 * D_FF), eidx),
                      pl.BlockSpec((1, 1, 2 * D_FF), eidx),
                      pl.BlockSpec((1, D_FF, D_MODEL), eidx),
                      pl.BlockSpec((1, 1, D_MODEL), eidx)],
            out_specs=pl.BlockSpec((2, MOE_BM, SC_W), lambda i, be, nv, nu: (0, i, 0)),
        ),
        out_shape=jax.ShapeDtypeStruct((2, n_rows, SC_W), U32),
        compiler_params=pltpu.CompilerParams(dimension_semantics=("arbitrary",), vmem_limit_bytes=VMEM_LIMIT),
        name="moe_experts",
    )(blk_e, nvalid, n_used, xb, w1, b1, w2, b2)


def _combine_kernel(h_ref, gate_ref, y_ref, g_ref, o_ref):
    gates = gate_ref[...]
    acc = None
    for k in range(TOP_K):
        lo0, hi0 = _unpack_bf16_pair(y_ref[k, 0])
        lo1, hi1 = _unpack_bf16_pair(y_ref[k, 1])
        term = gates[:, k:k + 1] * jnp.concatenate([lo0, lo1, hi0, hi1], axis=1)
        acc = term if acc is None else acc + term
    h = h_ref[...] + acc
    o_ref[...] = (h * lax.rsqrt(jnp.mean(h * h, axis=-1, keepdims=True) + EPS)) * g_ref[...]


def _combine(h, gates, y_rows, norm_final_g, tm):
    t = h.shape[0]
    return pl.pallas_call(
        _combine_kernel,
        grid=(t // tm,),
        in_specs=[pl.BlockSpec((tm, D_MODEL), lambda i: (i, 0)),
                  pl.BlockSpec((tm, 128), lambda i: (i, 0)),
                  pl.BlockSpec((TOP_K, 2, tm, SC_W), lambda i: (0, 0, i, 0)),
                  pl.BlockSpec((1, D_MODEL), lambda i: (0, 0))],
        out_specs=pl.BlockSpec((tm, D_MODEL), lambda i: (i, 0)),
        out_shape=jax.ShapeDtypeStruct((t, D_MODEL), F32),
        compiler_params=pltpu.CompilerParams(dimension_semantics=("arbitrary",), vmem_limit_bytes=VMEM_LIMIT),
        name="moe_combine",
    )(h, gates, y_rows, norm_final_g[None, :])


def _pick(n, pref):
    b = min(pref, n)
    while n % b:
        b -= CHUNK
    return b


def _prep_w_in(w_in):
    widths = (GLA_QK, GLA_QK, GLA_V, GLA_V, GLA_RANK, GLA_RANK, SSD_DINNER, SSD_DINNER, SSD_BC, SSD_BC,
              2 * SSD_HEADS, D_MODEL, D_MODEL)
    pts, acc = [], 0
    for w in widths[:-1]:
        acc += w
        pts.append(acc)
    q, k, v, r, lrf, lrb, z, xs, bm, cm, dtr, gg, gs = jnp.split(w_in, pts, axis=1)
    main = jnp.concatenate([z, xs, q, k, v, r, gg, gs, bm, cm], axis=1).astype(BF16)
    pad = jnp.zeros((D_MODEL, N_SMALL - 2 * GLA_RANK - 2 * SSD_HEADS), w_in.dtype)
    small = jnp.concatenate([lrf, lrb, dtr, pad], axis=1).astype(BF16)
    return main, small


def _layer(h, norm_mix_g, w_in, gla_fw2_f, gla_fb_f, gla_fw2_b, gla_fb_b, gla_norm_g, conv_w, conv_b,
           dt_bias_f, dt_bias_b, a_log_f, a_log_b, ssd_d, ssd_norm_g, w_up_gla, w_up_ssd, w_out,
           norm_ffn_g, w_router, b_router, w1, b1, w2, b2, out_norm_g):
    bsz, seq, _ = h.shape
    t = bsz * seq
    x2 = h.reshape(t, D_MODEL)
    w_main, w_small = _prep_w_in(w_in)
    proj, small = _inproj(x2, norm_mix_g[None, :], w_main, w_small, tm=_pick(t, 1024), tn=1024)
    lb = _pick(seq, 512)
    o_f, o_b = _gla(proj, small, gla_fw2_f.astype(BF16), gla_fb_f[None, :], gla_fw2_b.astype(BF16),
                    gla_fb_b[None, :], bsz, seq, lb)
    xs_c, bm_c, cm_c = _conv(proj, conv_w, conv_b, seq, _pick(seq, 512))
    y_f, y_b = _ssd(xs_c, bm_c, cm_c, small, dt_bias_f, dt_bias_b, a_log_f, a_log_b, bsz, seq, lb)
    hres, xp, route, gates, counts8 = _merge(x2, o_f, o_b, proj, y_f, y_b, xs_c, gla_norm_g, ssd_d, ssd_norm_g,
                                             w_up_gla, w_up_ssd, w_out, norm_ffn_g, w_router, b_router,
                                             tm=_pick(t, 256))
    counts = counts8[0]
    padded = (counts + MOE_BM - 1) // MOE_BM * MOE_BM
    pend = jnp.cumsum(padded)
    pstart = (pend - padded).astype(I32)
    n_rows = t * TOP_K + N_EXPERTS * MOE_BM
    blk_row = jnp.arange(n_rows // MOE_BM, dtype=I32) * MOE_BM
    blk_e = jnp.minimum(jnp.sum(pend[None, :] <= blk_row[:, None], axis=1), N_EXPERTS - 1).astype(I32)
    nvalid = jnp.clip(pstart[blk_e] + counts[blk_e] - blk_row, 0, MOE_BM).astype(I32)
    n_used = (pend[-1:] // MOE_BM).astype(I32)
    top_e, rank = route[:, :TOP_K], route[:, TOP_K:2 * TOP_K]
    dest = jnp.sum(jnp.where(top_e[:, :, None] == jnp.arange(N_EXPERTS, dtype=I32), pstart, 0), axis=-1) + rank
    idx = (dest.T[:, None, :] + (jnp.arange(2, dtype=I32) * n_rows)[None, :, None]).reshape(1, 2 * TOP_K * t)
    xb = _sc_scatter_rows(xp.reshape(2 * t, SC_W), idx, 2 * n_rows).reshape(2, n_rows, SC_W)
    yb = _experts(xb, blk_e, nvalid, n_used, w1.astype(BF16), b1[:, None, :], w2.astype(BF16), b2[:, None, :])
    y_rows = _sc_gather_rows(yb.reshape(2 * n_rows, SC_W), idx).reshape(TOP_K, 2, t, SC_W)
    out = _combine(hres, gates, y_rows, out_norm_g, tm=_pick(t, 256))
    return out.reshape(bsz, seq, D_MODEL)


def kernel(x, norm_mix_g, w_in, gla_fw2_f, gla_fb_f, gla_fw2_b, gla_fb_b, gla_norm_g, conv_w, conv_b, dt_bias_f,
           dt_bias_b, a_log_f, a_log_b, ssd_d, ssd_norm_g, w_up_gla, w_up_ssd, w_out, norm_ffn_g, w_router,
           b_router, w1, b1, w2, b2, norm_final_g):
    assert x.shape[-1] == D_MODEL and norm_mix_g.shape[0] == 1
    return _layer(x, norm_mix_g[0], w_in[0], gla_fw2_f[0], gla_fb_f[0], gla_fw2_b[0], gla_fb_b[0], gla_norm_g[0],
                  conv_w[0], conv_b[0], dt_bias_f[0], dt_bias_b[0], a_log_f[0], a_log_b[0], ssd_d[0],
                  ssd_norm_g[0], w_up_gla[0], w_up_ssd[0], w_out[0], norm_ffn_g[0], w_router[0], b_router[0],
                  w1[0], b1[0], w2[0], b2[0], norm_final_g)
```

```python
import functools

import jax
import jax.numpy as jnp
from jax import lax
from jax.experimental import pallas as pl
from jax.experimental.pallas import tpu as pltpu
from jax.experimental.pallas import tpu_sc as plsc

F32 = jnp.float32
BF16 = jnp.bfloat16
I32 = jnp.int32
U32 = jnp.uint32

D_MODEL = 1024
EPS = 1e-5
GLA_HEADS = 4
GLA_DK = 128
GLA_DV = 256
GLA_RANK = 16
GLA_TAU = 16.0
GLA_QK = GLA_HEADS * GLA_DK
GLA_V = GLA_HEADS * GLA_DV
SSD_DINNER = 2048
SSD_HEADDIM = 64
SSD_HEADS = 32
SSD_GROUPS = 4
SSD_HPG = 8
SSD_STATE = 128
SSD_CONV = 4
SSD_BC = SSD_GROUPS * SSD_STATE
SSD_GW = SSD_HPG * SSD_HEADDIM
N_EXPERTS = 32
TOP_K = 4
D_FF = 1024
SWIGLU_LIMIT = 7.0
SWIGLU_ALPHA = 1.702
CHUNK = 64
SCAN_UNROLL = 2
MXU_LAG = 3

C_Z, C_XS, C_Q, C_K, C_V, C_R, C_GG, C_GS, C_B, C_C = 0, 2048, 4096, 4608, 5120, 6144, 7168, 8192, 9216, 9728
N_MAIN = 10240
N_SMALL = 128
S_LRF, S_LRB, S_DTF, S_DTB = 0, 16, 32, 64

VMEM_LIMIT = 56 * 1024 * 1024
MOE_BM = 512
SC_WIN = 128
SC_W = D_MODEL // 4


def _dot(a, b):
    return jnp.dot(a, b, preferred_element_type=F32)


def _dot_nt(a, b):
    return lax.dot_general(a, b, (((1,), (1,)), ((), ())), preferred_element_type=F32)


def _dot_tn(a, b):
    return lax.dot_general(a, b, (((0,), (0,)), ((), ())), preferred_element_type=F32)


def _split_bf16(x, n):
    parts = []
    r = x
    for _ in range(n):
        p = r.astype(BF16)
        parts.append(p)
        r = r - p.astype(F32)
    return parts


def _dot_exact_lhs(m_bf16, x, n):
    acc = None
    for p in _split_bf16(x, n):
        t = _dot(m_bf16, p)
        acc = t if acc is None else acc + t
    return acc


def _dot_exact_rhs(x, m_bf16, n):
    acc = None
    for p in _split_bf16(x, n):
        t = _dot(p, m_bf16)
        acc = t if acc is None else acc + t
    return acc


def _sigmoid(x):
    return 1.0 / (1.0 + jnp.exp(-x))


def _silu(x):
    return x * _sigmoid(x)


def _pack_bf16_pair(lo, hi):
    lo_b = lax.bitcast_convert_type(lo.astype(BF16).astype(F32), U32)
    hi_b = lax.bitcast_convert_type(hi.astype(BF16).astype(F32), U32)
    return (lo_b >> 16) | (hi_b & jnp.uint32(0xFFFF0000))


def _unpack_bf16_pair(u):
    lo = lax.bitcast_convert_type(u << 16, F32)
    hi = lax.bitcast_convert_type(u & jnp.uint32(0xFFFF0000), F32)
    return lo, hi


def _inproj_kernel(x_ref, g_ref, w_ref, ws_ref, o_ref, os_ref, xn_ref):
    @pl.when(pl.program_id(1) == 0)
    def _():
        x = x_ref[...]
        ms = jnp.mean(x * x, axis=-1, keepdims=True)
        xb = ((x * lax.rsqrt(ms + EPS)) * g_ref[...]).astype(BF16)
        xn_ref[...] = xb
        os_ref[...] = _dot(xb, ws_ref[...])

    o_ref[...] = _dot(xn_ref[...], w_ref[...]).astype(BF16)


def _inproj(x2, g, w_main, w_small, tm, tn):
    t = x2.shape[0]
    return pl.pallas_call(
        _inproj_kernel,
        grid=(t // tm, N_MAIN // tn),
        in_specs=[
            pl.BlockSpec((tm, D_MODEL), lambda i, j: (i, 0)),
            pl.BlockSpec((1, D_MODEL), lambda i, j: (0, 0)),
            pl.BlockSpec((D_MODEL, tn), lambda i, j: (0, j)),
            pl.BlockSpec((D_MODEL, N_SMALL), lambda i, j: (0, 0)),
        ],
        out_specs=[
            pl.BlockSpec((tm, tn), lambda i, j: (i, j)),
            pl.BlockSpec((tm, N_SMALL), lambda i, j: (i, 0)),
        ],
        out_shape=[
            jax.ShapeDtypeStruct((t, N_MAIN), BF16),
            jax.ShapeDtypeStruct((t, N_SMALL), F32),
        ],
        scratch_shapes=[pltpu.VMEM((tm, D_MODEL), BF16)],
        compiler_params=pltpu.CompilerParams(
            dimension_semantics=("arbitrary", "arbitrary"), vmem_limit_bytes=VMEM_LIMIT),
        name="inproj",
    )(x2, g, w_main, w_small)


def _tri_masks(n):
    r = lax.broadcasted_iota(I32, (n, n), 0)
    c = lax.broadcasted_iota(I32, (n, n), 1)
    return r >= c, c >= r


class _GlaDir:
    def __init__(self, ins, outs, scratch, mask, lr_off, mid_row, last_row):
        self.q, self.k, self.v, self.sm, self.fw2, self.fb = ins
        self.o, self.st = outs
        self.b_s, self.qs_s, self.ks_s, self.kd_s, self.qe_s, self.oi_s, self.u_s, self.el_s = scratch
        self.mask, self.lr_off, self.mid_row, self.last_row = mask, lr_off, mid_row, last_row


def _gla_decay(d, n_chunks):
    tri = jnp.where(d.mask, 1.0, 0.0).astype(BF16)
    lr = d.sm[:, d.lr_off:d.lr_off + GLA_RANK].astype(BF16)
    xg = _dot(lr, d.fw2[...]) + d.fb[...]
    d.b_s[...] = (jnp.minimum(xg, 0.0) - jnp.log(1.0 + jnp.exp(-jnp.abs(xg)))) * (1.0 / GLA_TAU)
    for c in range(n_chunks):
        rows = slice(c * CHUNK, (c + 1) * CHUNK)
        d.b_s[rows, :] = _dot_exact_lhs(tri, d.b_s[rows, :], 2)


def _gla_scale(d, c):
    rows = slice(c * CHUNK, (c + 1) * CHUNK)
    b = d.b_s[rows, :]
    b_mid = b[d.mid_row:d.mid_row + 1, :]
    b_last = b[d.last_row:d.last_row + 1, :]
    q = d.q[rows, :].astype(F32) * (GLA_DK ** -0.5)
    k = d.k[rows, :].astype(F32)
    d.qs_s[rows, :] = (q * jnp.exp(b - b_mid)).astype(BF16)
    d.ks_s[rows, :] = (k * jnp.exp(b_mid - b)).astype(BF16)
    d.kd_s[rows, :] = (k * jnp.exp(b_last - b)).astype(BF16)
    d.qe_s[rows, :] = (q * jnp.exp(b)).astype(BF16)
    d.el_s[c:c + 1, :] = jnp.exp(b_last)


def _gla_local(units):
    def score(u):
        d, c, h = u
        rows, ks_ = slice(c * CHUNK, (c + 1) * CHUNK), slice(h * GLA_DK, (h + 1) * GLA_DK)
        return _dot_nt(d.qs_s[rows, ks_], d.ks_s[rows, ks_])

    def finish(u, s):
        d, c, h = u
        rows, ks_ = slice(c * CHUNK, (c + 1) * CHUNK), slice(h * GLA_DK, (h + 1) * GLA_DK)
        vs_ = slice(h * GLA_DV, (h + 1) * GLA_DV)
        p = jnp.where(d.mask, s, 0.0).astype(BF16)
        d.oi_s[rows, vs_] = _dot(p, d.v[rows, vs_])
        d.u_s[c, h] = _dot_tn(d.kd_s[rows, ks_], d.v[rows, vs_])

    pending = []
    for u in units:
        pending.append((u, score(u)))
        if len(pending) > MXU_LAG:
            finish(*pending.pop(0))
    for item in pending:
        finish(*item)


def _gla_carry(d, c):
    rows = slice(c * CHUNK, (c + 1) * CHUNK)
    for h in range(GLA_HEADS):
        ks_ = slice(h * GLA_DK, (h + 1) * GLA_DK)
        vs_ = slice(h * GLA_DV, (h + 1) * GLA_DV)
        st = d.st[h]
        d.o[rows, vs_] = (d.oi_s[rows, vs_] + _dot(d.qe_s[rows, ks_], st.astype(BF16))).astype(d.o.dtype)
        e_col = jnp.transpose(jnp.broadcast_to(d.el_s[c:c + 1, ks_], (8, GLA_DK)))[:, 0:1]
        d.st[h] = st * e_col + d.u_s[c, h]


def _gla_kernel(*refs, n_chunks):
    ins_f, ins_b, (fw2f, fbf, fw2b, fbb), (of_ref, ob_ref, stf, stb) = refs[0:4], refs[4:8], refs[8:12], refs[12:16]
    scr_f, scr_b = refs[16:24], refs[24:32]

    @pl.when(pl.program_id(1) == 0)
    def _():
        stf[...] = jnp.zeros_like(stf)
        stb[...] = jnp.zeros_like(stb)

    lower, upper = _tri_masks(CHUNK)
    fwd = _GlaDir((*ins_f, fw2f, fbf), (of_ref, stf), scr_f, lower, S_LRF, CHUNK // 2, CHUNK - 1)
    bwd = _GlaDir((*ins_b, fw2b, fbb), (ob_ref, stb), scr_b, upper, S_LRB, CHUNK // 2 - 1, 0)
    _gla_decay(fwd, n_chunks)
    _gla_decay(bwd, n_chunks)
    for c in range(n_chunks):
        _gla_scale(fwd, c)
        _gla_scale(bwd, c)
    _gla_local([(d, c, h) for c in range(n_chunks) for d in (fwd, bwd) for h in range(GLA_HEADS)])
    for i in range(n_chunks):
        _gla_carry(fwd, i)
        _gla_carry(bwd, n_chunks - 1 - i)


def _gla(proj, small, fw2f, fbf, fw2b, fbb, bsz, seq, lb):
    t = bsz * seq
    nb = seq // lb

    def fwd(w, col):
        return pl.BlockSpec((lb, w), lambda b, n: (b * nb + n, col))

    def bwd(w, col):
        return pl.BlockSpec((lb, w), lambda b, n: (b * nb + nb - 1 - n, col))

    const = lambda shape: pl.BlockSpec(shape, lambda b, n: (0, 0))
    per_dir_scratch = ([pltpu.VMEM((lb, GLA_QK), F32)] + [pltpu.VMEM((lb, GLA_QK), BF16)] * 4
                       + [pltpu.VMEM((lb, GLA_V), F32),
                          pltpu.VMEM((lb // CHUNK, GLA_HEADS, GLA_DK, GLA_DV), F32),
                          pltpu.VMEM((lb // CHUNK, GLA_QK), F32)])
    return pl.pallas_call(
        functools.partial(_gla_kernel, n_chunks=lb // CHUNK),
        grid=(bsz, nb),
        in_specs=[
            fwd(GLA_QK, C_Q // GLA_QK), fwd(GLA_QK, C_K // GLA_QK), fwd(GLA_V, C_V // GLA_V), fwd(N_SMALL, 0),
            bwd(GLA_QK, C_Q // GLA_QK), bwd(GLA_QK, C_K // GLA_QK), bwd(GLA_V, C_V // GLA_V), bwd(N_SMALL, 0),
            const((GLA_RANK, GLA_QK)), const((1, GLA_QK)), const((GLA_RANK, GLA_QK)), const((1, GLA_QK)),
        ],
        out_specs=[fwd(GLA_V, 0), bwd(GLA_V, 0)],
        out_shape=[jax.ShapeDtypeStruct((t, GLA_V), BF16)] * 2,
        scratch_shapes=[pltpu.VMEM((GLA_HEADS, GLA_DK, GLA_DV), F32)] * 2 + per_dir_scratch * 2,
        compiler_params=pltpu.CompilerParams(
            dimension_semantics=("arbitrary", "arbitrary"), vmem_limit_bytes=VMEM_LIMIT),
        name="gla_scan",
    )(proj, proj, proj, small, proj, proj, proj, small, fw2f, fbf, fw2b, fbb)


HALO = 16


def _conv_one(x_ref, p_ref, n_ref, w_ref, b_ref, o_ref, has_prev, has_next):
    x = x_ref[...].astype(F32)
    rb = x.shape[0]
    row = lax.broadcasted_iota(I32, x.shape, 0)
    prev = jnp.where(has_prev, p_ref[HALO - 1:HALO, :].astype(F32), 0.0)
    nxt = jnp.where(has_next, n_ref[0:2, :].astype(F32), 0.0)
    xm1 = jnp.where(row == 0, prev, pltpu.roll(x, 1, 0))
    xp1 = jnp.where(row == rb - 1, nxt[0:1, :], pltpu.roll(x, rb - 1, 0))
    xp2 = jnp.where(row == rb - 2, nxt[0:1, :], jnp.where(row == rb - 1, nxt[1:2, :], pltpu.roll(x, rb - 2, 0)))
    w = w_ref[...]
    y = xm1 * w[0:1, :] + x * w[1:2, :] + xp1 * w[2:3, :] + xp2 * w[3:4, :] + b_ref[...]
    o_ref[...] = _silu(y).astype(o_ref.dtype)


def _conv_kernel(xs, xsp, xsn, bm, bmp, bmn, cm, cmp_, cmn, wx, bx, wb, bb, wc, bc, oxs, obm, ocm, *, rb, seq):
    t0 = pl.program_id(0) * rb
    has_prev = (t0 % seq) != 0
    has_next = ((t0 + rb) % seq) != 0
    _conv_one(xs, xsp, xsn, wx, bx, oxs, has_prev, has_next)
    _conv_one(bm, bmp, bmn, wb, bb, obm, has_prev, has_next)
    _conv_one(cm, cmp_, cmn, wc, bc, ocm, has_prev, has_next)


def _conv(proj, conv_w, conv_b, seq, rb):
    t = proj.shape[0]
    nh = t // HALO
    per = rb // HALO

    def trio(w, col):
        cb = col // w
        return [
            pl.BlockSpec((rb, w), lambda i: (i, cb)),
            pl.BlockSpec((HALO, w), lambda i: (jnp.maximum(i * per - 1, 0), cb)),
            pl.BlockSpec((HALO, w), lambda i: (jnp.minimum((i + 1) * per, nh - 1), cb)),
        ]

    def wspecs(w):
        return [pl.BlockSpec((SSD_CONV, w), lambda i: (0, 0)), pl.BlockSpec((1, w), lambda i: (0, 0))]

    wx, wb, wc = conv_w[:, :SSD_DINNER], conv_w[:, SSD_DINNER:SSD_DINNER + SSD_BC], conv_w[:, SSD_DINNER + SSD_BC:]
    bx, bb, bc = (conv_b[None, :SSD_DINNER], conv_b[None, SSD_DINNER:SSD_DINNER + SSD_BC],
                  conv_b[None, SSD_DINNER + SSD_BC:])
    return pl.pallas_call(
        functools.partial(_conv_kernel, rb=rb, seq=seq),
        grid=(t // rb,),
        in_specs=trio(SSD_DINNER, C_XS) + trio(SSD_BC, C_B) + trio(SSD_BC, C_C)
        + wspecs(SSD_DINNER) + wspecs(SSD_BC) + wspecs(SSD_BC),
        out_specs=[pl.BlockSpec((rb, SSD_DINNER), lambda i: (i, 0)),
                   pl.BlockSpec((rb, SSD_BC), lambda i: (i, 0)),
                   pl.BlockSpec((rb, SSD_BC), lambda i: (i, 0))],
        out_shape=[jax.ShapeDtypeStruct((t, SSD_DINNER), BF16),
                   jax.ShapeDtypeStruct((t, SSD_BC), BF16),
                   jax.ShapeDtypeStruct((t, SSD_BC), BF16)],
        compiler_params=pltpu.CompilerParams(dimension_semantics=("arbitrary",), vmem_limit_bytes=VMEM_LIMIT),
        name="ssd_conv",
    )(proj, proj, proj, proj, proj, proj, proj, proj, proj, wx, bx, wb, bb, wc, bc)


def _softplus(x):
    return jnp.maximum(x, 0.0) + jnp.log(1.0 + jnp.exp(-jnp.abs(x)))


class _SsdDir:
    def __init__(self, xs, bm, cm, sm, dtb_row, alog_row, y, st, reverse, dt_off, last_row):
        self.xs, self.bm, self.cm, self.sm, self.dtb_row, self.alog_row = xs, bm, cm, sm, dtb_row, alog_row
        self.y, self.st, self.reverse, self.dt_off, self.last_row = y, st, reverse, dt_off, last_row


def _ssd_chunks(work):
    hh = lax.broadcasted_iota(I32, (SSD_HEADS, SSD_GW), 0)
    cc = lax.broadcasted_iota(I32, (SSD_HEADS, SSD_GW), 1)
    lane = lax.broadcasted_iota(I32, (CHUNK, 2 * SSD_HEADDIM), 1)
    row2 = lax.broadcasted_iota(I32, (CHUNK, 2 * SSD_HEADDIM), 0)
    left = lane < SSD_HEADDIM
    col2 = jnp.where(left, lane, lane - SSD_HEADDIM)
    lower, upper = _tri_masks(CHUNK)

    pre = []
    for d, c0 in work:
        rows = pl.ds(c0, CHUNK)
        tri = jnp.where(upper if d.reverse else lower, 1.0, 0.0).astype(BF16)
        a_row = -jnp.exp(d.alog_row[...])
        dt = _softplus(d.sm[rows, d.dt_off:d.dt_off + SSD_HEADS] + d.dtb_row[...])
        pre.append((dt, _dot_exact_lhs(tri, dt * a_row, 3)))
    heads = []
    for (d, c0), (dt, cum) in zip(work, pre):
        total = cum[d.last_row:d.last_row + 1, :]
        to_end = jnp.exp(total - cum) * dt
        fac = jnp.concatenate([to_end, jnp.exp(cum)], axis=0).astype(BF16)
        e_tot = jnp.broadcast_to(jnp.exp(total), (8, SSD_HEADS))
        heads.append((cum, jnp.transpose(cum), jnp.transpose(dt), fac, e_tot))

    units = [(w, g) for w in range(len(work)) for g in range(SSD_GROUPS)]
    groups = {}
    for w, g in units:
        d, c0 = work[w]
        rows, ns = pl.ds(c0, CHUNK), slice(g * SSD_STATE, (g + 1) * SSD_STATE)
        fac, e_tot = heads[w][3], heads[w][4]
        expand = jnp.where(cc // SSD_HEADDIM + g * SSD_HPG == hh, 1.0, 0.0).astype(BF16)
        fac_x = _dot(fac, expand)
        et_x = _dot_exact_rhs(e_tot, expand, 2)[0:1]
        groups[w, g] = (fac_x, et_x, _dot_nt(d.cm[rows, ns], d.bm[rows, ns]))

    for w, g in units:
        d, c0 = work[w]
        rows, ns, cs = pl.ds(c0, CHUNK), slice(g * SSD_STATE, (g + 1) * SSD_STATE), slice(g * SSD_GW, (g + 1) * SSD_GW)
        cum, cum_t, dt_t = heads[w][0:3]
        fac_x, _, cb = groups[w, g]
        mask2 = (col2 >= row2) if d.reverse else (row2 >= col2)
        y_inter = _dot(d.cm[rows, ns], d.st[g].astype(BF16))
        cb2 = jnp.concatenate([cb, cb], axis=1)
        parts = []
        for p in range(SSD_HPG // 2):
            h0 = g * SSD_HPG + 2 * p
            ps = slice(h0 * SSD_HEADDIM, (h0 + 2) * SSD_HEADDIM)
            col = jnp.where(left, cum[:, h0:h0 + 1], cum[:, h0 + 1:h0 + 2])
            rowv = jnp.concatenate([cum_t[h0:h0 + 1, :], cum_t[h0 + 1:h0 + 2, :]], axis=1)
            dtv = jnp.concatenate([dt_t[h0:h0 + 1, :], dt_t[h0 + 1:h0 + 2, :]], axis=1)
            decay = jnp.exp(jnp.where(mask2, col - rowv, -jnp.inf))
            wgt = (cb2 * decay * dtv).astype(BF16)
            xp = d.xs[rows, ps]
            zero = jnp.zeros_like(xp)
            xbd = jnp.concatenate([jnp.where(left, xp, zero), jnp.where(left, zero, xp)], axis=0)
            parts.append(_dot(wgt, xbd))
        y = jnp.concatenate(parts, axis=1) + y_inter * fac_x[CHUNK:2 * CHUNK]
        d.y[rows, cs] = y.astype(d.y.dtype)

    def increment(w, g):
        d, c0 = work[w]
        rows, ns, cs = pl.ds(c0, CHUNK), slice(g * SSD_STATE, (g + 1) * SSD_STATE), slice(g * SSD_GW, (g + 1) * SSD_GW)
        return _dot_tn(d.bm[rows, ns], d.xs[rows, cs] * groups[w, g][0][0:CHUNK].astype(BF16))

    def update(w, g, inc):
        d = work[w][0]
        d.st[g] = d.st[g] * groups[w, g][1] + inc

    pending = None
    for w, g in units:
        inc = increment(w, g)
        if pending is not None:
            update(*pending)
        pending = (w, g, inc)
    update(*pending)


def _ssd_kernel(xf, bf, cf, smf, xb, bb, cb, smb, dbf_r, dbb_r, alf_r, alb_r, yf_ref, yb_ref, stf, stb, *, n_chunks):
    @pl.when(pl.program_id(1) == 0)
    def _():
        stf[...] = jnp.zeros_like(stf)
        stb[...] = jnp.zeros_like(stb)

    fwd = _SsdDir(xf, bf, cf, smf, dbf_r, alf_r, yf_ref, stf, False, S_DTF, CHUNK - 1)
    bwd = _SsdDir(xb, bb, cb, smb, dbb_r, alb_r, yb_ref, stb, True, S_DTB, 0)

    def body(i, carry):
        c_f = pl.multiple_of(i * CHUNK, CHUNK)
        c_b = pl.multiple_of((n_chunks - 1 - i) * CHUNK, CHUNK)
        _ssd_chunks([(fwd, c_f), (bwd, c_b)])
        return carry

    lax.fori_loop(0, n_chunks, body, 0)


def _ssd(xs_c, bm_c, cm_c, small, dtb_f, dtb_b, alog_f, alog_b, bsz, seq, lb):
    t = bsz * seq
    nb = seq // lb
    fi = lambda b, n: b * nb + n
    bi = lambda b, n: b * nb + nb - 1 - n

    def specs(idx):
        return [
            pl.BlockSpec((lb, SSD_DINNER), lambda b, n: (idx(b, n), 0)),
            pl.BlockSpec((lb, SSD_BC), lambda b, n: (idx(b, n), 0)),
            pl.BlockSpec((lb, SSD_BC), lambda b, n: (idx(b, n), 0)),
            pl.BlockSpec((lb, N_SMALL), lambda b, n: (idx(b, n), 0)),
        ]

    row = pl.BlockSpec((1, SSD_HEADS), lambda b, n: (0, 0))
    args = (xs_c, bm_c, cm_c, small)
    return pl.pallas_call(
        functools.partial(_ssd_kernel, n_chunks=lb // CHUNK),
        grid=(bsz, nb),
        in_specs=specs(fi) + specs(bi) + [row] * 4,
        out_specs=[pl.BlockSpec((lb, SSD_DINNER), lambda b, n: (fi(b, n), 0)),
                   pl.BlockSpec((lb, SSD_DINNER), lambda b, n: (bi(b, n), 0))],
        out_shape=[jax.ShapeDtypeStruct((t, SSD_DINNER), BF16)] * 2,
        scratch_shapes=[pltpu.VMEM((SSD_GROUPS, SSD_STATE, SSD_GW), F32)] * 2,
        compiler_params=pltpu.CompilerParams(
            dimension_semantics=("arbitrary", "arbitrary"), vmem_limit_bytes=VMEM_LIMIT),
        name="ssd_scan",
    )(*args, *args, dtb_f[None, :], dtb_b[None, :], alog_f[None, :], alog_b[None, :])


def _merge_kernel(x_ref, of_ref, ob_ref, r_ref, gg_ref, yf_ref, yb_ref, xs_ref, z_ref, gs_ref,
                  gng_ref, dsk_ref, sng_ref, wug_ref, wus_ref, wo_ref, nfg_ref, wr_ref, br_ref,
                  h_ref, xp_ref, route_ref, gate_ref, cnt_out_ref, cnt_ref):
    @pl.when(pl.program_id(0) == 0)
    def _():
        cnt_ref[...] = jnp.zeros_like(cnt_ref)

    tm = x_ref.shape[0]
    o = of_ref[...].astype(F32) + ob_ref[...].astype(F32)
    gng = gng_ref[...]
    o_parts = []
    for h in range(GLA_HEADS):
        oh = o[:, h * GLA_DV:(h + 1) * GLA_DV]
        oh = oh * lax.rsqrt(jnp.mean(oh * oh, axis=-1, keepdims=True) + EPS)
        o_parts.append(oh * gng)
    o = jnp.concatenate(o_parts, axis=1) * _silu(r_ref[...].astype(F32))
    up_g = _dot(o.astype(BF16), wug_ref[...])
    y = yf_ref[...].astype(F32) + yb_ref[...].astype(F32) + dsk_ref[...] * xs_ref[...].astype(F32)
    y = y * _silu(z_ref[...].astype(F32))
    sng = sng_ref[...]
    y_parts = []
    for g in range(SSD_GROUPS):
        yg = y[:, g * SSD_GW:(g + 1) * SSD_GW]
        yg = yg * lax.rsqrt(jnp.mean(yg * yg, axis=-1, keepdims=True) + EPS)
        y_parts.append(yg * sng[:, g * SSD_GW:(g + 1) * SSD_GW])
    y = jnp.concatenate(y_parts, axis=1)
    up_s = _dot(y.astype(BF16), wus_ref[...])
    mix = _sigmoid(gg_ref[...].astype(F32)) * up_g + _sigmoid(gs_ref[...].astype(F32)) * up_s
    h = x_ref[...] + _dot(mix.astype(BF16), wo_ref[...])
    h_ref[...] = h
    hn = (h * lax.rsqrt(jnp.mean(h * h, axis=-1, keepdims=True) + EPS)) * nfg_ref[...]
    words = _pack_bf16_pair(hn[:, :D_MODEL // 2], hn[:, D_MODEL // 2:])
    xp_ref[0] = words[:, :SC_W]
    xp_ref[1] = words[:, SC_W:]
    hn_hi, hn_lo = _split_bf16(hn, 2)
    logits = _dot(hn_hi, wr_ref[0]) + (_dot(hn_hi, wr_ref[1]) + _dot(hn_lo, wr_ref[0])) + br_ref[...]
    lane = lax.broadcasted_iota(I32, (tm, N_EXPERTS), 1).astype(F32)
    work = logits
    idxs, vals = [], []
    for _ in range(TOP_K):
        m = jnp.max(work, axis=-1, keepdims=True)
        idx = jnp.min(jnp.where(work == m, lane, float(N_EXPERTS)), axis=-1, keepdims=True)
        idxs.append(idx)
        vals.append(m)
        work = jnp.where(lane == idx, -jnp.inf, work)
    exps = [jnp.exp(v - vals[0]) for v in vals]
    denom = exps[0] + exps[1] + exps[2] + exps[3]
    gates = [e / denom for e in exps]
    sel = jnp.zeros((tm, N_EXPERTS), F32)
    for idx in idxs:
        sel = sel + jnp.where(lane == idx, 1.0, 0.0)
    rr = lax.broadcasted_iota(I32, (tm, tm), 0)
    cc = lax.broadcasted_iota(I32, (tm, tm), 1)
    strict = jnp.where(rr > cc, 1.0, 0.0).astype(BF16)
    pos = _dot(strict, sel.astype(BF16)) + cnt_ref[0:1, :]
    ranks = [jnp.sum(jnp.where(lane == idx, pos, 0.0), axis=-1, keepdims=True).astype(I32) for idx in idxs]
    cnt_new = cnt_ref[0:1, :] + jnp.sum(sel, axis=0, keepdims=True)
    cnt_ref[...] = jnp.broadcast_to(cnt_new, cnt_ref.shape)
    cnt_out_ref[...] = jnp.broadcast_to(cnt_new, cnt_ref.shape).astype(I32)
    lane128 = lax.broadcasted_iota(I32, (tm, 128), 1)
    route = jnp.zeros((tm, 128), I32)
    gate_o = jnp.zeros((tm, 128), F32)
    for k in range(TOP_K):
        route = jnp.where(lane128 == k, idxs[k].astype(I32), route)
        route = jnp.where(lane128 == TOP_K + k, ranks[k], route)
        gate_o = jnp.where(lane128 == k, gates[k], gate_o)
    route_ref[...] = route
    gate_ref[...] = gate_o


def _merge(x2, o_f, o_b, proj, y_f, y_b, xs_c, gla_norm_g, ssd_d, ssd_norm_g, w_up_gla, w_up_ssd, w_out,
           norm_ffn_g, w_router, b_router, tm):
    t = x2.shape[0]
    rowblk = lambda w, col=0: pl.BlockSpec((tm, w), lambda i: (i, col))
    const = lambda a: pl.BlockSpec(a.shape, lambda i: (0,) * a.ndim)
    d_skip = jnp.repeat(ssd_d, SSD_HEADDIM)[None, :]
    params = [gla_norm_g[None, :], d_skip, ssd_norm_g[None, :], w_up_gla.astype(BF16), w_up_ssd.astype(BF16),
              w_out.astype(BF16), norm_ffn_g[None, :], jnp.stack(_split_bf16(w_router, 2)), b_router[None, :]]
    return pl.pallas_call(
        _merge_kernel,
        grid=(t // tm,),
        in_specs=[rowblk(D_MODEL), rowblk(GLA_V), rowblk(GLA_V), rowblk(GLA_V, C_R // GLA_V),
                  rowblk(D_MODEL, C_GG // D_MODEL), rowblk(SSD_DINNER), rowblk(SSD_DINNER), rowblk(SSD_DINNER),
                  rowblk(SSD_DINNER, C_Z // SSD_DINNER), rowblk(D_MODEL, C_GS // D_MODEL)]
        + [const(p) for p in params],
        out_specs=[rowblk(D_MODEL), pl.BlockSpec((2, tm, SC_W), lambda i: (0, i, 0)), rowblk(128), rowblk(128),
                   pl.BlockSpec((8, N_EXPERTS), lambda i: (0, 0))],
        out_shape=[jax.ShapeDtypeStruct((t, D_MODEL), F32), jax.ShapeDtypeStruct((2, t, SC_W), U32),
                   jax.ShapeDtypeStruct((t, 128), I32), jax.ShapeDtypeStruct((t, 128), F32),
                   jax.ShapeDtypeStruct((8, N_EXPERTS), I32)],
        scratch_shapes=[pltpu.VMEM((8, N_EXPERTS), F32)],
        compiler_params=pltpu.CompilerParams(dimension_semantics=("arbitrary",), vmem_limit_bytes=VMEM_LIMIT),
        name="merge_router",
    )(x2, o_f, o_b, proj, proj, y_f, y_b, xs_c, proj, proj, *params)


def _sc_mesh():
    return plsc.VectorSubcoreMesh(core_axis_name="c", subcore_axis_name="s")


def _sc_scatter_rows(x, idx, n_out):
    n, m = x.shape[0], idx.shape[1]
    n_win = n // SC_WIN

    @functools.partial(pl.kernel, out_type=jax.ShapeDtypeStruct((n_out, SC_W), x.dtype), mesh=_sc_mesh())
    def scatter(x_hbm, i_hbm, o_hbm):
        def body(x_vmem, i_vmem):
            pltpu.sync_copy(x_vmem, o_hbm.at[i_vmem.at[0]])

        pltpu.emit_pipeline(
            body, grid=(m // SC_WIN,),
            in_specs=[pl.BlockSpec((SC_WIN, SC_W), lambda i: (i % n_win, 0)),
                      pl.BlockSpec((1, SC_WIN), lambda i: (0, i))],
            out_specs=[], core_axis_name=("c", "s"), dimension_semantics=(pltpu.PARALLEL,),
        )(x_hbm, i_hbm)

    return scatter(x, idx)


def _sc_gather_rows(table, idx):
    m = idx.shape[1]

    @functools.partial(pl.kernel, out_type=jax.ShapeDtypeStruct((m, SC_W), table.dtype), mesh=_sc_mesh())
    def gather(t_hbm, i_hbm, o_hbm):
        def body(i_vmem, o_vmem):
            pltpu.sync_copy(t_hbm.at[i_vmem.at[0]], o_vmem)

        pltpu.emit_pipeline(
            body, grid=(m // SC_WIN,),
            in_specs=[pl.BlockSpec((1, SC_WIN), lambda i: (0, i))],
            out_specs=[pl.BlockSpec((SC_WIN, SC_W), lambda i: (i, 0))],
            core_axis_name=("c", "s"), dimension_semantics=(pltpu.PARALLEL,),
        )(i_hbm, o_hbm)

    return gather(table, idx)


def _expert_kernel(blk_e_ref, nvalid_ref, nused_ref, x_ref, w1_ref, b1_ref, w2_ref, b2_ref, o_ref, w1_s, w2_s):
    del nused_ref
    i = pl.program_id(0)
    nvalid = nvalid_ref[i]

    @pl.when((nvalid > 0) & ((i == 0) | (blk_e_ref[i] != blk_e_ref[jnp.maximum(i - 1, 0)])))
    def _():
        w1_s[...] = w1_ref[0].astype(BF16)
        w2_s[...] = w2_ref[0].astype(BF16)

    @pl.when(nvalid > 0)
    def _():
        live = lax.broadcasted_iota(I32, (MOE_BM, SC_W), 0) < nvalid
        lo0, hi0 = _unpack_bf16_pair(jnp.where(live, x_ref[0], jnp.uint32(0)))
        lo1, hi1 = _unpack_bf16_pair(jnp.where(live, x_ref[1], jnp.uint32(0)))
        x = jnp.concatenate([lo0, lo1, hi0, hi1], axis=1).astype(BF16)
        hdn = _dot(x, w1_s[...]) + b1_ref[0]
        gate = jnp.minimum(hdn[:, :D_FF], SWIGLU_LIMIT)
        lin = jnp.clip(hdn[:, D_FF:], -SWIGLU_LIMIT, SWIGLU_LIMIT)
        act = gate * _sigmoid(SWIGLU_ALPHA * gate) * (lin + 1.0)
        y = _dot(act.astype(BF16), w2_s[...]) + b2_ref[0]
        words = _pack_bf16_pair(y[:, :D_MODEL // 2], y[:, D_MODEL // 2:])
        o_ref[0] = words[:, :SC_W]
        o_ref[1] = words[:, SC_W:]

    @pl.when(nvalid == 0)
    def _():
        o_ref[...] = jnp.zeros_like(o_ref)


def _experts(xb, blk_e, nvalid, n_used, w1, b1, w2, b2):
    n_rows = xb.shape[1]
    n_blocks = n_rows // MOE_BM
    xidx = lambda i, be, nv, nu: (0, jnp.minimum(i, nu[0] - 1), 0)
    eidx = lambda i, be, nv, nu: (be[i], 0, 0)
    return pl.pallas_call(
        _expert_kernel,
        grid_spec=pltpu.PrefetchScalarGridSpec(
            num_scalar_prefetch=3,
            grid=(n_blocks,),
            in_specs=[pl.BlockSpec((2, MOE_BM, SC_W), xidx),
                      pl.BlockSpec((1, D_MODEL, 2 * D_FF), eidx),
                      pl.BlockSpec((1, 1, 2 * D_FF), eidx),
                      pl.BlockSpec((1, D_FF, D_MODEL), eidx),
                      pl.BlockSpec((1, 1, D_MODEL), eidx)],
            out_specs=pl.BlockSpec((2, MOE_BM, SC_W), lambda i, be, nv, nu: (0, i, 0)),
            scratch_shapes=[pltpu.VMEM((D_MODEL, 2 * D_FF), BF16), pltpu.VMEM((D_FF, D_MODEL), BF16)],
        ),
        out_shape=jax.ShapeDtypeStruct((2, n_rows, SC_W), U32),
        compiler_params=pltpu.CompilerParams(dimension_semantics=("arbitrary",), vmem_limit_bytes=VMEM_LIMIT),
        name="moe_experts",
    )(blk_e, nvalid, n_used, xb, w1, b1, w2, b2)


def _combine_kernel(h_ref, gate_ref, y_ref, g_ref, o_ref):
    gates = gate_ref[...]
    acc = None
    for k in range(TOP_K):
        lo0, hi0 = _unpack_bf16_pair(y_ref[k, 0])
        lo1, hi1 = _unpack_bf16_pair(y_ref[k, 1])
        term = gates[:, k:k + 1] * jnp.concatenate([lo0, lo1, hi0, hi1], axis=1)
        acc = term if acc is None else acc + term
    h = h_ref[...] + acc
    o_ref[...] = (h * lax.rsqrt(jnp.mean(h * h, axis=-1, keepdims=True) + EPS)) * g_ref[...]


def _combine(h, gates, y_rows, norm_final_g, tm):
    t = h.shape[0]
    return pl.pallas_call(
        _combine_kernel,
        grid=(t // tm,),
        in_specs=[pl.BlockSpec((tm, D_MODEL), lambda i: (i, 0)),
                  pl.BlockSpec((tm, 128), lambda i: (i, 0)),
                  pl.BlockSpec((TOP_K, 2, tm, SC_W), lambda i: (0, 0, i, 0)),
                  pl.BlockSpec((1, D_MODEL), lambda i: (0, 0))],
        out_specs=pl.BlockSpec((tm, D_MODEL), lambda i: (i, 0)),
        out_shape=jax.ShapeDtypeStruct((t, D_MODEL), F32),
        compiler_params=pltpu.CompilerParams(dimension_semantics=("arbitrary",), vmem_limit_bytes=VMEM_LIMIT),
        name="moe_combine",
    )(h, gates, y_rows, norm_final_g[None, :])


def _pick(n, pref):
    b = min(pref, n)
    while n % b:
        b -= CHUNK
    return b


def _prep_w_in(w_in):
    widths = (GLA_QK, GLA_QK, GLA_V, GLA_V, GLA_RANK, GLA_RANK, SSD_DINNER, SSD_DINNER, SSD_BC, SSD_BC,
              2 * SSD_HEADS, D_MODEL, D_MODEL)
    pts, acc = [], 0
    for w in widths[:-1]:
        acc += w
        pts.append(acc)
    q, k, v, r, lrf, lrb, z, xs, bm, cm, dtr, gg, gs = jnp.split(w_in, pts, axis=1)
    main = jnp.concatenate([z, xs, q, k, v, r, gg, gs, bm, cm], axis=1).astype(BF16)
    pad = jnp.zeros((D_MODEL, N_SMALL - 2 * GLA_RANK - 2 * SSD_HEADS), w_in.dtype)
    small = jnp.concatenate([lrf, lrb, dtr, pad], axis=1).astype(BF16)
    return main, small


def _layer(h, norm_mix_g, w_in, gla_fw2_f, gla_fb_f, gla_fw2_b, gla_fb_b, gla_norm_g, conv_w, conv_b,
           dt_bias_f, dt_bias_b, a_log_f, a_log_b, ssd_d, ssd_norm_g, w_up_gla, w_up_ssd, w_out,
           norm_ffn_g, w_router, b_router, w1, b1, w2, b2, out_norm_g):
    bsz, seq, _ = h.shape
    t = bsz * seq
    x2 = h.reshape(t, D_MODEL)
    w_main, w_small = _prep_w_in(w_in)
    proj, small = _inproj(x2, norm_mix_g[None, :], w_main, w_small, tm=_pick(t, 2048), tn=1024)
    lb = _pick(seq, 512)
    o_f, o_b = _gla(proj, small, gla_fw2_f.astype(BF16), gla_fb_f[None, :], gla_fw2_b.astype(BF16),
                    gla_fb_b[None, :], bsz, seq, lb)
    xs_c, bm_c, cm_c = _conv(proj, conv_w, conv_b, seq, _pick(seq, 512))
    y_f, y_b = _ssd(xs_c, bm_c, cm_c, small, dt_bias_f, dt_bias_b, a_log_f, a_log_b, bsz, seq, lb)
    hres, xp, route, gates, counts8 = _merge(x2, o_f, o_b, proj, y_f, y_b, xs_c, gla_norm_g, ssd_d, ssd_norm_g,
                                             w_up_gla, w_up_ssd, w_out, norm_ffn_g, w_router, b_router,
                                             tm=_pick(t, 256))
    counts = counts8[0]
    padded = (counts + MOE_BM - 1) // MOE_BM * MOE_BM
    pend = jnp.cumsum(padded)
    pstart = (pend - padded).astype(I32)
    n_rows = t * TOP_K + N_EXPERTS * MOE_BM
    blk_row = jnp.arange(n_rows // MOE_BM, dtype=I32) * MOE_BM
    blk_e = jnp.minimum(jnp.sum(pend[None, :] <= blk_row[:, None], axis=1), N_EXPERTS - 1).astype(I32)
    nvalid = jnp.clip(pstart[blk_e] + counts[blk_e] - blk_row, 0, MOE_BM).astype(I32)
    n_used = (pend[-1:] // MOE_BM).astype(I32)
    top_e, rank = route[:, :TOP_K], route[:, TOP_K:2 * TOP_K]
    dest = jnp.sum(jnp.where(top_e[:, :, None] == jnp.arange(N_EXPERTS, dtype=I32), pstart, 0), axis=-1) + rank
    idx = (dest.T[:, None, :] + (jnp.arange(2, dtype=I32) * n_rows)[None, :, None]).reshape(1, 2 * TOP_K * t)
    xb = _sc_scatter_rows(xp.reshape(2 * t, SC_W), idx, 2 * n_rows).reshape(2, n_rows, SC_W)
    yb = _experts(xb, blk_e, nvalid, n_used, w1, b1[:, None, :], w2, b2[:, None, :])
    y_rows = _sc_gather_rows(yb.reshape(2 * n_rows, SC_W), idx).reshape(TOP_K, 2, t, SC_W)
    out = _combine(hres, gates, y_rows, out_norm_g, tm=_pick(t, 256))
    return out.reshape(bsz, seq, D_MODEL)


def kernel(x, norm_mix_g, w_in, gla_fw2_f, gla_fb_f, gla_fw2_b, gla_fb_b, gla_norm_g, conv_w, conv_b, dt_bias_f,
           dt_bias_b, a_log_f, a_log_b, ssd_d, ssd_norm_g, w_up_gla, w_up_ssd, w_out, norm_ffn_g, w_router,
           b_router, w1, b1, w2, b2, norm_final_g):
    assert x.shape[-1] == D_MODEL and norm_mix_g.shape[0] == 1
    return _layer(x, norm_mix_g[0], w_in[0], gla_fw2_f[0], gla_fb_f[0], gla_fw2_b[0], gla_fb_b[0], gla_norm_g[0],
                  conv_w[0], conv_b[0], dt_bias_f[0], dt_bias_b[0], a_log_f[0], a_log_b[0], ssd_d[0],
                  ssd_norm_g[0], w_up_gla[0], w_up_ssd[0], w_out[0], norm_ffn_g[0], w_router[0], b_router[0],
                  w1[0], b1[0], w2[0], b2[0], norm_final_g)
```

```python
import functools

import jax
import jax.numpy as jnp
from jax import lax
from jax.experimental import pallas as pl
from jax.experimental.pallas import tpu as pltpu
from jax.experimental.pallas import tpu_sc as plsc

F32 = jnp.float32
BF16 = jnp.bfloat16
I32 = jnp.int32
U32 = jnp.uint32

D_MODEL = 1024
EPS = 1e-5
GLA_HEADS = 4
GLA_DK = 128
GLA_DV = 256
GLA_RANK = 16
GLA_TAU = 16.0
GLA_QK = GLA_HEADS * GLA_DK
GLA_V = GLA_HEADS * GLA_DV
SSD_DINNER = 2048
SSD_HEADDIM = 64
SSD_HEADS = 32
SSD_GROUPS = 4
SSD_HPG = 8
SSD_STATE = 128
SSD_CONV = 4
SSD_BC = SSD_GROUPS * SSD_STATE
SSD_GW = SSD_HPG * SSD_HEADDIM
N_EXPERTS = 32
TOP_K = 4
D_FF = 1024
SWIGLU_LIMIT = 7.0
SWIGLU_ALPHA = 1.702
CHUNK = 64
TOKEN_GROUPS = 2
SSD_INNER = 2
MXU_LAG = 6

C_Z, C_XS, C_Q, C_K, C_V, C_R, C_GG, C_GS, C_B, C_C = 0, 2048, 4096, 4608, 5120, 6144, 7168, 8192, 9216, 9728
N_MAIN = 10240
N_SMALL = 128
S_LRF, S_LRB, S_DTF, S_DTB = 0, 16, 32, 64

VMEM_LIMIT = 56 * 1024 * 1024
MOE_BM = 512
SC_WIN = 128
SC_W = D_MODEL // 4


def _dot(a, b):
    return jnp.dot(a, b, preferred_element_type=F32)


def _dot_nt(a, b):
    return lax.dot_general(a, b, (((1,), (1,)), ((), ())), preferred_element_type=F32)


def _dot_tn(a, b):
    return lax.dot_general(a, b, (((0,), (0,)), ((), ())), preferred_element_type=F32)


def _split_bf16(x, n):
    parts = []
    r = x
    for _ in range(n):
        p = r.astype(BF16)
        parts.append(p)
        r = r - p.astype(F32)
    return parts


def _dot_exact_lhs(m_bf16, x, n):
    acc = None
    for p in _split_bf16(x, n):
        t = _dot(m_bf16, p)
        acc = t if acc is None else acc + t
    return acc


def _dot_exact_rhs(x, m_bf16, n):
    acc = None
    for p in _split_bf16(x, n):
        t = _dot(p, m_bf16)
        acc = t if acc is None else acc + t
    return acc


def _sigmoid(x):
    return 1.0 / (1.0 + jnp.exp(-x))


def _silu(x):
    return x * _sigmoid(x)


def _pack_bf16_pair(lo, hi):
    lo_b = lax.bitcast_convert_type(lo.astype(BF16).astype(F32), U32)
    hi_b = lax.bitcast_convert_type(hi.astype(BF16).astype(F32), U32)
    return (lo_b >> 16) | (hi_b & jnp.uint32(0xFFFF0000))


def _unpack_bf16_pair(u):
    lo = lax.bitcast_convert_type(u << 16, F32)
    hi = lax.bitcast_convert_type(u & jnp.uint32(0xFFFF0000), F32)
    return lo, hi


def _inproj_kernel(x_ref, g_ref, w_ref, ws_ref, o_ref, os_ref, xn_ref):
    @pl.when(pl.program_id(1) == 0)
    def _():
        x = x_ref[...]
        ms = jnp.mean(x * x, axis=-1, keepdims=True)
        xb = ((x * lax.rsqrt(ms + EPS)) * g_ref[...]).astype(BF16)
        xn_ref[...] = xb
        os_ref[...] = _dot(xb, ws_ref[...])

    o_ref[...] = _dot(xn_ref[...], w_ref[...]).astype(BF16)


def _inproj(x2, g, w_main, w_small, row0, t, tm, tn):
    blk0 = row0 // tm
    return pl.pallas_call(
        _inproj_kernel,
        grid=(t // tm, N_MAIN // tn),
        in_specs=[
            pl.BlockSpec((tm, D_MODEL), lambda i, j: (blk0 + i, 0)),
            pl.BlockSpec((1, D_MODEL), lambda i, j: (0, 0)),
            pl.BlockSpec((D_MODEL, tn), lambda i, j: (0, j)),
            pl.BlockSpec((D_MODEL, N_SMALL), lambda i, j: (0, 0)),
        ],
        out_specs=[
            pl.BlockSpec((tm, tn), lambda i, j: (i, j)),
            pl.BlockSpec((tm, N_SMALL), lambda i, j: (i, 0)),
        ],
        out_shape=[
            jax.ShapeDtypeStruct((t, N_MAIN), BF16),
            jax.ShapeDtypeStruct((t, N_SMALL), F32),
        ],
        scratch_shapes=[pltpu.VMEM((tm, D_MODEL), BF16)],
        compiler_params=pltpu.CompilerParams(
            dimension_semantics=("arbitrary", "arbitrary"), vmem_limit_bytes=VMEM_LIMIT),
        name="inproj",
    )(x2, g, w_main, w_small)


def _tri_masks(n):
    r = lax.broadcasted_iota(I32, (n, n), 0)
    c = lax.broadcasted_iota(I32, (n, n), 1)
    return r >= c, c >= r


class _GlaDir:
    def __init__(self, ins, outs, scratch, mask, lr_off, mid_row, last_row):
        self.q, self.k, self.v, self.sm, self.fw2, self.fb = ins
        self.o, self.st = outs
        self.b_s, self.qs_s, self.ks_s, self.kd_s, self.qe_s, self.oi_s, self.u_s, self.el_s = scratch
        self.mask, self.lr_off, self.mid_row, self.last_row = mask, lr_off, mid_row, last_row


def _gla_decay(d, n_chunks):
    tri = jnp.where(d.mask, 1.0, 0.0).astype(BF16)
    lr = d.sm[:, d.lr_off:d.lr_off + GLA_RANK].astype(BF16)
    xg = _dot(lr, d.fw2[...]) + d.fb[...]
    d.b_s[...] = (jnp.minimum(xg, 0.0) - jnp.log(1.0 + jnp.exp(-jnp.abs(xg)))) * (1.0 / GLA_TAU)
    for c in range(n_chunks):
        rows = slice(c * CHUNK, (c + 1) * CHUNK)
        d.b_s[rows, :] = _dot_exact_lhs(tri, d.b_s[rows, :], 2)


def _gla_scale(d, c):
    rows = slice(c * CHUNK, (c + 1) * CHUNK)
    b = d.b_s[rows, :]
    b_mid = b[d.mid_row:d.mid_row + 1, :]
    b_last = b[d.last_row:d.last_row + 1, :]
    q = d.q[rows, :].astype(F32) * (GLA_DK ** -0.5)
    k = d.k[rows, :].astype(F32)
    d.qs_s[rows, :] = (q * jnp.exp(b - b_mid)).astype(BF16)
    d.ks_s[rows, :] = (k * jnp.exp(b_mid - b)).astype(BF16)
    d.kd_s[rows, :] = (k * jnp.exp(b_last - b)).astype(BF16)
    d.qe_s[rows, :] = (q * jnp.exp(b)).astype(BF16)
    d.el_s[c:c + 1, :] = jnp.exp(b_last)


def _gla_local(units):
    def score(u):
        d, c, h = u
        rows, ks_ = slice(c * CHUNK, (c + 1) * CHUNK), slice(h * GLA_DK, (h + 1) * GLA_DK)
        return _dot_nt(d.qs_s[rows, ks_], d.ks_s[rows, ks_])

    def finish(u, s):
        d, c, h = u
        rows, ks_ = slice(c * CHUNK, (c + 1) * CHUNK), slice(h * GLA_DK, (h + 1) * GLA_DK)
        vs_ = slice(h * GLA_DV, (h + 1) * GLA_DV)
        p = jnp.where(d.mask, s, 0.0).astype(BF16)
        d.oi_s[rows, vs_] = _dot(p, d.v[rows, vs_])
        d.u_s[c, h] = _dot_tn(d.kd_s[rows, ks_], d.v[rows, vs_])

    pending = []
    for u in units:
        pending.append((u, score(u)))
        if len(pending) > MXU_LAG:
            finish(*pending.pop(0))
    for item in pending:
        finish(*item)


def _gla_carry(d, c):
    rows = slice(c * CHUNK, (c + 1) * CHUNK)
    for h in range(GLA_HEADS):
        ks_ = slice(h * GLA_DK, (h + 1) * GLA_DK)
        vs_ = slice(h * GLA_DV, (h + 1) * GLA_DV)
        st = d.st[h]
        d.o[rows, vs_] = (d.oi_s[rows, vs_] + _dot(d.qe_s[rows, ks_], st.astype(BF16))).astype(d.o.dtype)
        e_col = jnp.transpose(jnp.broadcast_to(d.el_s[c:c + 1, ks_], (8, GLA_DK)))[:, 0:1]
        d.st[h] = st * e_col + d.u_s[c, h]


def _gla_kernel(*refs, n_chunks):
    ins_f, ins_b, (fw2f, fbf, fw2b, fbb), (of_ref, ob_ref, stf, stb) = refs[0:4], refs[4:8], refs[8:12], refs[12:16]
    scr_f, scr_b = refs[16:24], refs[24:32]

    @pl.when(pl.program_id(1) == 0)
    def _():
        stf[...] = jnp.zeros_like(stf)
        stb[...] = jnp.zeros_like(stb)

    lower, upper = _tri_masks(CHUNK)
    fwd = _GlaDir((*ins_f, fw2f, fbf), (of_ref, stf), scr_f, lower, S_LRF, CHUNK // 2, CHUNK - 1)
    bwd = _GlaDir((*ins_b, fw2b, fbb), (ob_ref, stb), scr_b, upper, S_LRB, CHUNK // 2 - 1, 0)
    _gla_decay(fwd, n_chunks)
    _gla_decay(bwd, n_chunks)
    for c in range(n_chunks):
        _gla_scale(fwd, c)
        _gla_scale(bwd, c)
    _gla_local([(d, c, h) for c in range(n_chunks) for d in (fwd, bwd) for h in range(GLA_HEADS)])
    for i in range(n_chunks):
        _gla_carry(fwd, i)
        _gla_carry(bwd, n_chunks - 1 - i)


def _gla(proj, small, fw2f, fbf, fw2b, fbb, bsz, seq, lb):
    t = bsz * seq
    nb = seq // lb

    def fwd(w, col):
        return pl.BlockSpec((lb, w), lambda b, n: (b * nb + n, col))

    def bwd(w, col):
        return pl.BlockSpec((lb, w), lambda b, n: (b * nb + nb - 1 - n, col))

    const = lambda shape: pl.BlockSpec(shape, lambda b, n: (0, 0))
    per_dir_scratch = ([pltpu.VMEM((lb, GLA_QK), F32)] + [pltpu.VMEM((lb, GLA_QK), BF16)] * 4
                       + [pltpu.VMEM((lb, GLA_V), F32),
                          pltpu.VMEM((lb // CHUNK, GLA_HEADS, GLA_DK, GLA_DV), F32),
                          pltpu.VMEM((lb // CHUNK, GLA_QK), F32)])
    return pl.pallas_call(
        functools.partial(_gla_kernel, n_chunks=lb // CHUNK),
        grid=(bsz, nb),
        in_specs=[
            fwd(GLA_QK, C_Q // GLA_QK), fwd(GLA_QK, C_K // GLA_QK), fwd(GLA_V, C_V // GLA_V), fwd(N_SMALL, 0),
            bwd(GLA_QK, C_Q // GLA_QK), bwd(GLA_QK, C_K // GLA_QK), bwd(GLA_V, C_V // GLA_V), bwd(N_SMALL, 0),
            const((GLA_RANK, GLA_QK)), const((1, GLA_QK)), const((GLA_RANK, GLA_QK)), const((1, GLA_QK)),
        ],
        out_specs=[fwd(GLA_V, 0), bwd(GLA_V, 0)],
        out_shape=[jax.ShapeDtypeStruct((t, GLA_V), BF16)] * 2,
        scratch_shapes=[pltpu.VMEM((GLA_HEADS, GLA_DK, GLA_DV), F32)] * 2 + per_dir_scratch * 2,
        compiler_params=pltpu.CompilerParams(
            dimension_semantics=("arbitrary", "arbitrary"), vmem_limit_bytes=VMEM_LIMIT),
        name="gla_scan",
    )(proj, proj, proj, small, proj, proj, proj, small, fw2f, fbf, fw2b, fbb)


HALO = 16


def _conv_one(x_ref, p_ref, n_ref, w_ref, b_ref, o_ref, has_prev, has_next):
    x = x_ref[...].astype(F32)
    rb = x.shape[0]
    row = lax.broadcasted_iota(I32, x.shape, 0)
    prev = jnp.where(has_prev, p_ref[HALO - 1:HALO, :].astype(F32), 0.0)
    nxt = jnp.where(has_next, n_ref[0:2, :].astype(F32), 0.0)
    xm1 = jnp.where(row == 0, prev, pltpu.roll(x, 1, 0))
    xp1 = jnp.where(row == rb - 1, nxt[0:1, :], pltpu.roll(x, rb - 1, 0))
    xp2 = jnp.where(row == rb - 2, nxt[0:1, :], jnp.where(row == rb - 1, nxt[1:2, :], pltpu.roll(x, rb - 2, 0)))
    w = w_ref[...]
    y = xm1 * w[0:1, :] + x * w[1:2, :] + xp1 * w[2:3, :] + xp2 * w[3:4, :] + b_ref[...]
    o_ref[...] = _silu(y).astype(o_ref.dtype)


def _conv_kernel(xs, xsp, xsn, bm, bmp, bmn, cm, cmp_, cmn, wx, bx, wb, bb, wc, bc, oxs, obm, ocm, *, rb, seq):
    t0 = pl.program_id(0) * rb
    has_prev = (t0 % seq) != 0
    has_next = ((t0 + rb) % seq) != 0
    _conv_one(xs, xsp, xsn, wx, bx, oxs, has_prev, has_next)
    _conv_one(bm, bmp, bmn, wb, bb, obm, has_prev, has_next)
    _conv_one(cm, cmp_, cmn, wc, bc, ocm, has_prev, has_next)


def _conv(proj, conv_w, conv_b, seq, rb):
    t = proj.shape[0]
    nh = t // HALO
    per = rb // HALO

    def trio(w, col):
        cb = col // w
        return [
            pl.BlockSpec((rb, w), lambda i: (i, cb)),
            pl.BlockSpec((HALO, w), lambda i: (jnp.maximum(i * per - 1, 0), cb)),
            pl.BlockSpec((HALO, w), lambda i: (jnp.minimum((i + 1) * per, nh - 1), cb)),
        ]

    def wspecs(w):
        return [pl.BlockSpec((SSD_CONV, w), lambda i: (0, 0)), pl.BlockSpec((1, w), lambda i: (0, 0))]

    wx, wb, wc = conv_w[:, :SSD_DINNER], conv_w[:, SSD_DINNER:SSD_DINNER + SSD_BC], conv_w[:, SSD_DINNER + SSD_BC:]
    bx, bb, bc = (conv_b[None, :SSD_DINNER], conv_b[None, SSD_DINNER:SSD_DINNER + SSD_BC],
                  conv_b[None, SSD_DINNER + SSD_BC:])
    return pl.pallas_call(
        functools.partial(_conv_kernel, rb=rb, seq=seq),
        grid=(t // rb,),
        in_specs=trio(SSD_DINNER, C_XS) + trio(SSD_BC, C_B) + trio(SSD_BC, C_C)
        + wspecs(SSD_DINNER) + wspecs(SSD_BC) + wspecs(SSD_BC),
        out_specs=[pl.BlockSpec((rb, SSD_DINNER), lambda i: (i, 0)),
                   pl.BlockSpec((rb, SSD_BC), lambda i: (i, 0)),
                   pl.BlockSpec((rb, SSD_BC), lambda i: (i, 0))],
        out_shape=[jax.ShapeDtypeStruct((t, SSD_DINNER), BF16),
                   jax.ShapeDtypeStruct((t, SSD_BC), BF16),
                   jax.ShapeDtypeStruct((t, SSD_BC), BF16)],
        compiler_params=pltpu.CompilerParams(dimension_semantics=("arbitrary",), vmem_limit_bytes=VMEM_LIMIT),
        name="ssd_conv",
    )(proj, proj, proj, proj, proj, proj, proj, proj, proj, wx, bx, wb, bb, wc, bc)


def _softplus(x):
    return jnp.maximum(x, 0.0) + jnp.log(1.0 + jnp.exp(-jnp.abs(x)))


class _SsdDir:
    def __init__(self, xs, bm, cm, sm, dtb_row, alog_row, y, st, reverse, dt_off, last_row):
        self.xs, self.bm, self.cm, self.sm, self.dtb_row, self.alog_row = xs, bm, cm, sm, dtb_row, alog_row
        self.y, self.st, self.reverse, self.dt_off, self.last_row = y, st, reverse, dt_off, last_row


def _ssd_chunks(work):
    hh = lax.broadcasted_iota(I32, (SSD_HEADS, SSD_GW), 0)
    cc = lax.broadcasted_iota(I32, (SSD_HEADS, SSD_GW), 1)
    lane = lax.broadcasted_iota(I32, (CHUNK, 2 * SSD_HEADDIM), 1)
    row2 = lax.broadcasted_iota(I32, (CHUNK, 2 * SSD_HEADDIM), 0)
    left = lane < SSD_HEADDIM
    col2 = jnp.where(left, lane, lane - SSD_HEADDIM)
    lower, upper = _tri_masks(CHUNK)

    pre = []
    for d, c0 in work:
        rows = pl.ds(c0, CHUNK)
        tri = jnp.where(upper if d.reverse else lower, 1.0, 0.0).astype(BF16)
        a_row = -jnp.exp(d.alog_row[...])
        dt = _softplus(d.sm[rows, d.dt_off:d.dt_off + SSD_HEADS] + d.dtb_row[...])
        pre.append((dt, _dot_exact_lhs(tri, dt * a_row, 3)))
    heads = []
    for (d, c0), (dt, cum) in zip(work, pre):
        total = cum[d.last_row:d.last_row + 1, :]
        to_end = jnp.exp(total - cum) * dt
        fac = jnp.concatenate([to_end, jnp.exp(cum)], axis=0).astype(BF16)
        e_tot = jnp.broadcast_to(jnp.exp(total), (8, SSD_HEADS))
        heads.append((cum, jnp.transpose(cum), jnp.transpose(dt), fac, e_tot))

    units = [(w, g) for w in range(len(work)) for g in range(SSD_GROUPS)]
    groups = {}
    for w, g in units:
        d, c0 = work[w]
        rows, ns = pl.ds(c0, CHUNK), slice(g * SSD_STATE, (g + 1) * SSD_STATE)
        fac, e_tot = heads[w][3], heads[w][4]
        expand = jnp.where(cc // SSD_HEADDIM + g * SSD_HPG == hh, 1.0, 0.0).astype(BF16)
        fac_x = _dot(fac, expand)
        et_x = _dot_exact_rhs(e_tot, expand, 2)[0:1]
        groups[w, g] = (fac_x, et_x, _dot_nt(d.cm[rows, ns], d.bm[rows, ns]))

    for w, g in units:
        d, c0 = work[w]
        rows, ns, cs = pl.ds(c0, CHUNK), slice(g * SSD_STATE, (g + 1) * SSD_STATE), slice(g * SSD_GW, (g + 1) * SSD_GW)
        cum, cum_t, dt_t = heads[w][0:3]
        fac_x, _, cb = groups[w, g]
        mask2 = (col2 >= row2) if d.reverse else (row2 >= col2)
        y_inter = _dot(d.cm[rows, ns], d.st[g].astype(BF16))
        cb2 = jnp.concatenate([cb, cb], axis=1)
        parts = []
        for p in range(SSD_HPG // 2):
            h0 = g * SSD_HPG + 2 * p
            ps = slice(h0 * SSD_HEADDIM, (h0 + 2) * SSD_HEADDIM)
            col = jnp.where(left, cum[:, h0:h0 + 1], cum[:, h0 + 1:h0 + 2])
            rowv = jnp.concatenate([cum_t[h0:h0 + 1, :], cum_t[h0 + 1:h0 + 2, :]], axis=1)
            dtv = jnp.concatenate([dt_t[h0:h0 + 1, :], dt_t[h0 + 1:h0 + 2, :]], axis=1)
            decay = jnp.exp(jnp.where(mask2, col - rowv, -jnp.inf))
            wgt = (cb2 * decay * dtv).astype(BF16)
            xp = d.xs[rows, ps]
            zero = jnp.zeros_like(xp)
            xbd = jnp.concatenate([jnp.where(left, xp, zero), jnp.where(left, zero, xp)], axis=0)
            parts.append(_dot(wgt, xbd))
        y = jnp.concatenate(parts, axis=1) + y_inter * fac_x[CHUNK:2 * CHUNK]
        d.y[rows, cs] = y.astype(d.y.dtype)

    def increment(w, g):
        d, c0 = work[w]
        rows, ns, cs = pl.ds(c0, CHUNK), slice(g * SSD_STATE, (g + 1) * SSD_STATE), slice(g * SSD_GW, (g + 1) * SSD_GW)
        return _dot_tn(d.bm[rows, ns], d.xs[rows, cs] * groups[w, g][0][0:CHUNK].astype(BF16))

    def update(w, g, inc):
        d = work[w][0]
        d.st[g] = d.st[g] * groups[w, g][1] + inc

    pending = None
    for w, g in units:
        inc = increment(w, g)
        if pending is not None:
            update(*pending)
        pending = (w, g, inc)
    update(*pending)


def _ssd_kernel(*refs, n_chunks, inner):
    ins, (dbf_r, dbb_r, alf_r, alb_r) = refs[:8 * inner], refs[8 * inner:8 * inner + 4]
    (yf_ref, yb_ref), states = refs[8 * inner + 4:8 * inner + 6], refs[8 * inner + 6:]

    @pl.when(pl.program_id(1) == 0)
    def _():
        for st in states:
            st[...] = jnp.zeros_like(st)

    fwd = [_SsdDir(*ins[8 * e:8 * e + 4], dbf_r, alf_r, yf_ref.at[e], states[2 * e], False, S_DTF, CHUNK - 1)
           for e in range(inner)]
    bwd = [_SsdDir(*ins[8 * e + 4:8 * e + 8], dbb_r, alb_r, yb_ref.at[e], states[2 * e + 1], True, S_DTB, 0)
           for e in range(inner)]

    def body(i, carry):
        c_f = pl.multiple_of(i * CHUNK, CHUNK)
        c_b = pl.multiple_of((n_chunks - 1 - i) * CHUNK, CHUNK)
        _ssd_chunks([(d, c_f) for d in fwd] + [(d, c_b) for d in bwd])
        return carry

    lax.fori_loop(0, n_chunks, body, 0)


def _ssd(xs_c, bm_c, cm_c, small, dtb_f, dtb_b, alog_f, alog_b, bsz, seq, lb):
    t = bsz * seq
    nb = seq // lb
    inner = SSD_INNER if bsz % SSD_INNER == 0 else 1

    def specs(e, reverse):
        idx = lambda p, n: (p * inner + e) * nb + (nb - 1 - n if reverse else n)
        return [pl.BlockSpec((lb, w), lambda p, n: (idx(p, n), 0)) for w in (SSD_DINNER, SSD_BC, SSD_BC, N_SMALL)]

    row = pl.BlockSpec((1, SSD_HEADS), lambda p, n: (0, 0))
    args = (xs_c, bm_c, cm_c, small)
    y_f, y_b = pl.pallas_call(
        functools.partial(_ssd_kernel, n_chunks=lb // CHUNK, inner=inner),
        grid=(bsz // inner, nb),
        in_specs=[s for e in range(inner) for rev in (False, True) for s in specs(e, rev)] + [row] * 4,
        out_specs=[pl.BlockSpec((None, inner, lb, SSD_DINNER), lambda p, n: (p, 0, n, 0)),
                   pl.BlockSpec((None, inner, lb, SSD_DINNER), lambda p, n: (p, 0, nb - 1 - n, 0))],
        out_shape=[jax.ShapeDtypeStruct((bsz // inner, inner, seq, SSD_DINNER), BF16)] * 2,
        scratch_shapes=[pltpu.VMEM((SSD_GROUPS, SSD_STATE, SSD_GW), F32)] * (2 * inner),
        compiler_params=pltpu.CompilerParams(
            dimension_semantics=("arbitrary", "arbitrary"), vmem_limit_bytes=VMEM_LIMIT),
        name="ssd_scan",
    )(*(args * (2 * inner)), dtb_f[None, :], dtb_b[None, :], alog_f[None, :], alog_b[None, :])
    return y_f.reshape(t, SSD_DINNER), y_b.reshape(t, SSD_DINNER)


def _merge_kernel(x_ref, of_ref, ob_ref, r_ref, gg_ref, yf_ref, yb_ref, xs_ref, z_ref, gs_ref,
                  gng_ref, dsk_ref, sng_ref, wug_ref, wus_ref, wo_ref, nfg_ref, wr_ref, br_ref,
                  h_ref, xp_ref, route_ref, gate_ref, cnt_out_ref, cnt_ref):
    @pl.when(pl.program_id(0) == 0)
    def _():
        cnt_ref[...] = jnp.zeros_like(cnt_ref)

    tm = x_ref.shape[0]
    o = of_ref[...].astype(F32) + ob_ref[...].astype(F32)
    gng = gng_ref[...]
    o_parts = []
    for h in range(GLA_HEADS):
        oh = o[:, h * GLA_DV:(h + 1) * GLA_DV]
        oh = oh * lax.rsqrt(jnp.mean(oh * oh, axis=-1, keepdims=True) + EPS)
        o_parts.append(oh * gng)
    o = jnp.concatenate(o_parts, axis=1) * _silu(r_ref[...].astype(F32))
    up_g = _dot(o.astype(BF16), wug_ref[...])
    y = yf_ref[...].astype(F32) + yb_ref[...].astype(F32) + dsk_ref[...] * xs_ref[...].astype(F32)
    y = y * _silu(z_ref[...].astype(F32))
    sng = sng_ref[...]
    y_parts = []
    for g in range(SSD_GROUPS):
        yg = y[:, g * SSD_GW:(g + 1) * SSD_GW]
        yg = yg * lax.rsqrt(jnp.mean(yg * yg, axis=-1, keepdims=True) + EPS)
        y_parts.append(yg * sng[:, g * SSD_GW:(g + 1) * SSD_GW])
    y = jnp.concatenate(y_parts, axis=1)
    up_s = _dot(y.astype(BF16), wus_ref[...])
    mix = _sigmoid(gg_ref[...].astype(F32)) * up_g + _sigmoid(gs_ref[...].astype(F32)) * up_s
    h = x_ref[...] + _dot(mix.astype(BF16), wo_ref[...])
    h_ref[...] = h
    hn = (h * lax.rsqrt(jnp.mean(h * h, axis=-1, keepdims=True) + EPS)) * nfg_ref[...]
    words = _pack_bf16_pair(hn[:, :D_MODEL // 2], hn[:, D_MODEL // 2:])
    xp_ref[0] = words[:, :SC_W]
    xp_ref[1] = words[:, SC_W:]
    hn_hi, hn_lo = _split_bf16(hn, 2)
    logits = _dot(hn_hi, wr_ref[0]) + (_dot(hn_hi, wr_ref[1]) + _dot(hn_lo, wr_ref[0])) + br_ref[...]
    lane = lax.broadcasted_iota(I32, (tm, N_EXPERTS), 1).astype(F32)
    work = logits
    idxs, vals = [], []
    for _ in range(TOP_K):
        m = jnp.max(work, axis=-1, keepdims=True)
        idx = jnp.min(jnp.where(work == m, lane, float(N_EXPERTS)), axis=-1, keepdims=True)
        idxs.append(idx)
        vals.append(m)
        work = jnp.where(lane == idx, -jnp.inf, work)
    exps = [jnp.exp(v - vals[0]) for v in vals]
    denom = exps[0] + exps[1] + exps[2] + exps[3]
    gates = [e / denom for e in exps]
    sel = jnp.zeros((tm, N_EXPERTS), F32)
    for idx in idxs:
        sel = sel + jnp.where(lane == idx, 1.0, 0.0)
    rr = lax.broadcasted_iota(I32, (tm, tm), 0)
    cc = lax.broadcasted_iota(I32, (tm, tm), 1)
    strict = jnp.where(rr > cc, 1.0, 0.0).astype(BF16)
    pos = _dot(strict, sel.astype(BF16)) + cnt_ref[0:1, :]
    ranks = [jnp.sum(jnp.where(lane == idx, pos, 0.0), axis=-1, keepdims=True).astype(I32) for idx in idxs]
    cnt_new = cnt_ref[0:1, :] + jnp.sum(sel, axis=0, keepdims=True)
    cnt_ref[...] = jnp.broadcast_to(cnt_new, cnt_ref.shape)
    cnt_out_ref[...] = jnp.broadcast_to(cnt_new, cnt_ref.shape).astype(I32)
    lane128 = lax.broadcasted_iota(I32, (tm, 128), 1)
    route = jnp.zeros((tm, 128), I32)
    gate_o = jnp.zeros((tm, 128), F32)
    for k in range(TOP_K):
        route = jnp.where(lane128 == k, idxs[k].astype(I32), route)
        route = jnp.where(lane128 == TOP_K + k, ranks[k], route)
        gate_o = jnp.where(lane128 == k, gates[k], gate_o)
    route_ref[...] = route
    gate_ref[...] = gate_o


def _merge_params(gla_norm_g, ssd_d, ssd_norm_g, w_up_gla, w_up_ssd, w_out, norm_ffn_g, w_router, b_router):
    d_skip = jnp.repeat(ssd_d, SSD_HEADDIM)[None, :]
    return [gla_norm_g[None, :], d_skip, ssd_norm_g[None, :], w_up_gla.astype(BF16), w_up_ssd.astype(BF16),
            w_out.astype(BF16), norm_ffn_g[None, :], jnp.stack(_split_bf16(w_router, 2)), b_router[None, :]]


def _merge(x2, row0, o_f, o_b, proj, y_f, y_b, xs_c, params, tm):
    t = o_f.shape[0]
    blk0 = row0 // tm
    rowblk = lambda w, col=0: pl.BlockSpec((tm, w), lambda i: (i, col))
    const = lambda a: pl.BlockSpec(a.shape, lambda i: (0,) * a.ndim)
    return pl.pallas_call(
        _merge_kernel,
        grid=(t // tm,),
        in_specs=[pl.BlockSpec((tm, D_MODEL), lambda i: (blk0 + i, 0)),
                  rowblk(GLA_V), rowblk(GLA_V), rowblk(GLA_V, C_R // GLA_V),
                  rowblk(D_MODEL, C_GG // D_MODEL), rowblk(SSD_DINNER), rowblk(SSD_DINNER), rowblk(SSD_DINNER),
                  rowblk(SSD_DINNER, C_Z // SSD_DINNER), rowblk(D_MODEL, C_GS // D_MODEL)]
        + [const(p) for p in params],
        out_specs=[rowblk(D_MODEL), pl.BlockSpec((2, tm, SC_W), lambda i: (0, i, 0)), rowblk(128), rowblk(128),
                   pl.BlockSpec((8, N_EXPERTS), lambda i: (0, 0))],
        out_shape=[jax.ShapeDtypeStruct((t, D_MODEL), F32), jax.ShapeDtypeStruct((2, t, SC_W), U32),
                   jax.ShapeDtypeStruct((t, 128), I32), jax.ShapeDtypeStruct((t, 128), F32),
                   jax.ShapeDtypeStruct((8, N_EXPERTS), I32)],
        scratch_shapes=[pltpu.VMEM((8, N_EXPERTS), F32)],
        compiler_params=pltpu.CompilerParams(dimension_semantics=("arbitrary",), vmem_limit_bytes=VMEM_LIMIT),
        name="merge_router",
    )(x2, o_f, o_b, proj, proj, y_f, y_b, xs_c, proj, proj, *params)


def _sc_mesh():
    return plsc.VectorSubcoreMesh(core_axis_name="c", subcore_axis_name="s")


def _sc_scatter_rows(x, idx, n_out):
    n, m = x.shape[0], idx.shape[1]
    n_win = n // SC_WIN

    @functools.partial(pl.kernel, out_type=jax.ShapeDtypeStruct((n_out, SC_W), x.dtype), mesh=_sc_mesh())
    def scatter(x_hbm, i_hbm, o_hbm):
        def body(x_vmem, i_vmem):
            pltpu.sync_copy(x_vmem, o_hbm.at[i_vmem.at[0]])

        pltpu.emit_pipeline(
            body, grid=(m // SC_WIN,),
            in_specs=[pl.BlockSpec((SC_WIN, SC_W), lambda i: (i % n_win, 0)),
                      pl.BlockSpec((1, SC_WIN), lambda i: (0, i))],
            out_specs=[], core_axis_name=("c", "s"), dimension_semantics=(pltpu.PARALLEL,),
        )(x_hbm, i_hbm)

    return scatter(x, idx)


def _sc_gather_rows(table, idx):
    m = idx.shape[1]

    @functools.partial(pl.kernel, out_type=jax.ShapeDtypeStruct((m, SC_W), table.dtype), mesh=_sc_mesh())
    def gather(t_hbm, i_hbm, o_hbm):
        def body(i_vmem, o_vmem):
            pltpu.sync_copy(t_hbm.at[i_vmem.at[0]], o_vmem)

        pltpu.emit_pipeline(
            body, grid=(m // SC_WIN,),
            in_specs=[pl.BlockSpec((1, SC_WIN), lambda i: (0, i))],
            out_specs=[pl.BlockSpec((SC_WIN, SC_W), lambda i: (i, 0))],
            core_axis_name=("c", "s"), dimension_semantics=(pltpu.PARALLEL,),
        )(i_hbm, o_hbm)

    return gather(table, idx)


def _expert_kernel(blk_e_ref, nvalid_ref, nused_ref, x_ref, w1_ref, b1_ref, w2_ref, b2_ref, o_ref, w1_s, w2_s):
    del nused_ref
    i = pl.program_id(0)
    nvalid = nvalid_ref[i]

    @pl.when((nvalid > 0) & ((i == 0) | (blk_e_ref[i] != blk_e_ref[jnp.maximum(i - 1, 0)])))
    def _():
        w1_s[...] = w1_ref[0].astype(BF16)
        w2_s[...] = w2_ref[0].astype(BF16)

    @pl.when(nvalid > 0)
    def _():
        live = lax.broadcasted_iota(I32, (MOE_BM, SC_W), 0) < nvalid
        lo0, hi0 = _unpack_bf16_pair(jnp.where(live, x_ref[0], jnp.uint32(0)))
        lo1, hi1 = _unpack_bf16_pair(jnp.where(live, x_ref[1], jnp.uint32(0)))
        x = jnp.concatenate([lo0, lo1, hi0, hi1], axis=1).astype(BF16)
        hdn = _dot(x, w1_s[...]) + b1_ref[0]
        gate = jnp.minimum(hdn[:, :D_FF], SWIGLU_LIMIT)
        lin = jnp.clip(hdn[:, D_FF:], -SWIGLU_LIMIT, SWIGLU_LIMIT)
        act = gate * _sigmoid(SWIGLU_ALPHA * gate) * (lin + 1.0)
        y = _dot(act.astype(BF16), w2_s[...]) + b2_ref[0]
        words = _pack_bf16_pair(y[:, :D_MODEL // 2], y[:, D_MODEL // 2:])
        o_ref[0] = words[:, :SC_W]
        o_ref[1] = words[:, SC_W:]

    @pl.when(nvalid == 0)
    def _():
        o_ref[...] = jnp.zeros_like(o_ref)


def _experts(xb, blk_e, nvalid, n_used, w1, b1, w2, b2):
    n_rows = xb.shape[1]
    n_blocks = n_rows // MOE_BM
    xidx = lambda i, be, nv, nu: (0, jnp.minimum(i, nu[0] - 1), 0)
    eidx = lambda i, be, nv, nu: (be[i], 0, 0)
    return pl.pallas_call(
        _expert_kernel,
        grid_spec=pltpu.PrefetchScalarGridSpec(
            num_scalar_prefetch=3,
            grid=(n_blocks,),
            in_specs=[pl.BlockSpec((2, MOE_BM, SC_W), xidx),
                      pl.BlockSpec((1, D_MODEL, 2 * D_FF), eidx),
                      pl.BlockSpec((1, 1, 2 * D_FF), eidx),
                      pl.BlockSpec((1, D_FF, D_MODEL), eidx),
                      pl.BlockSpec((1, 1, D_MODEL), eidx)],
            out_specs=pl.BlockSpec((2, MOE_BM, SC_W), lambda i, be, nv, nu: (0, i, 0)),
            scratch_shapes=[pltpu.VMEM((D_MODEL, 2 * D_FF), BF16), pltpu.VMEM((D_FF, D_MODEL), BF16)],
        ),
        out_shape=jax.ShapeDtypeStruct((2, n_rows, SC_W), U32),
        compiler_params=pltpu.CompilerParams(dimension_semantics=("arbitrary",), vmem_limit_bytes=VMEM_LIMIT),
        name="moe_experts",
    )(blk_e, nvalid, n_used, xb, w1, b1, w2, b2)


def _combine_kernel(h_ref, gate_ref, y_ref, g_ref, *rest):
    o_ref = rest[-1]
    gates = gate_ref[...]
    acc = None
    for k in range(TOP_K):
        lo0, hi0 = _unpack_bf16_pair(y_ref[k, 0])
        lo1, hi1 = _unpack_bf16_pair(y_ref[k, 1])
        term = gates[:, k:k + 1] * jnp.concatenate([lo0, lo1, hi0, hi1], axis=1)
        acc = term if acc is None else acc + term
    h = h_ref[...] + acc
    o_ref[...] = (h * lax.rsqrt(jnp.mean(h * h, axis=-1, keepdims=True) + EPS)) * g_ref[...]


def _combine(h, gates, y_rows, norm_final_g, t_total, row0, prev, tm):
    t = h.shape[0]
    blk0 = row0 // tm
    in_specs = [pl.BlockSpec((tm, D_MODEL), lambda i: (i, 0)),
                pl.BlockSpec((tm, 128), lambda i: (i, 0)),
                pl.BlockSpec((TOP_K, 2, tm, SC_W), lambda i: (0, 0, i, 0)),
                pl.BlockSpec((1, D_MODEL), lambda i: (0, 0))]
    args = [h, gates, y_rows, norm_final_g[None, :]]
    if prev is not None:
        in_specs.append(pl.BlockSpec(memory_space=pl.ANY))
        args.append(prev)
    return pl.pallas_call(
        _combine_kernel,
        grid=(t // tm,),
        in_specs=in_specs,
        out_specs=pl.BlockSpec((tm, D_MODEL), lambda i: (blk0 + i, 0)),
        out_shape=jax.ShapeDtypeStruct((t_total, D_MODEL), F32),
        input_output_aliases={} if prev is None else {4: 0},
        compiler_params=pltpu.CompilerParams(dimension_semantics=("arbitrary",), vmem_limit_bytes=VMEM_LIMIT),
        name="moe_combine",
    )(*args)


def _pick(n, pref):
    b = min(pref, n)
    while n % b:
        b -= CHUNK
    return b


def _prep_w_in(w_in):
    widths = (GLA_QK, GLA_QK, GLA_V, GLA_V, GLA_RANK, GLA_RANK, SSD_DINNER, SSD_DINNER, SSD_BC, SSD_BC,
              2 * SSD_HEADS, D_MODEL, D_MODEL)
    pts, acc = [], 0
    for w in widths[:-1]:
        acc += w
        pts.append(acc)
    q, k, v, r, lrf, lrb, z, xs, bm, cm, dtr, gg, gs = jnp.split(w_in, pts, axis=1)
    main = jnp.concatenate([z, xs, q, k, v, r, gg, gs, bm, cm], axis=1).astype(BF16)
    pad = jnp.zeros((D_MODEL, N_SMALL - 2 * GLA_RANK - 2 * SSD_HEADS), w_in.dtype)
    small = jnp.concatenate([lrf, lrb, dtr, pad], axis=1).astype(BF16)
    return main, small


def _layer(h, norm_mix_g, w_in, gla_fw2_f, gla_fb_f, gla_fw2_b, gla_fb_b, gla_norm_g, conv_w, conv_b,
           dt_bias_f, dt_bias_b, a_log_f, a_log_b, ssd_d, ssd_norm_g, w_up_gla, w_up_ssd, w_out,
           norm_ffn_g, w_router, b_router, w1, b1, w2, b2, out_norm_g):
    bsz, seq, _ = h.shape
    t_total = bsz * seq
    x2 = h.reshape(t_total, D_MODEL)
    w_main, w_small = _prep_w_in(w_in)
    fw2f, fw2b = gla_fw2_f.astype(BF16), gla_fw2_b.astype(BF16)
    mparams = _merge_params(gla_norm_g, ssd_d, ssd_norm_g, w_up_gla, w_up_ssd, w_out, norm_ffn_g, w_router, b_router)
    n_groups = TOKEN_GROUPS if bsz % TOKEN_GROUPS == 0 else 1
    gb = bsz // n_groups
    t = gb * seq
    lb = _pick(seq, 512)
    out = None
    for grp in range(n_groups):
        row0 = grp * t
        proj, small = _inproj(x2, norm_mix_g[None, :], w_main, w_small, row0, t, tm=_pick(t, 2048), tn=1024)
        o_f, o_b = _gla(proj, small, fw2f, gla_fb_f[None, :], fw2b, gla_fb_b[None, :], gb, seq, lb)
        xs_c, bm_c, cm_c = _conv(proj, conv_w, conv_b, seq, _pick(seq, 512))
        y_f, y_b = _ssd(xs_c, bm_c, cm_c, small, dt_bias_f, dt_bias_b, a_log_f, a_log_b, gb, seq, lb)
        hres, xp, route, gates, counts8 = _merge(x2, row0, o_f, o_b, proj, y_f, y_b, xs_c, mparams, tm=_pick(t, 256))
        counts = counts8[0]
        padded = (counts + MOE_BM - 1) // MOE_BM * MOE_BM
        pend = jnp.cumsum(padded)
        pstart = (pend - padded).astype(I32)
        n_rows = t * TOP_K + N_EXPERTS * MOE_BM
        blk_row = jnp.arange(n_rows // MOE_BM, dtype=I32) * MOE_BM
        blk_e = jnp.minimum(jnp.sum(pend[None, :] <= blk_row[:, None], axis=1), N_EXPERTS - 1).astype(I32)
        nvalid = jnp.clip(pstart[blk_e] + counts[blk_e] - blk_row, 0, MOE_BM).astype(I32)
        n_used = (pend[-1:] // MOE_BM).astype(I32)
        top_e, rank = route[:, :TOP_K], route[:, TOP_K:2 * TOP_K]
        dest = jnp.sum(jnp.where(top_e[:, :, None] == jnp.arange(N_EXPERTS, dtype=I32), pstart, 0), axis=-1) + rank
        idx = (dest.T[:, None, :] + (jnp.arange(2, dtype=I32) * n_rows)[None, :, None]).reshape(1, 2 * TOP_K * t)
        xb = _sc_scatter_rows(xp.reshape(2 * t, SC_W), idx, 2 * n_rows).reshape(2, n_rows, SC_W)
        yb = _experts(xb, blk_e, nvalid, n_used, w1, b1[:, None, :], w2, b2[:, None, :])
        y_rows = _sc_gather_rows(yb.reshape(2 * n_rows, SC_W), idx).reshape(TOP_K, 2, t, SC_W)
        out = _combine(hres, gates, y_rows, out_norm_g, t_total, row0, out, tm=_pick(t, 256))
    return out.reshape(bsz, seq, D_MODEL)


def kernel(x, norm_mix_g, w_in, gla_fw2_f, gla_fb_f, gla_fw2_b, gla_fb_b, gla_norm_g, conv_w, conv_b, dt_bias_f,
           dt_bias_b, a_log_f, a_log_b, ssd_d, ssd_norm_g, w_up_gla, w_up_ssd, w_out, norm_ffn_g, w_router,
           b_router, w1, b1, w2, b2, norm_final_g):
    assert x.shape[-1] == D_MODEL and norm_mix_g.shape[0] == 1
    return _layer(x, norm_mix_g[0], w_in[0], gla_fw2_f[0], gla_fb_f[0], gla_fw2_b[0], gla_fb_b[0], gla_norm_g[0],
                  conv_w[0], conv_b[0], dt_bias_f[0], dt_bias_b[0], a_log_f[0], a_log_b[0], ssd_d[0],
                  ssd_norm_g[0], w_up_gla[0], w_up_ssd[0], w_out[0], norm_ffn_g[0], w_router[0], b_router[0],
                  w1[0], b1[0], w2[0], b2[0], norm_final_g)
```

```python
import functools

import jax
import jax.numpy as jnp
from jax import lax
from jax.experimental import pallas as pl
from jax.experimental.pallas import tpu as pltpu
from jax.experimental.pallas import tpu_sc as plsc

F32 = jnp.float32
BF16 = jnp.bfloat16
I32 = jnp.int32
U32 = jnp.uint32

D_MODEL = 1024
EPS = 1e-5
GLA_HEADS = 4
GLA_DK = 128
GLA_DV = 256
GLA_RANK = 16
GLA_TAU = 16.0
GLA_QK = GLA_HEADS * GLA_DK
GLA_V = GLA_HEADS * GLA_DV
SSD_DINNER = 2048
SSD_HEADDIM = 64
SSD_HEADS = 32
SSD_GROUPS = 4
SSD_HPG = 8
SSD_STATE = 128
SSD_CONV = 4
SSD_BC = SSD_GROUPS * SSD_STATE
SSD_GW = SSD_HPG * SSD_HEADDIM
N_EXPERTS = 32
TOP_K = 4
D_FF = 1024
SWIGLU_LIMIT = 7.0
SWIGLU_ALPHA = 1.702
CHUNK = 64
TOKEN_GROUPS = 2
MERGE_SUBTILES = 2
SSD_INNER = 2
MXU_LAG = 6

C_Z, C_XS, C_Q, C_K, C_V, C_R, C_GG, C_GS, C_B, C_C = 0, 2048, 4096, 4608, 5120, 6144, 7168, 8192, 9216, 9728
N_MAIN = 10240
N_SMALL = 128
S_LRF, S_LRB, S_DTF, S_DTB = 0, 16, 32, 64

VMEM_LIMIT = 56 * 1024 * 1024
MOE_BM = 512
SC_WIN = 128
SC_W = D_MODEL // 4


def _dot(a, b):
    return jnp.dot(a, b, preferred_element_type=F32)


def _dot_nt(a, b):
    return lax.dot_general(a, b, (((1,), (1,)), ((), ())), preferred_element_type=F32)


def _dot_tn(a, b):
    return lax.dot_general(a, b, (((0,), (0,)), ((), ())), preferred_element_type=F32)


def _split_bf16(x, n):
    parts = []
    r = x
    for _ in range(n):
        p = r.astype(BF16)
        parts.append(p)
        r = r - p.astype(F32)
    return parts


def _dot_exact_lhs(m_bf16, x, n):
    acc = None
    for p in _split_bf16(x, n):
        t = _dot(m_bf16, p)
        acc = t if acc is None else acc + t
    return acc


def _dot_exact_rhs(x, m_bf16, n):
    acc = None
    for p in _split_bf16(x, n):
        t = _dot(p, m_bf16)
        acc = t if acc is None else acc + t
    return acc


def _sigmoid(x):
    return 1.0 / (1.0 + jnp.exp2(x * (-1.4426950408889634)))


def _silu(x):
    return x * _sigmoid(x)


def _pack_bf16_pair(lo, hi):
    lo_b = lax.bitcast_convert_type(lo.astype(BF16).astype(F32), U32)
    hi_b = lax.bitcast_convert_type(hi.astype(BF16).astype(F32), U32)
    return (lo_b >> 16) | (hi_b & jnp.uint32(0xFFFF0000))


def _unpack_bf16_pair(u):
    lo = lax.bitcast_convert_type(u << 16, F32)
    hi = lax.bitcast_convert_type(u & jnp.uint32(0xFFFF0000), F32)
    return lo, hi


def _inproj_kernel(x_ref, g_ref, w_ref, ws_ref, o_ref, os_ref, xn_ref):
    @pl.when(pl.program_id(1) == 0)
    def _():
        x = x_ref[...]
        ms = jnp.mean(x * x, axis=-1, keepdims=True)
        xb = ((x * lax.rsqrt(ms + EPS)) * g_ref[...]).astype(BF16)
        xn_ref[...] = xb
        os_ref[...] = _dot(xb, ws_ref[...])

    o_ref[...] = _dot(xn_ref[...], w_ref[...]).astype(BF16)


def _inproj(x2, g, w_main, w_small, row0, t, tm, tn):
    blk0 = row0 // tm
    return pl.pallas_call(
        _inproj_kernel,
        grid=(t // tm, N_MAIN // tn),
        in_specs=[
            pl.BlockSpec((tm, D_MODEL), lambda i, j: (blk0 + i, 0)),
            pl.BlockSpec((1, D_MODEL), lambda i, j: (0, 0)),
            pl.BlockSpec((D_MODEL, tn), lambda i, j: (0, j)),
            pl.BlockSpec((D_MODEL, N_SMALL), lambda i, j: (0, 0)),
        ],
        out_specs=[
            pl.BlockSpec((tm, tn), lambda i, j: (i, j)),
            pl.BlockSpec((tm, N_SMALL), lambda i, j: (i, 0)),
        ],
        out_shape=[
            jax.ShapeDtypeStruct((t, N_MAIN), BF16),
            jax.ShapeDtypeStruct((t, N_SMALL), F32),
        ],
        scratch_shapes=[pltpu.VMEM((tm, D_MODEL), BF16)],
        compiler_params=pltpu.CompilerParams(
            dimension_semantics=("arbitrary", "arbitrary"), vmem_limit_bytes=VMEM_LIMIT),
        name="inproj",
    )(x2, g, w_main, w_small)


def _tri_masks(n):
    r = lax.broadcasted_iota(I32, (n, n), 0)
    c = lax.broadcasted_iota(I32, (n, n), 1)
    return r >= c, c >= r


class _GlaDir:
    def __init__(self, ins, outs, scratch, mask, lr_off, mid_row, last_row):
        self.q, self.k, self.v, self.sm, self.fw2, self.fb = ins
        self.o, self.st = outs
        self.b_s, self.qs_s, self.ks_s, self.kd_s, self.qe_s, self.oi_s, self.u_s, self.el_s = scratch
        self.mask, self.lr_off, self.mid_row, self.last_row = mask, lr_off, mid_row, last_row


def _gla_decay(d, n_chunks):
    tri = jnp.where(d.mask, 1.0, 0.0).astype(BF16)
    lr = d.sm[:, d.lr_off:d.lr_off + GLA_RANK].astype(BF16)
    xg = _dot(lr, d.fw2[...]) + d.fb[...]
    d.b_s[...] = (jnp.minimum(xg, 0.0) - jnp.log(1.0 + jnp.exp(-jnp.abs(xg)))) * (1.0 / GLA_TAU)
    for c in range(n_chunks):
        rows = slice(c * CHUNK, (c + 1) * CHUNK)
        d.b_s[rows, :] = _dot_exact_lhs(tri, d.b_s[rows, :], 2)


def _gla_scale(d, c):
    rows = slice(c * CHUNK, (c + 1) * CHUNK)
    b = d.b_s[rows, :]
    b_mid = b[d.mid_row:d.mid_row + 1, :]
    b_last = b[d.last_row:d.last_row + 1, :]
    q = d.q[rows, :].astype(F32) * (GLA_DK ** -0.5)
    k = d.k[rows, :].astype(F32)
    d.qs_s[rows, :] = (q * jnp.exp(b - b_mid)).astype(BF16)
    d.ks_s[rows, :] = (k * jnp.exp(b_mid - b)).astype(BF16)
    d.kd_s[rows, :] = (k * jnp.exp(b_last - b)).astype(BF16)
    d.qe_s[rows, :] = (q * jnp.exp(b)).astype(BF16)
    d.el_s[c:c + 1, :] = jnp.exp(b_last)


def _gla_local(units):
    def score(u):
        d, c, h = u
        rows, ks_ = slice(c * CHUNK, (c + 1) * CHUNK), slice(h * GLA_DK, (h + 1) * GLA_DK)
        return _dot_nt(d.qs_s[rows, ks_], d.ks_s[rows, ks_])

    def finish(u, s):
        d, c, h = u
        rows, ks_ = slice(c * CHUNK, (c + 1) * CHUNK), slice(h * GLA_DK, (h + 1) * GLA_DK)
        vs_ = slice(h * GLA_DV, (h + 1) * GLA_DV)
        p = jnp.where(d.mask, s, 0.0).astype(BF16)
        d.oi_s[rows, vs_] = _dot(p, d.v[rows, vs_])
        d.u_s[c, h] = _dot_tn(d.kd_s[rows, ks_], d.v[rows, vs_])

    pending = []
    for u in units:
        pending.append((u, score(u)))
        if len(pending) > MXU_LAG:
            finish(*pending.pop(0))
    for item in pending:
        finish(*item)


def _gla_carry(d, c):
    rows = slice(c * CHUNK, (c + 1) * CHUNK)
    for h in range(GLA_HEADS):
        ks_ = slice(h * GLA_DK, (h + 1) * GLA_DK)
        vs_ = slice(h * GLA_DV, (h + 1) * GLA_DV)
        st = d.st[h]
        d.o[rows, vs_] = (d.oi_s[rows, vs_] + _dot(d.qe_s[rows, ks_], st.astype(BF16))).astype(d.o.dtype)
        e_col = jnp.transpose(jnp.broadcast_to(d.el_s[c:c + 1, ks_], (8, GLA_DK)))[:, 0:1]
        d.st[h] = st * e_col + d.u_s[c, h]


def _gla_kernel(*refs, n_chunks):
    ins_f, ins_b, (fw2f, fbf, fw2b, fbb), (of_ref, ob_ref, stf, stb) = refs[0:4], refs[4:8], refs[8:12], refs[12:16]
    scr_f, scr_b = refs[16:24], refs[24:32]

    @pl.when(pl.program_id(1) == 0)
    def _():
        stf[...] = jnp.zeros_like(stf)
        stb[...] = jnp.zeros_like(stb)

    lower, upper = _tri_masks(CHUNK)
    fwd = _GlaDir((*ins_f, fw2f, fbf), (of_ref, stf), scr_f, lower, S_LRF, CHUNK // 2, CHUNK - 1)
    bwd = _GlaDir((*ins_b, fw2b, fbb), (ob_ref, stb), scr_b, upper, S_LRB, CHUNK // 2 - 1, 0)
    _gla_decay(fwd, n_chunks)
    _gla_decay(bwd, n_chunks)
    for c in range(n_chunks):
        _gla_scale(fwd, c)
        _gla_scale(bwd, c)
    _gla_local([(d, c, h) for c in range(n_chunks) for d in (fwd, bwd) for h in range(GLA_HEADS)])
    for i in range(n_chunks):
        _gla_carry(fwd, i)
        _gla_carry(bwd, n_chunks - 1 - i)


def _gla(proj, small, fw2f, fbf, fw2b, fbb, bsz, seq, lb):
    t = bsz * seq
    nb = seq // lb

    def fwd(w, col):
        return pl.BlockSpec((lb, w), lambda b, n: (b * nb + n, col))

    def bwd(w, col):
        return pl.BlockSpec((lb, w), lambda b, n: (b * nb + nb - 1 - n, col))

    const = lambda shape: pl.BlockSpec(shape, lambda b, n: (0, 0))
    per_dir_scratch = ([pltpu.VMEM((lb, GLA_QK), F32)] + [pltpu.VMEM((lb, GLA_QK), BF16)] * 4
                       + [pltpu.VMEM((lb, GLA_V), F32),
                          pltpu.VMEM((lb // CHUNK, GLA_HEADS, GLA_DK, GLA_DV), F32),
                          pltpu.VMEM((lb // CHUNK, GLA_QK), F32)])
    return pl.pallas_call(
        functools.partial(_gla_kernel, n_chunks=lb // CHUNK),
        grid=(bsz, nb),
        in_specs=[
            fwd(GLA_QK, C_Q // GLA_QK), fwd(GLA_QK, C_K // GLA_QK), fwd(GLA_V, C_V // GLA_V), fwd(N_SMALL, 0),
            bwd(GLA_QK, C_Q // GLA_QK), bwd(GLA_QK, C_K // GLA_QK), bwd(GLA_V, C_V // GLA_V), bwd(N_SMALL, 0),
            const((GLA_RANK, GLA_QK)), const((1, GLA_QK)), const((GLA_RANK, GLA_QK)), const((1, GLA_QK)),
        ],
        out_specs=[fwd(GLA_V, 0), bwd(GLA_V, 0)],
        out_shape=[jax.ShapeDtypeStruct((t, GLA_V), BF16)] * 2,
        scratch_shapes=[pltpu.VMEM((GLA_HEADS, GLA_DK, GLA_DV), F32)] * 2 + per_dir_scratch * 2,
        compiler_params=pltpu.CompilerParams(
            dimension_semantics=("arbitrary", "arbitrary"), vmem_limit_bytes=VMEM_LIMIT),
        name="gla_scan",
    )(proj, proj, proj, small, proj, proj, proj, small, fw2f, fbf, fw2b, fbb)


HALO = 16


def _conv_taps(x, xm1, xp1, xp2, w_ref, b_ref):
    w = w_ref[...]
    return _silu(xm1 * w[0:1, :] + x * w[1:2, :] + xp1 * w[2:3, :] + xp2 * w[3:4, :] + b_ref[...])


def _conv_one(x_ref, p_ref, n_ref, w_ref, b_ref, o_ref, has_prev, has_next):
    rb = x_ref.shape[0]
    x = x_ref[...].astype(F32)
    y = _conv_taps(x, pltpu.roll(x, 1, 0), pltpu.roll(x, rb - 1, 0), pltpu.roll(x, rb - 2, 0), w_ref, b_ref)
    o_ref[...] = y.astype(o_ref.dtype)
    prev = jnp.where(has_prev, p_ref[HALO - 1:HALO, :].astype(F32), 0.0)
    nxt = jnp.where(has_next, n_ref[0:2, :].astype(F32), 0.0)
    row = lax.broadcasted_iota(I32, (HALO, x.shape[1]), 0)
    head = x_ref[0:2 * HALO, :].astype(F32)
    xm1 = jnp.where(row == 0, prev, pltpu.roll(head, 1, 0)[0:HALO])
    o_ref[0:HALO, :] = _conv_taps(head[0:HALO], xm1, pltpu.roll(head, 2 * HALO - 1, 0)[0:HALO],
                                  pltpu.roll(head, 2 * HALO - 2, 0)[0:HALO], w_ref, b_ref).astype(o_ref.dtype)
    tail = x_ref[rb - 2 * HALO:rb, :].astype(F32)
    xp1 = jnp.where(row == HALO - 1, nxt[0:1, :], pltpu.roll(tail, 2 * HALO - 1, 0)[HALO:])
    xp2 = jnp.where(row == HALO - 2, nxt[0:1, :],
                    jnp.where(row == HALO - 1, nxt[1:2, :], pltpu.roll(tail, 2 * HALO - 2, 0)[HALO:]))
    o_ref[rb - HALO:rb, :] = _conv_taps(tail[HALO:], pltpu.roll(tail, 1, 0)[HALO:], xp1, xp2,
                                        w_ref, b_ref).astype(o_ref.dtype)


def _conv_kernel(xs, xsp, xsn, bm, bmp, bmn, cm, cmp_, cmn, wx, bx, wb, bb, wc, bc, oxs, obm, ocm, *, rb, seq):
    t0 = pl.program_id(0) * rb
    has_prev = (t0 % seq) != 0
    has_next = ((t0 + rb) % seq) != 0
    _conv_one(xs, xsp, xsn, wx, bx, oxs, has_prev, has_next)
    _conv_one(bm, bmp, bmn, wb, bb, obm, has_prev, has_next)
    _conv_one(cm, cmp_, cmn, wc, bc, ocm, has_prev, has_next)


def _conv(proj, conv_w, conv_b, seq, rb):
    t = proj.shape[0]
    nh = t // HALO
    per = rb // HALO

    def trio(w, col):
        cb = col // w
        return [
            pl.BlockSpec((rb, w), lambda i: (i, cb)),
            pl.BlockSpec((HALO, w), lambda i: (jnp.maximum(i * per - 1, 0), cb)),
            pl.BlockSpec((HALO, w), lambda i: (jnp.minimum((i + 1) * per, nh - 1), cb)),
        ]

    def wspecs(w):
        return [pl.BlockSpec((SSD_CONV, w), lambda i: (0, 0)), pl.BlockSpec((1, w), lambda i: (0, 0))]

    wx, wb, wc = conv_w[:, :SSD_DINNER], conv_w[:, SSD_DINNER:SSD_DINNER + SSD_BC], conv_w[:, SSD_DINNER + SSD_BC:]
    bx, bb, bc = (conv_b[None, :SSD_DINNER], conv_b[None, SSD_DINNER:SSD_DINNER + SSD_BC],
                  conv_b[None, SSD_DINNER + SSD_BC:])
    return pl.pallas_call(
        functools.partial(_conv_kernel, rb=rb, seq=seq),
        grid=(t // rb,),
        in_specs=trio(SSD_DINNER, C_XS) + trio(SSD_BC, C_B) + trio(SSD_BC, C_C)
        + wspecs(SSD_DINNER) + wspecs(SSD_BC) + wspecs(SSD_BC),
        out_specs=[pl.BlockSpec((rb, SSD_DINNER), lambda i: (i, 0)),
                   pl.BlockSpec((rb, SSD_BC), lambda i: (i, 0)),
                   pl.BlockSpec((rb, SSD_BC), lambda i: (i, 0))],
        out_shape=[jax.ShapeDtypeStruct((t, SSD_DINNER), BF16),
                   jax.ShapeDtypeStruct((t, SSD_BC), BF16),
                   jax.ShapeDtypeStruct((t, SSD_BC), BF16)],
        compiler_params=pltpu.CompilerParams(dimension_semantics=("arbitrary",), vmem_limit_bytes=VMEM_LIMIT),
        name="ssd_conv",
    )(proj, proj, proj, proj, proj, proj, proj, proj, proj, wx, bx, wb, bb, wc, bc)


def _softplus(x):
    return jnp.maximum(x, 0.0) + jnp.log(1.0 + jnp.exp(-jnp.abs(x)))


class _SsdDir:
    def __init__(self, xs, bm, cm, sm, dtb_row, alog_row, y, st, reverse, dt_off, last_row):
        self.xs, self.bm, self.cm, self.sm, self.dtb_row, self.alog_row = xs, bm, cm, sm, dtb_row, alog_row
        self.y, self.st, self.reverse, self.dt_off, self.last_row = y, st, reverse, dt_off, last_row


def _ssd_chunks(work):
    hh = lax.broadcasted_iota(I32, (SSD_HEADS, SSD_GW), 0)
    cc = lax.broadcasted_iota(I32, (SSD_HEADS, SSD_GW), 1)
    lane = lax.broadcasted_iota(I32, (CHUNK, 2 * SSD_HEADDIM), 1)
    row2 = lax.broadcasted_iota(I32, (CHUNK, 2 * SSD_HEADDIM), 0)
    left = lane < SSD_HEADDIM
    col2 = jnp.where(left, lane, lane - SSD_HEADDIM)
    lower, upper = _tri_masks(CHUNK)

    pre = []
    for d, c0 in work:
        rows = pl.ds(c0, CHUNK)
        tri = jnp.where(upper if d.reverse else lower, 1.0, 0.0).astype(BF16)
        a_row = -jnp.exp(d.alog_row[...])
        dt = _softplus(d.sm[rows, d.dt_off:d.dt_off + SSD_HEADS] + d.dtb_row[...])
        pre.append((dt, _dot_exact_lhs(tri, dt * a_row, 3)))
    heads = []
    for (d, c0), (dt, cum) in zip(work, pre):
        total = cum[d.last_row:d.last_row + 1, :]
        to_end = jnp.exp(total - cum) * dt
        fac = jnp.concatenate([to_end, jnp.exp(cum)], axis=0).astype(BF16)
        e_tot = jnp.broadcast_to(jnp.exp(total), (8, SSD_HEADS))
        heads.append((cum, jnp.transpose(cum), jnp.transpose(dt), fac, e_tot))

    units = [(w, g) for w in range(len(work)) for g in range(SSD_GROUPS)]
    groups = {}
    for w, g in units:
        d, c0 = work[w]
        rows, ns = pl.ds(c0, CHUNK), slice(g * SSD_STATE, (g + 1) * SSD_STATE)
        fac, e_tot = heads[w][3], heads[w][4]
        expand = jnp.where(cc // SSD_HEADDIM + g * SSD_HPG == hh, 1.0, 0.0).astype(BF16)
        fac_x = _dot(fac, expand)
        et_x = _dot_exact_rhs(e_tot, expand, 2)[0:1]
        groups[w, g] = (fac_x, et_x, _dot_nt(d.cm[rows, ns], d.bm[rows, ns]))

    for w, g in units:
        d, c0 = work[w]
        rows, ns, cs = pl.ds(c0, CHUNK), slice(g * SSD_STATE, (g + 1) * SSD_STATE), slice(g * SSD_GW, (g + 1) * SSD_GW)
        cum, cum_t, dt_t = heads[w][0:3]
        fac_x, _, cb = groups[w, g]
        mask2 = (col2 >= row2) if d.reverse else (row2 >= col2)
        y_inter = _dot(d.cm[rows, ns], d.st[g].astype(BF16))
        cb2 = jnp.concatenate([cb, cb], axis=1)
        parts = []
        for p in range(SSD_HPG // 2):
            h0 = g * SSD_HPG + 2 * p
            ps = slice(h0 * SSD_HEADDIM, (h0 + 2) * SSD_HEADDIM)
            col = jnp.where(left, cum[:, h0:h0 + 1], cum[:, h0 + 1:h0 + 2])
            rowv = jnp.concatenate([cum_t[h0:h0 + 1, :], cum_t[h0 + 1:h0 + 2, :]], axis=1)
            dtv = jnp.concatenate([dt_t[h0:h0 + 1, :], dt_t[h0 + 1:h0 + 2, :]], axis=1)
            decay = jnp.exp(jnp.where(mask2, col - rowv, -jnp.inf))
            wgt = (cb2 * decay * dtv).astype(BF16)
            xp = d.xs[rows, ps]
            zero = jnp.zeros_like(xp)
            xbd = jnp.concatenate([jnp.where(left, xp, zero), jnp.where(left, zero, xp)], axis=0)
            parts.append(_dot(wgt, xbd))
        y = jnp.concatenate(parts, axis=1) + y_inter * fac_x[CHUNK:2 * CHUNK]
        d.y[rows, cs] = y.astype(d.y.dtype)

    def increment(w, g):
        d, c0 = work[w]
        rows, ns, cs = pl.ds(c0, CHUNK), slice(g * SSD_STATE, (g + 1) * SSD_STATE), slice(g * SSD_GW, (g + 1) * SSD_GW)
        return _dot_tn(d.bm[rows, ns], d.xs[rows, cs] * groups[w, g][0][0:CHUNK].astype(BF16))

    def update(w, g, inc):
        d = work[w][0]
        d.st[g] = d.st[g] * groups[w, g][1] + inc

    pending = None
    for w, g in units:
        inc = increment(w, g)
        if pending is not None:
            update(*pending)
        pending = (w, g, inc)
    update(*pending)


def _ssd_kernel(*refs, n_chunks, inner):
    ins, (dbf_r, dbb_r, alf_r, alb_r) = refs[:8 * inner], refs[8 * inner:8 * inner + 4]
    (yf_ref, yb_ref), states = refs[8 * inner + 4:8 * inner + 6], refs[8 * inner + 6:]

    @pl.when(pl.program_id(1) == 0)
    def _():
        for st in states:
            st[...] = jnp.zeros_like(st)

    fwd = [_SsdDir(*ins[8 * e:8 * e + 4], dbf_r, alf_r, yf_ref.at[e], states[2 * e], False, S_DTF, CHUNK - 1)
           for e in range(inner)]
    bwd = [_SsdDir(*ins[8 * e + 4:8 * e + 8], dbb_r, alb_r, yb_ref.at[e], states[2 * e + 1], True, S_DTB, 0)
           for e in range(inner)]

    def body(i, carry):
        c_f = pl.multiple_of(i * CHUNK, CHUNK)
        c_b = pl.multiple_of((n_chunks - 1 - i) * CHUNK, CHUNK)
        _ssd_chunks([(d, c_f) for d in fwd] + [(d, c_b) for d in bwd])
        return carry

    lax.fori_loop(0, n_chunks, body, 0)


def _ssd(xs_c, bm_c, cm_c, small, dtb_f, dtb_b, alog_f, alog_b, bsz, seq, lb):
    t = bsz * seq
    nb = seq // lb
    inner = SSD_INNER if bsz % SSD_INNER == 0 else 1

    def specs(e, reverse):
        idx = lambda p, n: (p * inner + e) * nb + (nb - 1 - n if reverse else n)
        return [pl.BlockSpec((lb, w), lambda p, n: (idx(p, n), 0)) for w in (SSD_DINNER, SSD_BC, SSD_BC, N_SMALL)]

    row = pl.BlockSpec((1, SSD_HEADS), lambda p, n: (0, 0))
    args = (xs_c, bm_c, cm_c, small)
    y_f, y_b = pl.pallas_call(
        functools.partial(_ssd_kernel, n_chunks=lb // CHUNK, inner=inner),
        grid=(bsz // inner, nb),
        in_specs=[s for e in range(inner) for rev in (False, True) for s in specs(e, rev)] + [row] * 4,
        out_specs=[pl.BlockSpec((None, inner, lb, SSD_DINNER), lambda p, n: (p, 0, n, 0)),
                   pl.BlockSpec((None, inner, lb, SSD_DINNER), lambda p, n: (p, 0, nb - 1 - n, 0))],
        out_shape=[jax.ShapeDtypeStruct((bsz // inner, inner, seq, SSD_DINNER), BF16)] * 2,
        scratch_shapes=[pltpu.VMEM((SSD_GROUPS, SSD_STATE, SSD_GW), F32)] * (2 * inner),
        compiler_params=pltpu.CompilerParams(
            dimension_semantics=("arbitrary", "arbitrary"), vmem_limit_bytes=VMEM_LIMIT),
        name="ssd_scan",
    )(*(args * (2 * inner)), dtb_f[None, :], dtb_b[None, :], alog_f[None, :], alog_b[None, :])
    return y_f.reshape(t, SSD_DINNER), y_b.reshape(t, SSD_DINNER)


def _merge_kernel(x_ref, of_ref, ob_ref, r_ref, gg_ref, yf_ref, yb_ref, xs_ref, z_ref, gs_ref,
                  gng_ref, dsk_ref, sng_ref, wug_ref, wus_ref, wo_ref, nfg_ref, wr_ref, br_ref,
                  h_ref, xp_ref, route_ref, gate_ref, cnt_out_ref, cnt_ref):
    @pl.when(pl.program_id(0) == 0)
    def _():
        cnt_ref[...] = jnp.zeros_like(cnt_ref)

    tm = x_ref.shape[0]
    subs = [slice(s * (tm // MERGE_SUBTILES), (s + 1) * (tm // MERGE_SUBTILES)) for s in range(MERGE_SUBTILES)]

    def gla_branch(rows):
        o = of_ref[rows, :].astype(F32) + ob_ref[rows, :].astype(F32)
        gng = gng_ref[...]
        o_parts = []
        for h in range(GLA_HEADS):
            oh = o[:, h * GLA_DV:(h + 1) * GLA_DV]
            oh = oh * lax.rsqrt(jnp.mean(oh * oh, axis=-1, keepdims=True) + EPS)
            o_parts.append(oh * gng)
        return (jnp.concatenate(o_parts, axis=1) * _silu(r_ref[rows, :]).astype(F32)).astype(BF16)

    def ssd_branch(rows):
        y = (yf_ref[rows, :].astype(F32) + yb_ref[rows, :].astype(F32)
             + dsk_ref[...] * xs_ref[rows, :].astype(F32))
        y = y * _silu(z_ref[rows, :]).astype(F32)
        sng = sng_ref[...]
        y_parts = []
        for g in range(SSD_GROUPS):
            yg = y[:, g * SSD_GW:(g + 1) * SSD_GW]
            yg = yg * lax.rsqrt(jnp.mean(yg * yg, axis=-1, keepdims=True) + EPS)
            y_parts.append(yg * sng[:, g * SSD_GW:(g + 1) * SSD_GW])
        return jnp.concatenate(y_parts, axis=1).astype(BF16)

    up_g = [_dot(gla_branch(rows), wug_ref[...]) for rows in subs]
    up_s = [_dot(ssd_branch(rows), wus_ref[...]) for rows in subs]
    mix = [(_sigmoid(gg_ref[rows, :]).astype(F32) * ug + _sigmoid(gs_ref[rows, :]).astype(F32) * us).astype(BF16)
           for rows, ug, us in zip(subs, up_g, up_s)]
    hs = [x_ref[rows, :] + _dot(m, wo_ref[...]) for rows, m in zip(subs, mix)]
    logit_parts = []
    for rows, h in zip(subs, hs):
        h_ref[rows, :] = h
        hn = (h * lax.rsqrt(jnp.mean(h * h, axis=-1, keepdims=True) + EPS)) * nfg_ref[...]
        words = _pack_bf16_pair(hn[:, :D_MODEL // 2], hn[:, D_MODEL // 2:])
        xp_ref[0, rows, :] = words[:, :SC_W]
        xp_ref[1, rows, :] = words[:, SC_W:]
        hn_hi, hn_lo = _split_bf16(hn, 2)
        logit_parts.append(_dot(hn_hi, wr_ref[0]) + (_dot(hn_hi, wr_ref[1]) + _dot(hn_lo, wr_ref[0])))
    logits = jnp.concatenate(logit_parts, axis=0) + br_ref[...]
    lane = lax.broadcasted_iota(I32, (tm, N_EXPERTS), 1).astype(F32)
    work = logits
    idxs, vals = [], []
    for _ in range(TOP_K):
        m = jnp.max(work, axis=-1, keepdims=True)
        idx = jnp.min(jnp.where(work == m, lane, float(N_EXPERTS)), axis=-1, keepdims=True)
        idxs.append(idx)
        vals.append(m)
        work = jnp.where(lane == idx, -jnp.inf, work)
    exps = [jnp.exp(v - vals[0]) for v in vals]
    denom = exps[0] + exps[1] + exps[2] + exps[3]
    gates = [e / denom for e in exps]
    sel = jnp.zeros((tm, N_EXPERTS), F32)
    for idx in idxs:
        sel = sel + jnp.where(lane == idx, 1.0, 0.0)
    rr = lax.broadcasted_iota(I32, (tm, tm), 0)
    cc = lax.broadcasted_iota(I32, (tm, tm), 1)
    strict = jnp.where(rr > cc, 1.0, 0.0).astype(BF16)
    pos = _dot(strict, sel.astype(BF16)) + cnt_ref[0:1, :]
    ranks = [jnp.sum(jnp.where(lane == idx, pos, 0.0), axis=-1, keepdims=True).astype(I32) for idx in idxs]
    cnt_new = cnt_ref[0:1, :] + jnp.sum(sel, axis=0, keepdims=True)
    cnt_ref[...] = jnp.broadcast_to(cnt_new, cnt_ref.shape)
    cnt_out_ref[...] = jnp.broadcast_to(cnt_new, cnt_ref.shape).astype(I32)
    lane128 = lax.broadcasted_iota(I32, (tm, 128), 1)
    route = jnp.zeros((tm, 128), I32)
    gate_o = jnp.zeros((tm, 128), F32)
    for k in range(TOP_K):
        route = jnp.where(lane128 == k, idxs[k].astype(I32), route)
        route = jnp.where(lane128 == TOP_K + k, ranks[k], route)
        gate_o = jnp.where(lane128 == k, gates[k], gate_o)
    route_ref[...] = route
    gate_ref[...] = gate_o


def _merge_params(gla_norm_g, ssd_d, ssd_norm_g, w_up_gla, w_up_ssd, w_out, norm_ffn_g, w_router, b_router):
    d_skip = jnp.repeat(ssd_d, SSD_HEADDIM)[None, :]
    return [gla_norm_g[None, :], d_skip, ssd_norm_g[None, :], w_up_gla.astype(BF16), w_up_ssd.astype(BF16),
            w_out.astype(BF16), norm_ffn_g[None, :], jnp.stack(_split_bf16(w_router, 2)), b_router[None, :]]


def _merge(x2, row0, o_f, o_b, proj, y_f, y_b, xs_c, params, tm):
    t = o_f.shape[0]
    blk0 = row0 // tm
    rowblk = lambda w, col=0: pl.BlockSpec((tm, w), lambda i: (i, col))
    const = lambda a: pl.BlockSpec(a.shape, lambda i: (0,) * a.ndim)
    return pl.pallas_call(
        _merge_kernel,
        grid=(t // tm,),
        in_specs=[pl.BlockSpec((tm, D_MODEL), lambda i: (blk0 + i, 0)),
                  rowblk(GLA_V), rowblk(GLA_V), rowblk(GLA_V, C_R // GLA_V),
                  rowblk(D_MODEL, C_GG // D_MODEL), rowblk(SSD_DINNER), rowblk(SSD_DINNER), rowblk(SSD_DINNER),
                  rowblk(SSD_DINNER, C_Z // SSD_DINNER), rowblk(D_MODEL, C_GS // D_MODEL)]
        + [const(p) for p in params],
        out_specs=[rowblk(D_MODEL), pl.BlockSpec((2, tm, SC_W), lambda i: (0, i, 0)), rowblk(128), rowblk(128),
                   pl.BlockSpec((8, N_EXPERTS), lambda i: (0, 0))],
        out_shape=[jax.ShapeDtypeStruct((t, D_MODEL), F32), jax.ShapeDtypeStruct((2, t, SC_W), U32),
                   jax.ShapeDtypeStruct((t, 128), I32), jax.ShapeDtypeStruct((t, 128), F32),
                   jax.ShapeDtypeStruct((8, N_EXPERTS), I32)],
        scratch_shapes=[pltpu.VMEM((8, N_EXPERTS), F32)],
        compiler_params=pltpu.CompilerParams(dimension_semantics=("arbitrary",), vmem_limit_bytes=VMEM_LIMIT),
        name="merge_router",
    )(x2, o_f, o_b, proj, proj, y_f, y_b, xs_c, proj, proj, *params)


def _sc_mesh():
    return plsc.VectorSubcoreMesh(core_axis_name="c", subcore_axis_name="s")


def _sc_scatter_rows(x, idx, n_out):
    n, m = x.shape[0], idx.shape[1]
    n_win = n // SC_WIN

    @functools.partial(pl.kernel, out_type=jax.ShapeDtypeStruct((n_out, SC_W), x.dtype), mesh=_sc_mesh())
    def scatter(x_hbm, i_hbm, o_hbm):
        def body(x_vmem, i_vmem):
            pltpu.sync_copy(x_vmem, o_hbm.at[i_vmem.at[0]])

        pltpu.emit_pipeline(
            body, grid=(m // SC_WIN,),
            in_specs=[pl.BlockSpec((SC_WIN, SC_W), lambda i: (i % n_win, 0)),
                      pl.BlockSpec((1, SC_WIN), lambda i: (0, i))],
            out_specs=[], core_axis_name=("c", "s"), dimension_semantics=(pltpu.PARALLEL,),
        )(x_hbm, i_hbm)

    return scatter(x, idx)


def _sc_gather_rows(table, idx):
    m = idx.shape[1]

    @functools.partial(pl.kernel, out_type=jax.ShapeDtypeStruct((m, SC_W), table.dtype), mesh=_sc_mesh())
    def gather(t_hbm, i_hbm, o_hbm):
        def body(i_vmem, o_vmem):
            pltpu.sync_copy(t_hbm.at[i_vmem.at[0]], o_vmem)

        pltpu.emit_pipeline(
            body, grid=(m // SC_WIN,),
            in_specs=[pl.BlockSpec((1, SC_WIN), lambda i: (0, i))],
            out_specs=[pl.BlockSpec((SC_WIN, SC_W), lambda i: (i, 0))],
            core_axis_name=("c", "s"), dimension_semantics=(pltpu.PARALLEL,),
        )(i_hbm, o_hbm)

    return gather(table, idx)


def _expert_kernel(blk_e_ref, nvalid_ref, nused_ref, x_ref, w1_ref, b1_ref, w2_ref, b2_ref, o_ref, w1_s, w2_s):
    del nused_ref
    i = pl.program_id(0)
    nvalid = nvalid_ref[i]

    @pl.when((nvalid > 0) & ((i == 0) | (blk_e_ref[i] != blk_e_ref[jnp.maximum(i - 1, 0)])))
    def _():
        w1_s[...] = w1_ref[0].astype(BF16)
        w2_s[...] = w2_ref[0].astype(BF16)

    @pl.when(nvalid > 0)
    def _():
        live = lax.broadcasted_iota(I32, (MOE_BM, SC_W), 0) < nvalid
        lo0, hi0 = _unpack_bf16_pair(jnp.where(live, x_ref[0], jnp.uint32(0)))
        lo1, hi1 = _unpack_bf16_pair(jnp.where(live, x_ref[1], jnp.uint32(0)))
        x = jnp.concatenate([lo0, lo1, hi0, hi1], axis=1).astype(BF16)
        hdn = _dot(x, w1_s[...]) + b1_ref[0]
        gate = jnp.minimum(hdn[:, :D_FF], SWIGLU_LIMIT)
        lin = jnp.clip(hdn[:, D_FF:], -SWIGLU_LIMIT, SWIGLU_LIMIT)
        act = gate * _sigmoid(SWIGLU_ALPHA * gate) * (lin + 1.0)
        y = _dot(act.astype(BF16), w2_s[...]) + b2_ref[0]
        words = _pack_bf16_pair(y[:, :D_MODEL // 2], y[:, D_MODEL // 2:])
        o_ref[0] = words[:, :SC_W]
        o_ref[1] = words[:, SC_W:]

    @pl.when(nvalid == 0)
    def _():
        o_ref[...] = jnp.zeros_like(o_ref)


def _experts(xb, blk_e, nvalid, n_used, w1, b1, w2, b2):
    n_rows = xb.shape[1]
    n_blocks = n_rows // MOE_BM
    xidx = lambda i, be, nv, nu: (0, jnp.minimum(i, nu[0] - 1), 0)
    eidx = lambda i, be, nv, nu: (be[i], 0, 0)
    return pl.pallas_call(
        _expert_kernel,
        grid_spec=pltpu.PrefetchScalarGridSpec(
            num_scalar_prefetch=3,
            grid=(n_blocks,),
            in_specs=[pl.BlockSpec((2, MOE_BM, SC_W), xidx),
                      pl.BlockSpec((1, D_MODEL, 2 * D_FF), eidx),
                      pl.BlockSpec((1, 1, 2 * D_FF), eidx),
                      pl.BlockSpec((1, D_FF, D_MODEL), eidx),
                      pl.BlockSpec((1, 1, D_MODEL), eidx)],
            out_specs=pl.BlockSpec((2, MOE_BM, SC_W), lambda i, be, nv, nu: (0, i, 0)),
            scratch_shapes=[pltpu.VMEM((D_MODEL, 2 * D_FF), BF16), pltpu.VMEM((D_FF, D_MODEL), BF16)],
        ),
        out_shape=jax.ShapeDtypeStruct((2, n_rows, SC_W), U32),
        compiler_params=pltpu.CompilerParams(dimension_semantics=("arbitrary",), vmem_limit_bytes=VMEM_LIMIT),
        name="moe_experts",
    )(blk_e, nvalid, n_used, xb, w1, b1, w2, b2)


def _combine_kernel(h_ref, gate_ref, y_ref, g_ref, *rest):
    o_ref = rest[-1]
    gates = gate_ref[...]
    acc = None
    for k in range(TOP_K):
        lo0, hi0 = _unpack_bf16_pair(y_ref[k, 0])
        lo1, hi1 = _unpack_bf16_pair(y_ref[k, 1])
        term = gates[:, k:k + 1] * jnp.concatenate([lo0, lo1, hi0, hi1], axis=1)
        acc = term if acc is None else acc + term
    h = h_ref[...] + acc
    o_ref[...] = (h * lax.rsqrt(jnp.mean(h * h, axis=-1, keepdims=True) + EPS)) * g_ref[...]


def _combine(h, gates, y_rows, norm_final_g, t_total, row0, prev, tm):
    t = h.shape[0]
    blk0 = row0 // tm
    in_specs = [pl.BlockSpec((tm, D_MODEL), lambda i: (i, 0)),
                pl.BlockSpec((tm, 128), lambda i: (i, 0)),
                pl.BlockSpec((TOP_K, 2, tm, SC_W), lambda i: (0, 0, i, 0)),
                pl.BlockSpec((1, D_MODEL), lambda i: (0, 0))]
    args = [h, gates, y_rows, norm_final_g[None, :]]
    if prev is not None:
        in_specs.append(pl.BlockSpec(memory_space=pl.ANY))
        args.append(prev)
    return pl.pallas_call(
        _combine_kernel,
        grid=(t // tm,),
        in_specs=in_specs,
        out_specs=pl.BlockSpec((tm, D_MODEL), lambda i: (blk0 + i, 0)),
        out_shape=jax.ShapeDtypeStruct((t_total, D_MODEL), F32),
        input_output_aliases={} if prev is None else {4: 0},
        compiler_params=pltpu.CompilerParams(dimension_semantics=("arbitrary",), vmem_limit_bytes=VMEM_LIMIT),
        name="moe_combine",
    )(*args)


def _pick(n, pref):
    b = min(pref, n)
    while n % b:
        b -= CHUNK
    return b


def _prep_w_in(w_in):
    widths = (GLA_QK, GLA_QK, GLA_V, GLA_V, GLA_RANK, GLA_RANK, SSD_DINNER, SSD_DINNER, SSD_BC, SSD_BC,
              2 * SSD_HEADS, D_MODEL, D_MODEL)
    pts, acc = [], 0
    for w in widths[:-1]:
        acc += w
        pts.append(acc)
    q, k, v, r, lrf, lrb, z, xs, bm, cm, dtr, gg, gs = jnp.split(w_in, pts, axis=1)
    main = jnp.concatenate([z, xs, q, k, v, r, gg, gs, bm, cm], axis=1).astype(BF16)
    pad = jnp.zeros((D_MODEL, N_SMALL - 2 * GLA_RANK - 2 * SSD_HEADS), w_in.dtype)
    small = jnp.concatenate([lrf, lrb, dtr, pad], axis=1).astype(BF16)
    return main, small


def _layer(h, norm_mix_g, w_in, gla_fw2_f, gla_fb_f, gla_fw2_b, gla_fb_b, gla_norm_g, conv_w, conv_b,
           dt_bias_f, dt_bias_b, a_log_f, a_log_b, ssd_d, ssd_norm_g, w_up_gla, w_up_ssd, w_out,
           norm_ffn_g, w_router, b_router, w1, b1, w2, b2, out_norm_g):
    bsz, seq, _ = h.shape
    t_total = bsz * seq
    x2 = h.reshape(t_total, D_MODEL)
    w_main, w_small = _prep_w_in(w_in)
    fw2f, fw2b = gla_fw2_f.astype(BF16), gla_fw2_b.astype(BF16)
    mparams = _merge_params(gla_norm_g, ssd_d, ssd_norm_g, w_up_gla, w_up_ssd, w_out, norm_ffn_g, w_router, b_router)
    n_groups = TOKEN_GROUPS if bsz % TOKEN_GROUPS == 0 else 1
    gb = bsz // n_groups
    t = gb * seq
    lb = _pick(seq, 512)
    out = None
    for grp in range(n_groups):
        row0 = grp * t
        proj, small = _inproj(x2, norm_mix_g[None, :], w_main, w_small, row0, t, tm=_pick(t, 2048), tn=1024)
        o_f, o_b = _gla(proj, small, fw2f, gla_fb_f[None, :], fw2b, gla_fb_b[None, :], gb, seq, lb)
        xs_c, bm_c, cm_c = _conv(proj, conv_w, conv_b, seq, _pick(seq, 512))
        y_f, y_b = _ssd(xs_c, bm_c, cm_c, small, dt_bias_f, dt_bias_b, a_log_f, a_log_b, gb, seq, lb)
        hres, xp, route, gates, counts8 = _merge(x2, row0, o_f, o_b, proj, y_f, y_b, xs_c, mparams, tm=_pick(t, 256))
        counts = counts8[0]
        padded = (counts + MOE_BM - 1) // MOE_BM * MOE_BM
        pend = jnp.cumsum(padded)
        pstart = (pend - padded).astype(I32)
        n_rows = t * TOP_K + N_EXPERTS * MOE_BM
        blk_row = jnp.arange(n_rows // MOE_BM, dtype=I32) * MOE_BM
        blk_e = jnp.minimum(jnp.sum(pend[None, :] <= blk_row[:, None], axis=1), N_EXPERTS - 1).astype(I32)
        nvalid = jnp.clip(pstart[blk_e] + counts[blk_e] - blk_row, 0, MOE_BM).astype(I32)
        n_used = (pend[-1:] // MOE_BM).astype(I32)
        top_e, rank = route[:, :TOP_K], route[:, TOP_K:2 * TOP_K]
        dest = jnp.sum(jnp.where(top_e[:, :, None] == jnp.arange(N_EXPERTS, dtype=I32), pstart, 0), axis=-1) + rank
        idx = (dest.T[:, None, :] + (jnp.arange(2, dtype=I32) * n_rows)[None, :, None]).reshape(1, 2 * TOP_K * t)
        xb = _sc_scatter_rows(xp.reshape(2 * t, SC_W), idx, 2 * n_rows).reshape(2, n_rows, SC_W)
        yb = _experts(xb, blk_e, nvalid, n_used, w1, b1[:, None, :], w2, b2[:, None, :])
        y_rows = _sc_gather_rows(yb.reshape(2 * n_rows, SC_W), idx).reshape(TOP_K, 2, t, SC_W)
        out = _combine(hres, gates, y_rows, out_norm_g, t_total, row0, out, tm=_pick(t, 256))
    return out.reshape(bsz, seq, D_MODEL)


def kernel(x, norm_mix_g, w_in, gla_fw2_f, gla_fb_f, gla_fw2_b, gla_fb_b, gla_norm_g, conv_w, conv_b, dt_bias_f,
           dt_bias_b, a_log_f, a_log_b, ssd_d, ssd_norm_g, w_up_gla, w_up_ssd, w_out, norm_ffn_g, w_router,
           b_router, w1, b1, w2, b2, norm_final_g):
    assert x.shape[-1] == D_MODEL and norm_mix_g.shape[0] == 1
    return _layer(x, norm_mix_g[0], w_in[0], gla_fw2_f[0], gla_fb_f[0], gla_fw2_b[0], gla_fb_b[0], gla_norm_g[0],
                  conv_w[0], conv_b[0], dt_bias_f[0], dt_bias_b[0], a_log_f[0], a_log_b[0], ssd_d[0],
                  ssd_norm_g[0], w_up_gla[0], w_up_ssd[0], w_out[0], norm_ffn_g[0], w_router[0], b_router[0],
                  w1[0], b1[0], w2[0], b2[0], norm_final_g)
```

```python
import functools

import jax
import jax.numpy as jnp
from jax import lax
from jax.experimental import pallas as pl
from jax.experimental.pallas import tpu as pltpu
from jax.experimental.pallas import tpu_sc as plsc

F32 = jnp.float32
BF16 = jnp.bfloat16
I32 = jnp.int32
U32 = jnp.uint32

D_MODEL = 1024
EPS = 1e-5
GLA_HEADS = 4
GLA_DK = 128
GLA_DV = 256
GLA_RANK = 16
GLA_TAU = 16.0
GLA_QK = GLA_HEADS * GLA_DK
GLA_V = GLA_HEADS * GLA_DV
SSD_DINNER = 2048
SSD_HEADDIM = 64
SSD_HEADS = 32
SSD_GROUPS = 4
SSD_HPG = 8
SSD_STATE = 128
SSD_CONV = 4
SSD_BC = SSD_GROUPS * SSD_STATE
SSD_GW = SSD_HPG * SSD_HEADDIM
N_EXPERTS = 32
TOP_K = 4
D_FF = 1024
SWIGLU_LIMIT = 7.0
SWIGLU_ALPHA = 1.702
CHUNK = 64
TOKEN_GROUPS = 2
MERGE_SUBTILES = 2
SSD_INNER = 2
MXU_LAG = 6

C_Z, C_XS, C_Q, C_K, C_V, C_R, C_GG, C_GS, C_B, C_C = 0, 2048, 4096, 4608, 5120, 6144, 7168, 8192, 9216, 9728
N_MAIN = 10240
N_SMALL = 128
S_LRF, S_LRB, S_DTF, S_DTB = 0, 16, 32, 64

VMEM_LIMIT = 56 * 1024 * 1024
MOE_BM = 512
SC_WIN = 128
SC_W = D_MODEL // 4


def _dot(a, b):
    return jnp.dot(a, b, preferred_element_type=F32)


def _dot_nt(a, b):
    return lax.dot_general(a, b, (((1,), (1,)), ((), ())), preferred_element_type=F32)


def _dot_tn(a, b):
    return lax.dot_general(a, b, (((0,), (0,)), ((), ())), preferred_element_type=F32)


def _split_bf16(x, n):
    parts = []
    r = x
    for _ in range(n):
        p = r.astype(BF16)
        parts.append(p)
        r = r - p.astype(F32)
    return parts


def _dot_exact_lhs(m_bf16, x, n):
    acc = None
    for p in _split_bf16(x, n):
        t = _dot(m_bf16, p)
        acc = t if acc is None else acc + t
    return acc


def _dot_exact_rhs(x, m_bf16, n):
    acc = None
    for p in _split_bf16(x, n):
        t = _dot(p, m_bf16)
        acc = t if acc is None else acc + t
    return acc


def _sigmoid(x):
    return 1.0 / (1.0 + jnp.exp2(x * (-1.4426950408889634)))


def _silu(x):
    return x * _sigmoid(x)


def _pack_bf16_pair(lo, hi):
    lo_b = lax.bitcast_convert_type(lo.astype(BF16).astype(F32), U32)
    hi_b = lax.bitcast_convert_type(hi.astype(BF16).astype(F32), U32)
    return (lo_b >> 16) | (hi_b & jnp.uint32(0xFFFF0000))


def _unpack_bf16_pair(u):
    lo = lax.bitcast_convert_type(u << 16, F32)
    hi = lax.bitcast_convert_type(u & jnp.uint32(0xFFFF0000), F32)
    return lo, hi


def _inproj_kernel(x_ref, g_ref, w_ref, ws_ref, o_ref, os_ref, xn_ref):
    @pl.when(pl.program_id(1) == 0)
    def _():
        x = x_ref[...]
        ms = jnp.mean(x * x, axis=-1, keepdims=True)
        xb = ((x * lax.rsqrt(ms + EPS)) * g_ref[...]).astype(BF16)
        xn_ref[...] = xb
        os_ref[...] = _dot(xb, ws_ref[...])

    o_ref[...] = _dot(xn_ref[...], w_ref[...]).astype(BF16)


def _inproj(x2, g, w_main, w_small, row0, t, tm, tn):
    blk0 = row0 // tm
    return pl.pallas_call(
        _inproj_kernel,
        grid=(t // tm, N_MAIN // tn),
        in_specs=[
            pl.BlockSpec((tm, D_MODEL), lambda i, j: (blk0 + i, 0)),
            pl.BlockSpec((1, D_MODEL), lambda i, j: (0, 0)),
            pl.BlockSpec((D_MODEL, tn), lambda i, j: (0, j)),
            pl.BlockSpec((D_MODEL, N_SMALL), lambda i, j: (0, 0)),
        ],
        out_specs=[
            pl.BlockSpec((tm, tn), lambda i, j: (i, j)),
            pl.BlockSpec((tm, N_SMALL), lambda i, j: (i, 0)),
        ],
        out_shape=[
            jax.ShapeDtypeStruct((t, N_MAIN), BF16),
            jax.ShapeDtypeStruct((t, N_SMALL), F32),
        ],
        scratch_shapes=[pltpu.VMEM((tm, D_MODEL), BF16)],
        compiler_params=pltpu.CompilerParams(
            dimension_semantics=("arbitrary", "arbitrary"), vmem_limit_bytes=VMEM_LIMIT),
        name="inproj",
    )(x2, g, w_main, w_small)


def _tri_masks(n):
    r = lax.broadcasted_iota(I32, (n, n), 0)
    c = lax.broadcasted_iota(I32, (n, n), 1)
    return r >= c, c >= r


class _GlaDir:
    def __init__(self, ins, outs, scratch, mask, lr_off, mid_row, last_row):
        self.q, self.k, self.v, self.sm, self.fw2, self.fb = ins
        self.o, self.st = outs
        self.b_s, self.qs_s, self.ks_s, self.kd_s, self.qe_s, self.p_s, self.u_s, self.el_s = scratch
        self.mask, self.lr_off, self.mid_row, self.last_row = mask, lr_off, mid_row, last_row


def _gla_decay(d, n_chunks):
    tri = jnp.where(d.mask, 1.0, 0.0).astype(BF16)
    lr = d.sm[:, d.lr_off:d.lr_off + GLA_RANK].astype(BF16)
    xg = _dot(lr, d.fw2[...]) + d.fb[...]
    d.b_s[...] = (jnp.minimum(xg, 0.0) - jnp.log(1.0 + jnp.exp(-jnp.abs(xg)))) * (1.0 / GLA_TAU)
    for c in range(n_chunks):
        rows = slice(c * CHUNK, (c + 1) * CHUNK)
        d.b_s[rows, :] = _dot_exact_lhs(tri, d.b_s[rows, :], 2)


def _gla_scale(d, c):
    rows = slice(c * CHUNK, (c + 1) * CHUNK)
    b = d.b_s[rows, :]
    b_mid = b[d.mid_row:d.mid_row + 1, :]
    b_last = b[d.last_row:d.last_row + 1, :]
    q = d.q[rows, :].astype(F32) * (GLA_DK ** -0.5)
    k = d.k[rows, :].astype(F32)
    d.qs_s[rows, :] = (q * jnp.exp(b - b_mid)).astype(BF16)
    d.ks_s[rows, :] = (k * jnp.exp(b_mid - b)).astype(BF16)
    d.kd_s[rows, :] = (k * jnp.exp(b_last - b)).astype(BF16)
    d.qe_s[rows, :] = (q * jnp.exp(b)).astype(BF16)
    d.el_s[c:c + 1, :] = jnp.exp(b_last)


def _gla_local(units):
    def score(u):
        d, c, h = u
        rows, ks_ = slice(c * CHUNK, (c + 1) * CHUNK), slice(h * GLA_DK, (h + 1) * GLA_DK)
        return _dot_nt(d.qs_s[rows, ks_], d.ks_s[rows, ks_])

    def finish(u, s):
        d, c, h = u
        rows, ks_ = slice(c * CHUNK, (c + 1) * CHUNK), slice(h * GLA_DK, (h + 1) * GLA_DK)
        vs_ = slice(h * GLA_DV, (h + 1) * GLA_DV)
        d.p_s[c, h] = jnp.where(d.mask, s, 0.0).astype(BF16)
        d.u_s[c, h] = _dot_tn(d.kd_s[rows, ks_], d.v[rows, vs_])

    pending = []
    for u in units:
        pending.append((u, score(u)))
        if len(pending) > MXU_LAG:
            finish(*pending.pop(0))
    for item in pending:
        finish(*item)


def _gla_carry(d, c):
    rows = slice(c * CHUNK, (c + 1) * CHUNK)
    for h in range(GLA_HEADS):
        ks_ = slice(h * GLA_DK, (h + 1) * GLA_DK)
        vs_ = slice(h * GLA_DV, (h + 1) * GLA_DV)
        st = d.st[h]
        o = _dot(d.p_s[c, h], d.v[rows, vs_]) + _dot(d.qe_s[rows, ks_], st.astype(BF16))
        d.o[rows, vs_] = o.astype(d.o.dtype)
        e_col = jnp.transpose(jnp.broadcast_to(d.el_s[c:c + 1, ks_], (8, GLA_DK)))[:, 0:1]
        d.st[h] = st * e_col + d.u_s[c, h]


def _gla_kernel(*refs, n_chunks):
    ins_f, ins_b, (fw2f, fbf, fw2b, fbb), (of_ref, ob_ref, stf, stb) = refs[0:4], refs[4:8], refs[8:12], refs[12:16]
    scr_f, scr_b = refs[16:24], refs[24:32]

    @pl.when(pl.program_id(1) == 0)
    def _():
        stf[...] = jnp.zeros_like(stf)
        stb[...] = jnp.zeros_like(stb)

    lower, upper = _tri_masks(CHUNK)
    fwd = _GlaDir((*ins_f, fw2f, fbf), (of_ref, stf), scr_f, lower, S_LRF, CHUNK // 2, CHUNK - 1)
    bwd = _GlaDir((*ins_b, fw2b, fbb), (ob_ref, stb), scr_b, upper, S_LRB, CHUNK // 2 - 1, 0)
    _gla_decay(fwd, n_chunks)
    _gla_decay(bwd, n_chunks)
    for c in range(n_chunks):
        _gla_scale(fwd, c)
        _gla_scale(bwd, c)
    _gla_local([(d, c, h) for c in range(n_chunks) for d in (fwd, bwd) for h in range(GLA_HEADS)])
    for i in range(n_chunks):
        _gla_carry(fwd, i)
        _gla_carry(bwd, n_chunks - 1 - i)


def _gla(proj, small, fw2f, fbf, fw2b, fbb, bsz, seq, lb):
    t = bsz * seq
    nb = seq // lb

    def fwd(w, col):
        return pl.BlockSpec((lb, w), lambda b, n: (b * nb + n, col))

    def bwd(w, col):
        return pl.BlockSpec((lb, w), lambda b, n: (b * nb + nb - 1 - n, col))

    const = lambda shape: pl.BlockSpec(shape, lambda b, n: (0, 0))
    per_dir_scratch = ([pltpu.VMEM((lb, GLA_QK), F32)] + [pltpu.VMEM((lb, GLA_QK), BF16)] * 4
                       + [pltpu.VMEM((lb // CHUNK, GLA_HEADS, CHUNK, CHUNK), BF16),
                          pltpu.VMEM((lb // CHUNK, GLA_HEADS, GLA_DK, GLA_DV), F32),
                          pltpu.VMEM((lb // CHUNK, GLA_QK), F32)])
    return pl.pallas_call(
        functools.partial(_gla_kernel, n_chunks=lb // CHUNK),
        grid=(bsz, nb),
        in_specs=[
            fwd(GLA_QK, C_Q // GLA_QK), fwd(GLA_QK, C_K // GLA_QK), fwd(GLA_V, C_V // GLA_V), fwd(N_SMALL, 0),
            bwd(GLA_QK, C_Q // GLA_QK), bwd(GLA_QK, C_K // GLA_QK), bwd(GLA_V, C_V // GLA_V), bwd(N_SMALL, 0),
            const((GLA_RANK, GLA_QK)), const((1, GLA_QK)), const((GLA_RANK, GLA_QK)), const((1, GLA_QK)),
        ],
        out_specs=[fwd(GLA_V, 0), bwd(GLA_V, 0)],
        out_shape=[jax.ShapeDtypeStruct((t, GLA_V), BF16)] * 2,
        scratch_shapes=[pltpu.VMEM((GLA_HEADS, GLA_DK, GLA_DV), F32)] * 2 + per_dir_scratch * 2,
        compiler_params=pltpu.CompilerParams(
            dimension_semantics=("arbitrary", "arbitrary"), vmem_limit_bytes=VMEM_LIMIT),
        name="gla_scan",
    )(proj, proj, proj, small, proj, proj, proj, small, fw2f, fbf, fw2b, fbb)


HALO = 16


def _conv_taps(x, xm1, xp1, xp2, w_ref, b_ref):
    w = w_ref[...]
    return _silu(xm1 * w[0:1, :] + x * w[1:2, :] + xp1 * w[2:3, :] + xp2 * w[3:4, :] + b_ref[...])


def _conv_one(x_ref, p_ref, n_ref, w_ref, b_ref, o_ref, has_prev, has_next):
    rb = x_ref.shape[0]
    x = x_ref[...].astype(F32)
    y = _conv_taps(x, pltpu.roll(x, 1, 0), pltpu.roll(x, rb - 1, 0), pltpu.roll(x, rb - 2, 0), w_ref, b_ref)
    o_ref[...] = y.astype(o_ref.dtype)
    prev = jnp.where(has_prev, p_ref[HALO - 1:HALO, :].astype(F32), 0.0)
    nxt = jnp.where(has_next, n_ref[0:2, :].astype(F32), 0.0)
    row = lax.broadcasted_iota(I32, (HALO, x.shape[1]), 0)
    head = x_ref[0:2 * HALO, :].astype(F32)
    xm1 = jnp.where(row == 0, prev, pltpu.roll(head, 1, 0)[0:HALO])
    o_ref[0:HALO, :] = _conv_taps(head[0:HALO], xm1, pltpu.roll(head, 2 * HALO - 1, 0)[0:HALO],
                                  pltpu.roll(head, 2 * HALO - 2, 0)[0:HALO], w_ref, b_ref).astype(o_ref.dtype)
    tail = x_ref[rb - 2 * HALO:rb, :].astype(F32)
    xp1 = jnp.where(row == HALO - 1, nxt[0:1, :], pltpu.roll(tail, 2 * HALO - 1, 0)[HALO:])
    xp2 = jnp.where(row == HALO - 2, nxt[0:1, :],
                    jnp.where(row == HALO - 1, nxt[1:2, :], pltpu.roll(tail, 2 * HALO - 2, 0)[HALO:]))
    o_ref[rb - HALO:rb, :] = _conv_taps(tail[HALO:], pltpu.roll(tail, 1, 0)[HALO:], xp1, xp2,
                                        w_ref, b_ref).astype(o_ref.dtype)


def _conv_kernel(xs, xsp, xsn, bm, bmp, bmn, cm, cmp_, cmn, wx, bx, wb, bb, wc, bc, oxs, obm, ocm, *, rb, seq):
    t0 = pl.program_id(0) * rb
    has_prev = (t0 % seq) != 0
    has_next = ((t0 + rb) % seq) != 0
    _conv_one(xs, xsp, xsn, wx, bx, oxs, has_prev, has_next)
    _conv_one(bm, bmp, bmn, wb, bb, obm, has_prev, has_next)
    _conv_one(cm, cmp_, cmn, wc, bc, ocm, has_prev, has_next)


def _conv(proj, conv_w, conv_b, seq, rb):
    t = proj.shape[0]
    nh = t // HALO
    per = rb // HALO

    def trio(w, col):
        cb = col // w
        return [
            pl.BlockSpec((rb, w), lambda i: (i, cb)),
            pl.BlockSpec((HALO, w), lambda i: (jnp.maximum(i * per - 1, 0), cb)),
            pl.BlockSpec((HALO, w), lambda i: (jnp.minimum((i + 1) * per, nh - 1), cb)),
        ]

    def wspecs(w):
        return [pl.BlockSpec((SSD_CONV, w), lambda i: (0, 0)), pl.BlockSpec((1, w), lambda i: (0, 0))]

    wx, wb, wc = conv_w[:, :SSD_DINNER], conv_w[:, SSD_DINNER:SSD_DINNER + SSD_BC], conv_w[:, SSD_DINNER + SSD_BC:]
    bx, bb, bc = (conv_b[None, :SSD_DINNER], conv_b[None, SSD_DINNER:SSD_DINNER + SSD_BC],
                  conv_b[None, SSD_DINNER + SSD_BC:])
    return pl.pallas_call(
        functools.partial(_conv_kernel, rb=rb, seq=seq),
        grid=(t // rb,),
        in_specs=trio(SSD_DINNER, C_XS) + trio(SSD_BC, C_B) + trio(SSD_BC, C_C)
        + wspecs(SSD_DINNER) + wspecs(SSD_BC) + wspecs(SSD_BC),
        out_specs=[pl.BlockSpec((rb, SSD_DINNER), lambda i: (i, 0)),
                   pl.BlockSpec((rb, SSD_BC), lambda i: (i, 0)),
                   pl.BlockSpec((rb, SSD_BC), lambda i: (i, 0))],
        out_shape=[jax.ShapeDtypeStruct((t, SSD_DINNER), BF16),
                   jax.ShapeDtypeStruct((t, SSD_BC), BF16),
                   jax.ShapeDtypeStruct((t, SSD_BC), BF16)],
        compiler_params=pltpu.CompilerParams(dimension_semantics=("arbitrary",), vmem_limit_bytes=VMEM_LIMIT),
        name="ssd_conv",
    )(proj, proj, proj, proj, proj, proj, proj, proj, proj, wx, bx, wb, bb, wc, bc)


def _softplus(x):
    return jnp.maximum(x, 0.0) + jnp.log(1.0 + jnp.exp(-jnp.abs(x)))


class _SsdDir:
    def __init__(self, xs, bm, cm, sm, dtb_row, alog_row, y, st, reverse, dt_off, last_row):
        self.xs, self.bm, self.cm, self.sm, self.dtb_row, self.alog_row = xs, bm, cm, sm, dtb_row, alog_row
        self.y, self.st, self.reverse, self.dt_off, self.last_row = y, st, reverse, dt_off, last_row


def _ssd_chunks(work):
    hh = lax.broadcasted_iota(I32, (SSD_HEADS, SSD_GW), 0)
    cc = lax.broadcasted_iota(I32, (SSD_HEADS, SSD_GW), 1)
    lane = lax.broadcasted_iota(I32, (CHUNK, 2 * SSD_HEADDIM), 1)
    row2 = lax.broadcasted_iota(I32, (CHUNK, 2 * SSD_HEADDIM), 0)
    left = lane < SSD_HEADDIM
    col2 = jnp.where(left, lane, lane - SSD_HEADDIM)
    lower, upper = _tri_masks(CHUNK)

    pre = []
    for d, c0 in work:
        rows = pl.ds(c0, CHUNK)
        tri = jnp.where(upper if d.reverse else lower, 1.0, 0.0).astype(BF16)
        a_row = -jnp.exp(d.alog_row[...])
        dt = _softplus(d.sm[rows, d.dt_off:d.dt_off + SSD_HEADS] + d.dtb_row[...])
        pre.append((dt, _dot_exact_lhs(tri, dt * a_row, 3)))
    heads = []
    for (d, c0), (dt, cum) in zip(work, pre):
        total = cum[d.last_row:d.last_row + 1, :]
        to_end = jnp.exp(total - cum) * dt
        fac = jnp.concatenate([to_end, jnp.exp(cum)], axis=0).astype(BF16)
        e_tot = jnp.broadcast_to(jnp.exp(total), (8, SSD_HEADS))
        heads.append((cum, jnp.transpose(cum), jnp.transpose(dt), fac, e_tot))

    units = [(w, g) for w in range(len(work)) for g in range(SSD_GROUPS)]
    groups = {}
    for w, g in units:
        d, c0 = work[w]
        rows, ns = pl.ds(c0, CHUNK), slice(g * SSD_STATE, (g + 1) * SSD_STATE)
        fac, e_tot = heads[w][3], heads[w][4]
        expand = jnp.where(cc // SSD_HEADDIM + g * SSD_HPG == hh, 1.0, 0.0).astype(BF16)
        fac_x = _dot(fac, expand)
        et_x = _dot_exact_rhs(e_tot, expand, 2)[0:1]
        groups[w, g] = (fac_x, et_x, _dot_nt(d.cm[rows, ns], d.bm[rows, ns]))

    for w, g in units:
        d, c0 = work[w]
        rows, ns, cs = pl.ds(c0, CHUNK), slice(g * SSD_STATE, (g + 1) * SSD_STATE), slice(g * SSD_GW, (g + 1) * SSD_GW)
        cum, cum_t, dt_t = heads[w][0:3]
        fac_x, _, cb = groups[w, g]
        mask2 = (col2 >= row2) if d.reverse else (row2 >= col2)
        y_inter = _dot(d.cm[rows, ns], d.st[g].astype(BF16))
        cb2 = jnp.concatenate([cb, cb], axis=1)
        parts = []
        for p in range(SSD_HPG // 2):
            h0 = g * SSD_HPG + 2 * p
            ps = slice(h0 * SSD_HEADDIM, (h0 + 2) * SSD_HEADDIM)
            col = jnp.where(left, cum[:, h0:h0 + 1], cum[:, h0 + 1:h0 + 2])
            rowv = jnp.concatenate([cum_t[h0:h0 + 1, :], cum_t[h0 + 1:h0 + 2, :]], axis=1)
            dtv = jnp.concatenate([dt_t[h0:h0 + 1, :], dt_t[h0 + 1:h0 + 2, :]], axis=1)
            decay = jnp.exp(jnp.where(mask2, col - rowv, -jnp.inf))
            wgt = (cb2 * decay * dtv).astype(BF16)
            xp = d.xs[rows, ps]
            zero = jnp.zeros_like(xp)
            xbd = jnp.concatenate([jnp.where(left, xp, zero), jnp.where(left, zero, xp)], axis=0)
            parts.append(_dot(wgt, xbd))
        y = jnp.concatenate(parts, axis=1) + y_inter * fac_x[CHUNK:2 * CHUNK]
        d.y[rows, cs] = y.astype(d.y.dtype)

    def increment(w, g):
        d, c0 = work[w]
        rows, ns, cs = pl.ds(c0, CHUNK), slice(g * SSD_STATE, (g + 1) * SSD_STATE), slice(g * SSD_GW, (g + 1) * SSD_GW)
        return _dot_tn(d.bm[rows, ns], d.xs[rows, cs] * groups[w, g][0][0:CHUNK].astype(BF16))

    def update(w, g, inc):
        d = work[w][0]
        d.st[g] = d.st[g] * groups[w, g][1] + inc

    pending = None
    for w, g in units:
        inc = increment(w, g)
        if pending is not None:
            update(*pending)
        pending = (w, g, inc)
    update(*pending)


def _ssd_kernel(*refs, n_chunks, inner):
    ins, (dbf_r, dbb_r, alf_r, alb_r) = refs[:8 * inner], refs[8 * inner:8 * inner + 4]
    (yf_ref, yb_ref), states = refs[8 * inner + 4:8 * inner + 6], refs[8 * inner + 6:]

    @pl.when(pl.program_id(1) == 0)
    def _():
        for st in states:
            st[...] = jnp.zeros_like(st)

    fwd = [_SsdDir(*ins[8 * e:8 * e + 4], dbf_r, alf_r, yf_ref.at[e], states[2 * e], False, S_DTF, CHUNK - 1)
           for e in range(inner)]
    bwd = [_SsdDir(*ins[8 * e + 4:8 * e + 8], dbb_r, alb_r, yb_ref.at[e], states[2 * e + 1], True, S_DTB, 0)
           for e in range(inner)]

    def body(i, carry):
        c_f = pl.multiple_of(i * CHUNK, CHUNK)
        c_b = pl.multiple_of((n_chunks - 1 - i) * CHUNK, CHUNK)
        _ssd_chunks([(d, c_f) for d in fwd] + [(d, c_b) for d in bwd])
        return carry

    lax.fori_loop(0, n_chunks, body, 0)


def _ssd(xs_c, bm_c, cm_c, small, dtb_f, dtb_b, alog_f, alog_b, bsz, seq, lb):
    t = bsz * seq
    nb = seq // lb
    inner = SSD_INNER if bsz % SSD_INNER == 0 else 1

    def specs(e, reverse):
        idx = lambda p, n: (p * inner + e) * nb + (nb - 1 - n if reverse else n)
        return [pl.BlockSpec((lb, w), lambda p, n: (idx(p, n), 0)) for w in (SSD_DINNER, SSD_BC, SSD_BC, N_SMALL)]

    row = pl.BlockSpec((1, SSD_HEADS), lambda p, n: (0, 0))
    args = (xs_c, bm_c, cm_c, small)
    y_f, y_b = pl.pallas_call(
        functools.partial(_ssd_kernel, n_chunks=lb // CHUNK, inner=inner),
        grid=(bsz // inner, nb),
        in_specs=[s for e in range(inner) for rev in (False, True) for s in specs(e, rev)] + [row] * 4,
        out_specs=[pl.BlockSpec((None, inner, lb, SSD_DINNER), lambda p, n: (p, 0, n, 0)),
                   pl.BlockSpec((None, inner, lb, SSD_DINNER), lambda p, n: (p, 0, nb - 1 - n, 0))],
        out_shape=[jax.ShapeDtypeStruct((bsz // inner, inner, seq, SSD_DINNER), BF16)] * 2,
        scratch_shapes=[pltpu.VMEM((SSD_GROUPS, SSD_STATE, SSD_GW), F32)] * (2 * inner),
        compiler_params=pltpu.CompilerParams(
            dimension_semantics=("arbitrary", "arbitrary"), vmem_limit_bytes=VMEM_LIMIT),
        name="ssd_scan",
    )(*(args * (2 * inner)), dtb_f[None, :], dtb_b[None, :], alog_f[None, :], alog_b[None, :])
    return y_f.reshape(t, SSD_DINNER), y_b.reshape(t, SSD_DINNER)


def _merge_kernel(x_ref, of_ref, ob_ref, r_ref, gg_ref, yf_ref, yb_ref, xs_ref, z_ref, gs_ref,
                  gng_ref, dsk_ref, sng_ref, wug_ref, wus_ref, wo_ref, nfg_ref, wr_ref, br_ref,
                  h_ref, xp_ref, route_ref, gate_ref, cnt_out_ref, cnt_ref):
    @pl.when(pl.program_id(0) == 0)
    def _():
        cnt_ref[...] = jnp.zeros_like(cnt_ref)

    tm = x_ref.shape[0]
    subs = [slice(s * (tm // MERGE_SUBTILES), (s + 1) * (tm // MERGE_SUBTILES)) for s in range(MERGE_SUBTILES)]

    def gla_branch(rows):
        o = of_ref[rows, :].astype(F32) + ob_ref[rows, :].astype(F32)
        gng = gng_ref[...]
        o_parts = []
        for h in range(GLA_HEADS):
            oh = o[:, h * GLA_DV:(h + 1) * GLA_DV]
            oh = oh * lax.rsqrt(jnp.mean(oh * oh, axis=-1, keepdims=True) + EPS)
            o_parts.append(oh * gng)
        return (jnp.concatenate(o_parts, axis=1) * _silu(r_ref[rows, :]).astype(F32)).astype(BF16)

    def ssd_branch(rows):
        y = (yf_ref[rows, :].astype(F32) + yb_ref[rows, :].astype(F32)
             + dsk_ref[...] * xs_ref[rows, :].astype(F32))
        y = y * _silu(z_ref[rows, :]).astype(F32)
        sng = sng_ref[...]
        y_parts = []
        for g in range(SSD_GROUPS):
            yg = y[:, g * SSD_GW:(g + 1) * SSD_GW]
            yg = yg * lax.rsqrt(jnp.mean(yg * yg, axis=-1, keepdims=True) + EPS)
            y_parts.append(yg * sng[:, g * SSD_GW:(g + 1) * SSD_GW])
        return jnp.concatenate(y_parts, axis=1).astype(BF16)

    up_g = [_dot(gla_branch(rows), wug_ref[...]) for rows in subs]
    up_s = [_dot(ssd_branch(rows), wus_ref[...]) for rows in subs]
    mix = [(_sigmoid(gg_ref[rows, :]).astype(F32) * ug + _sigmoid(gs_ref[rows, :]).astype(F32) * us).astype(BF16)
           for rows, ug, us in zip(subs, up_g, up_s)]
    hs = [x_ref[rows, :] + _dot(m, wo_ref[...]) for rows, m in zip(subs, mix)]
    logit_parts = []
    for rows, h in zip(subs, hs):
        h_ref[rows, :] = h
        hn = (h * lax.rsqrt(jnp.mean(h * h, axis=-1, keepdims=True) + EPS)) * nfg_ref[...]
        words = _pack_bf16_pair(hn[:, :D_MODEL // 2], hn[:, D_MODEL // 2:])
        xp_ref[0, rows, :] = words[:, :SC_W]
        xp_ref[1, rows, :] = words[:, SC_W:]
        hn_hi, hn_lo = _split_bf16(hn, 2)
        logit_parts.append(_dot(hn_hi, wr_ref[0]) + (_dot(hn_hi, wr_ref[1]) + _dot(hn_lo, wr_ref[0])))
    logits = jnp.concatenate(logit_parts, axis=0) + br_ref[...]
    lane = lax.broadcasted_iota(I32, (tm, N_EXPERTS), 1).astype(F32)
    work = logits
    idxs, vals = [], []
    for _ in range(TOP_K):
        m = jnp.max(work, axis=-1, keepdims=True)
        idx = jnp.min(jnp.where(work == m, lane, float(N_EXPERTS)), axis=-1, keepdims=True)
        idxs.append(idx)
        vals.append(m)
        work = jnp.where(lane == idx, -jnp.inf, work)
    exps = [jnp.exp(v - vals[0]) for v in vals]
    denom = exps[0] + exps[1] + exps[2] + exps[3]
    gates = [e / denom for e in exps]
    sel = jnp.zeros((tm, N_EXPERTS), F32)
    for idx in idxs:
        sel = sel + jnp.where(lane == idx, 1.0, 0.0)
    rr = lax.broadcasted_iota(I32, (tm, tm), 0)
    cc = lax.broadcasted_iota(I32, (tm, tm), 1)
    strict = jnp.where(rr > cc, 1.0, 0.0).astype(BF16)
    pos = _dot(strict, sel.astype(BF16)) + cnt_ref[0:1, :]
    ranks = [jnp.sum(jnp.where(lane == idx, pos, 0.0), axis=-1, keepdims=True).astype(I32) for idx in idxs]
    cnt_new = cnt_ref[0:1, :] + jnp.sum(sel, axis=0, keepdims=True)
    cnt_ref[...] = jnp.broadcast_to(cnt_new, cnt_ref.shape)
    cnt_out_ref[...] = jnp.broadcast_to(cnt_new, cnt_ref.shape).astype(I32)
    lane128 = lax.broadcasted_iota(I32, (tm, 128), 1)
    route = jnp.zeros((tm, 128), I32)
    gate_o = jnp.zeros((tm, 128), F32)
    for k in range(TOP_K):
        route = jnp.where(lane128 == k, idxs[k].astype(I32), route)
        route = jnp.where(lane128 == TOP_K + k, ranks[k], route)
        gate_o = jnp.where(lane128 == k, gates[k], gate_o)
    route_ref[...] = route
    gate_ref[...] = gate_o


def _merge_params(gla_norm_g, ssd_d, ssd_norm_g, w_up_gla, w_up_ssd, w_out, norm_ffn_g, w_router, b_router):
    d_skip = jnp.repeat(ssd_d, SSD_HEADDIM)[None, :]
    return [gla_norm_g[None, :], d_skip, ssd_norm_g[None, :], w_up_gla.astype(BF16), w_up_ssd.astype(BF16),
            w_out.astype(BF16), norm_ffn_g[None, :], jnp.stack(_split_bf16(w_router, 2)), b_router[None, :]]


def _merge(x2, row0, o_f, o_b, proj, y_f, y_b, xs_c, params, tm):
    t = o_f.shape[0]
    blk0 = row0 // tm
    rowblk = lambda w, col=0: pl.BlockSpec((tm, w), lambda i: (i, col))
    const = lambda a: pl.BlockSpec(a.shape, lambda i: (0,) * a.ndim)
    return pl.pallas_call(
        _merge_kernel,
        grid=(t // tm,),
        in_specs=[pl.BlockSpec((tm, D_MODEL), lambda i: (blk0 + i, 0)),
                  rowblk(GLA_V), rowblk(GLA_V), rowblk(GLA_V, C_R // GLA_V),
                  rowblk(D_MODEL, C_GG // D_MODEL), rowblk(SSD_DINNER), rowblk(SSD_DINNER), rowblk(SSD_DINNER),
                  rowblk(SSD_DINNER, C_Z // SSD_DINNER), rowblk(D_MODEL, C_GS // D_MODEL)]
        + [const(p) for p in params],
        out_specs=[rowblk(D_MODEL), pl.BlockSpec((2, tm, SC_W), lambda i: (0, i, 0)), rowblk(128), rowblk(128),
                   pl.BlockSpec((8, N_EXPERTS), lambda i: (0, 0))],
        out_shape=[jax.ShapeDtypeStruct((t, D_MODEL), F32), jax.ShapeDtypeStruct((2, t, SC_W), U32),
                   jax.ShapeDtypeStruct((t, 128), I32), jax.ShapeDtypeStruct((t, 128), F32),
                   jax.ShapeDtypeStruct((8, N_EXPERTS), I32)],
        scratch_shapes=[pltpu.VMEM((8, N_EXPERTS), F32)],
        compiler_params=pltpu.CompilerParams(dimension_semantics=("arbitrary",), vmem_limit_bytes=VMEM_LIMIT),
        name="merge_router",
    )(x2, o_f, o_b, proj, proj, y_f, y_b, xs_c, proj, proj, *params)


def _sc_mesh():
    return plsc.VectorSubcoreMesh(core_axis_name="c", subcore_axis_name="s")


def _sc_scatter_rows(x, idx, n_out):
    n, m = x.shape[0], idx.shape[1]
    n_win = n // SC_WIN

    @functools.partial(pl.kernel, out_type=jax.ShapeDtypeStruct((n_out, SC_W), x.dtype), mesh=_sc_mesh())
    def scatter(x_hbm, i_hbm, o_hbm):
        def body(x_vmem, i_vmem):
            pltpu.sync_copy(x_vmem, o_hbm.at[i_vmem.at[0]])

        pltpu.emit_pipeline(
            body, grid=(m // SC_WIN,),
            in_specs=[pl.BlockSpec((SC_WIN, SC_W), lambda i: (i % n_win, 0)),
                      pl.BlockSpec((1, SC_WIN), lambda i: (0, i))],
            out_specs=[], core_axis_name=("c", "s"), dimension_semantics=(pltpu.PARALLEL,),
        )(x_hbm, i_hbm)

    return scatter(x, idx)


def _sc_gather_rows(table, idx):
    m = idx.shape[1]

    @functools.partial(pl.kernel, out_type=jax.ShapeDtypeStruct((m, SC_W), table.dtype), mesh=_sc_mesh())
    def gather(t_hbm, i_hbm, o_hbm):
        def body(i_vmem, o_vmem):
            pltpu.sync_copy(t_hbm.at[i_vmem.at[0]], o_vmem)

        pltpu.emit_pipeline(
            body, grid=(m // SC_WIN,),
            in_specs=[pl.BlockSpec((1, SC_WIN), lambda i: (0, i))],
            out_specs=[pl.BlockSpec((SC_WIN, SC_W), lambda i: (i, 0))],
            core_axis_name=("c", "s"), dimension_semantics=(pltpu.PARALLEL,),
        )(i_hbm, o_hbm)

    return gather(table, idx)


def _expert_kernel(blk_e_ref, nvalid_ref, nused_ref, x_ref, w1_ref, b1_ref, w2_ref, b2_ref, o_ref, w1_s, w2_s):
    del nused_ref
    i = pl.program_id(0)
    nvalid = nvalid_ref[i]

    @pl.when((nvalid > 0) & ((i == 0) | (blk_e_ref[i] != blk_e_ref[jnp.maximum(i - 1, 0)])))
    def _():
        w1_s[...] = w1_ref[0].astype(BF16)
        w2_s[...] = w2_ref[0].astype(BF16)

    @pl.when(nvalid > 0)
    def _():
        live = lax.broadcasted_iota(I32, (MOE_BM, SC_W), 0) < nvalid
        lo0, hi0 = _unpack_bf16_pair(jnp.where(live, x_ref[0], jnp.uint32(0)))
        lo1, hi1 = _unpack_bf16_pair(jnp.where(live, x_ref[1], jnp.uint32(0)))
        x = jnp.concatenate([lo0, lo1, hi0, hi1], axis=1).astype(BF16)
        hdn = _dot(x, w1_s[...]) + b1_ref[0]
        gate = jnp.minimum(hdn[:, :D_FF], SWIGLU_LIMIT)
        lin = jnp.clip(hdn[:, D_FF:], -SWIGLU_LIMIT, SWIGLU_LIMIT)
        act = gate * _sigmoid(SWIGLU_ALPHA * gate) * (lin + 1.0)
        y = _dot(act.astype(BF16), w2_s[...]) + b2_ref[0]
        words = _pack_bf16_pair(y[:, :D_MODEL // 2], y[:, D_MODEL // 2:])
        o_ref[0] = words[:, :SC_W]
        o_ref[1] = words[:, SC_W:]

    @pl.when(nvalid == 0)
    def _():
        o_ref[...] = jnp.zeros_like(o_ref)


def _experts(xb, blk_e, nvalid, n_used, w1, b1, w2, b2):
    n_rows = xb.shape[1]
    n_blocks = n_rows // MOE_BM
    xidx = lambda i, be, nv, nu: (0, jnp.minimum(i, nu[0] - 1), 0)
    eidx = lambda i, be, nv, nu: (be[i], 0, 0)
    return pl.pallas_call(
        _expert_kernel,
        grid_spec=pltpu.PrefetchScalarGridSpec(
            num_scalar_prefetch=3,
            grid=(n_blocks,),
            in_specs=[pl.BlockSpec((2, MOE_BM, SC_W), xidx),
                      pl.BlockSpec((1, D_MODEL, 2 * D_FF), eidx),
                      pl.BlockSpec((1, 1, 2 * D_FF), eidx),
                      pl.BlockSpec((1, D_FF, D_MODEL), eidx),
                      pl.BlockSpec((1, 1, D_MODEL), eidx)],
            out_specs=pl.BlockSpec((2, MOE_BM, SC_W), lambda i, be, nv, nu: (0, i, 0)),
            scratch_shapes=[pltpu.VMEM((D_MODEL, 2 * D_FF), BF16), pltpu.VMEM((D_FF, D_MODEL), BF16)],
        ),
        out_shape=jax.ShapeDtypeStruct((2, n_rows, SC_W), U32),
        compiler_params=pltpu.CompilerParams(dimension_semantics=("arbitrary",), vmem_limit_bytes=VMEM_LIMIT),
        name="moe_experts",
    )(blk_e, nvalid, n_used, xb, w1, b1, w2, b2)


def _combine_kernel(h_ref, gate_ref, y_ref, g_ref, *rest):
    o_ref = rest[-1]
    gates = gate_ref[...]
    acc = None
    for k in range(TOP_K):
        lo0, hi0 = _unpack_bf16_pair(y_ref[k, 0])
        lo1, hi1 = _unpack_bf16_pair(y_ref[k, 1])
        term = gates[:, k:k + 1] * jnp.concatenate([lo0, lo1, hi0, hi1], axis=1)
        acc = term if acc is None else acc + term
    h = h_ref[...] + acc
    o_ref[...] = (h * lax.rsqrt(jnp.mean(h * h, axis=-1, keepdims=True) + EPS)) * g_ref[...]


def _combine(h, gates, y_rows, norm_final_g, t_total, row0, prev, tm):
    t = h.shape[0]
    blk0 = row0 // tm
    in_specs = [pl.BlockSpec((tm, D_MODEL), lambda i: (i, 0)),
                pl.BlockSpec((tm, 128), lambda i: (i, 0)),
                pl.BlockSpec((TOP_K, 2, tm, SC_W), lambda i: (0, 0, i, 0)),
                pl.BlockSpec((1, D_MODEL), lambda i: (0, 0))]
    args = [h, gates, y_rows, norm_final_g[None, :]]
    if prev is not None:
        in_specs.append(pl.BlockSpec(memory_space=pl.ANY))
        args.append(prev)
    return pl.pallas_call(
        _combine_kernel,
        grid=(t // tm,),
        in_specs=in_specs,
        out_specs=pl.BlockSpec((tm, D_MODEL), lambda i: (blk0 + i, 0)),
        out_shape=jax.ShapeDtypeStruct((t_total, D_MODEL), F32),
        input_output_aliases={} if prev is None else {4: 0},
        compiler_params=pltpu.CompilerParams(dimension_semantics=("arbitrary",), vmem_limit_bytes=VMEM_LIMIT),
        name="moe_combine",
    )(*args)


def _pick(n, pref):
    b = min(pref, n)
    while n % b:
        b -= CHUNK
    return b


def _prep_w_in(w_in):
    widths = (GLA_QK, GLA_QK, GLA_V, GLA_V, GLA_RANK, GLA_RANK, SSD_DINNER, SSD_DINNER, SSD_BC, SSD_BC,
              2 * SSD_HEADS, D_MODEL, D_MODEL)
    pts, acc = [], 0
    for w in widths[:-1]:
        acc += w
        pts.append(acc)
    q, k, v, r, lrf, lrb, z, xs, bm, cm, dtr, gg, gs = jnp.split(w_in, pts, axis=1)
    main = jnp.concatenate([z, xs, q, k, v, r, gg, gs, bm, cm], axis=1).astype(BF16)
    pad = jnp.zeros((D_MODEL, N_SMALL - 2 * GLA_RANK - 2 * SSD_HEADS), w_in.dtype)
    small = jnp.concatenate([lrf, lrb, dtr, pad], axis=1).astype(BF16)
    return main, small


def _layer(h, norm_mix_g, w_in, gla_fw2_f, gla_fb_f, gla_fw2_b, gla_fb_b, gla_norm_g, conv_w, conv_b,
           dt_bias_f, dt_bias_b, a_log_f, a_log_b, ssd_d, ssd_norm_g, w_up_gla, w_up_ssd, w_out,
           norm_ffn_g, w_router, b_router, w1, b1, w2, b2, out_norm_g):
    bsz, seq, _ = h.shape
    t_total = bsz * seq
    x2 = h.reshape(t_total, D_MODEL)
    w_main, w_small = _prep_w_in(w_in)
    fw2f, fw2b = gla_fw2_f.astype(BF16), gla_fw2_b.astype(BF16)
    mparams = _merge_params(gla_norm_g, ssd_d, ssd_norm_g, w_up_gla, w_up_ssd, w_out, norm_ffn_g, w_router, b_router)
    n_groups = TOKEN_GROUPS if bsz % TOKEN_GROUPS == 0 else 1
    gb = bsz // n_groups
    t = gb * seq
    lb = _pick(seq, 512)
    out = None
    for grp in range(n_groups):
        row0 = grp * t
        proj, small = _inproj(x2, norm_mix_g[None, :], w_main, w_small, row0, t, tm=_pick(t, 2048), tn=1024)
        o_f, o_b = _gla(proj, small, fw2f, gla_fb_f[None, :], fw2b, gla_fb_b[None, :], gb, seq, lb)
        xs_c, bm_c, cm_c = _conv(proj, conv_w, conv_b, seq, _pick(seq, 512))
        y_f, y_b = _ssd(xs_c, bm_c, cm_c, small, dt_bias_f, dt_bias_b, a_log_f, a_log_b, gb, seq, lb)
        hres, xp, route, gates, counts8 = _merge(x2, row0, o_f, o_b, proj, y_f, y_b, xs_c, mparams, tm=_pick(t, 256))
        counts = counts8[0]
        padded = (counts + MOE_BM - 1) // MOE_BM * MOE_BM
        pend = jnp.cumsum(padded)
        pstart = (pend - padded).astype(I32)
        n_rows = t * TOP_K + N_EXPERTS * MOE_BM
        blk_row = jnp.arange(n_rows // MOE_BM, dtype=I32) * MOE_BM
        blk_e = jnp.minimum(jnp.sum(pend[None, :] <= blk_row[:, None], axis=1), N_EXPERTS - 1).astype(I32)
        nvalid = jnp.clip(pstart[blk_e] + counts[blk_e] - blk_row, 0, MOE_BM).astype(I32)
        n_used = (pend[-1:] // MOE_BM).astype(I32)
        top_e, rank = route[:, :TOP_K], route[:, TOP_K:2 * TOP_K]
        dest = jnp.sum(jnp.where(top_e[:, :, None] == jnp.arange(N_EXPERTS, dtype=I32), pstart, 0), axis=-1) + rank
        idx = (dest.T[:, None, :] + (jnp.arange(2, dtype=I32) * n_rows)[None, :, None]).reshape(1, 2 * TOP_K * t)
        xb = _sc_scatter_rows(xp.reshape(2 * t, SC_W), idx, 2 * n_rows).reshape(2, n_rows, SC_W)
        yb = _experts(xb, blk_e, nvalid, n_used, w1, b1[:, None, :], w2, b2[:, None, :])
        y_rows = _sc_gather_rows(yb.reshape(2 * n_rows, SC_W), idx).reshape(TOP_K, 2, t, SC_W)
        out = _combine(hres, gates, y_rows, out_norm_g, t_total, row0, out, tm=_pick(t, 512))
    return out.reshape(bsz, seq, D_MODEL)


def kernel(x, norm_mix_g, w_in, gla_fw2_f, gla_fb_f, gla_fw2_b, gla_fb_b, gla_norm_g, conv_w, conv_b, dt_bias_f,
           dt_bias_b, a_log_f, a_log_b, ssd_d, ssd_norm_g, w_up_gla, w_up_ssd, w_out, norm_ffn_g, w_router,
           b_router, w1, b1, w2, b2, norm_final_g):
    assert x.shape[-1] == D_MODEL and norm_mix_g.shape[0] == 1
    return _layer(x, norm_mix_g[0], w_in[0], gla_fw2_f[0], gla_fb_f[0], gla_fw2_b[0], gla_fb_b[0], gla_norm_g[0],
                  conv_w[0], conv_b[0], dt_bias_f[0], dt_bias_b[0], a_log_f[0], a_log_b[0], ssd_d[0],
                  ssd_norm_g[0], w_up_gla[0], w_up_ssd[0], w_out[0], norm_ffn_g[0], w_router[0], b_router[0],
                  w1[0], b1[0], w2[0], b2[0], norm_final_g)
```

```python
import functools

import jax
import jax.numpy as jnp
from jax import lax
from jax.experimental import pallas as pl
from jax.experimental.pallas import tpu as pltpu
from jax.experimental.pallas import tpu_sc as plsc

F32 = jnp.float32
BF16 = jnp.bfloat16
I32 = jnp.int32
U32 = jnp.uint32

D_MODEL = 1024
EPS = 1e-5
GLA_HEADS = 4
GLA_DK = 128
GLA_DV = 256
GLA_RANK = 16
GLA_TAU = 16.0
GLA_QK = GLA_HEADS * GLA_DK
GLA_V = GLA_HEADS * GLA_DV
SSD_DINNER = 2048
SSD_HEADDIM = 64
SSD_HEADS = 32
SSD_GROUPS = 4
SSD_HPG = 8
SSD_STATE = 128
SSD_CONV = 4
SSD_BC = SSD_GROUPS * SSD_STATE
SSD_GW = SSD_HPG * SSD_HEADDIM
N_EXPERTS = 32
TOP_K = 4
D_FF = 1024
SWIGLU_LIMIT = 7.0
SWIGLU_ALPHA = 1.702
CHUNK = 64
TOKEN_GROUPS = 2
MERGE_SUBTILES = 2
SSD_INNER = 2
MXU_LAG = 6

C_Z, C_XS, C_Q, C_K, C_V, C_R, C_GG, C_GS, C_B, C_C = 0, 2048, 4096, 4608, 5120, 6144, 7168, 8192, 9216, 9728
N_MAIN = 10240
N_SMALL = 128
S_LRF, S_LRB, S_DTF, S_DTB = 0, 16, 32, 64

VMEM_LIMIT = 56 * 1024 * 1024
MOE_BM = 512
SC_WIN = 128
SC_W = D_MODEL // 4


def _dot(a, b):
    return jnp.dot(a, b, preferred_element_type=F32)


def _dot_nt(a, b):
    return lax.dot_general(a, b, (((1,), (1,)), ((), ())), preferred_element_type=F32)


def _dot_tn(a, b):
    return lax.dot_general(a, b, (((0,), (0,)), ((), ())), preferred_element_type=F32)


def _split_bf16(x, n):
    parts = []
    r = x
    for _ in range(n):
        p = r.astype(BF16)
        parts.append(p)
        r = r - p.astype(F32)
    return parts


def _dot_exact_lhs(m_bf16, x, n):
    acc = None
    for p in _split_bf16(x, n):
        t = _dot(m_bf16, p)
        acc = t if acc is None else acc + t
    return acc


def _dot_exact_rhs(x, m_bf16, n):
    acc = None
    for p in _split_bf16(x, n):
        t = _dot(p, m_bf16)
        acc = t if acc is None else acc + t
    return acc


def _sigmoid(x):
    return 1.0 / (1.0 + jnp.exp2(x * (-1.4426950408889634)))


def _silu(x):
    return x * _sigmoid(x)


def _pack_bf16_pair(lo, hi):
    lo_b = lax.bitcast_convert_type(lo.astype(BF16).astype(F32), U32)
    hi_b = lax.bitcast_convert_type(hi.astype(BF16).astype(F32), U32)
    return (lo_b >> 16) | (hi_b & jnp.uint32(0xFFFF0000))


def _unpack_bf16_pair(u):
    lo = lax.bitcast_convert_type(u << 16, F32)
    hi = lax.bitcast_convert_type(u & jnp.uint32(0xFFFF0000), F32)
    return lo, hi


def _inproj_kernel(x_ref, g_ref, w_ref, ws_ref, o_ref, os_ref, xn_ref):
    @pl.when(pl.program_id(1) == 0)
    def _():
        x = x_ref[...]
        ms = jnp.mean(x * x, axis=-1, keepdims=True)
        xb = ((x * lax.rsqrt(ms + EPS)) * g_ref[...]).astype(BF16)
        xn_ref[...] = xb
        os_ref[...] = _dot(xb, ws_ref[...])

    o_ref[...] = _dot(xn_ref[...], w_ref[...]).astype(BF16)


def _inproj(x2, g, w_main, w_small, row0, t, tm, tn):
    blk0 = row0 // tm
    return pl.pallas_call(
        _inproj_kernel,
        grid=(t // tm, N_MAIN // tn),
        in_specs=[
            pl.BlockSpec((tm, D_MODEL), lambda i, j: (blk0 + i, 0)),
            pl.BlockSpec((1, D_MODEL), lambda i, j: (0, 0)),
            pl.BlockSpec((D_MODEL, tn), lambda i, j: (0, j)),
            pl.BlockSpec((D_MODEL, N_SMALL), lambda i, j: (0, 0)),
        ],
        out_specs=[
            pl.BlockSpec((tm, tn), lambda i, j: (i, j)),
            pl.BlockSpec((tm, N_SMALL), lambda i, j: (i, 0)),
        ],
        out_shape=[
            jax.ShapeDtypeStruct((t, N_MAIN), BF16),
            jax.ShapeDtypeStruct((t, N_SMALL), F32),
        ],
        scratch_shapes=[pltpu.VMEM((tm, D_MODEL), BF16)],
        compiler_params=pltpu.CompilerParams(
            dimension_semantics=("arbitrary", "arbitrary"), vmem_limit_bytes=VMEM_LIMIT),
        name="inproj",
    )(x2, g, w_main, w_small)


def _tri_masks(n):
    r = lax.broadcasted_iota(I32, (n, n), 0)
    c = lax.broadcasted_iota(I32, (n, n), 1)
    return r >= c, c >= r


class _GlaDir:
    def __init__(self, ins, outs, scratch, mask, lr_off, mid_row, last_row):
        self.q, self.k, self.v, self.sm, self.fw2, self.fb = ins
        self.o, self.st = outs
        self.b_s, self.qs_s, self.ks_s, self.kd_s, self.qe_s, self.p_s, self.u_s, self.el_s = scratch
        self.mask, self.lr_off, self.mid_row, self.last_row = mask, lr_off, mid_row, last_row


def _gla_decay(d, n_chunks):
    tri = jnp.where(d.mask, 1.0, 0.0).astype(BF16)
    lr = d.sm[:, d.lr_off:d.lr_off + GLA_RANK].astype(BF16)
    xg = _dot(lr, d.fw2[...]) + d.fb[...]
    d.b_s[...] = (jnp.minimum(xg, 0.0) - jnp.log(1.0 + jnp.exp(-jnp.abs(xg)))) * (1.0 / GLA_TAU)
    for c in range(n_chunks):
        rows = slice(c * CHUNK, (c + 1) * CHUNK)
        d.b_s[rows, :] = _dot_exact_lhs(tri, d.b_s[rows, :], 2)


def _gla_scale(d, c):
    rows = slice(c * CHUNK, (c + 1) * CHUNK)
    b = d.b_s[rows, :]
    b_mid = b[d.mid_row:d.mid_row + 1, :]
    b_last = b[d.last_row:d.last_row + 1, :]
    q = d.q[rows, :].astype(F32) * (GLA_DK ** -0.5)
    k = d.k[rows, :].astype(F32)
    d.qs_s[rows, :] = (q * jnp.exp(b - b_mid)).astype(BF16)
    d.ks_s[rows, :] = (k * jnp.exp(b_mid - b)).astype(BF16)
    d.kd_s[rows, :] = (k * jnp.exp(b_last - b)).astype(BF16)
    d.qe_s[rows, :] = (q * jnp.exp(b)).astype(BF16)
    d.el_s[c:c + 1, :] = jnp.exp(b_last)


def _gla_local(units):
    def score(u):
        d, c, h = u
        rows, ks_ = slice(c * CHUNK, (c + 1) * CHUNK), slice(h * GLA_DK, (h + 1) * GLA_DK)
        return _dot_nt(d.qs_s[rows, ks_], d.ks_s[rows, ks_])

    def finish(u, s):
        d, c, h = u
        rows, ks_ = slice(c * CHUNK, (c + 1) * CHUNK), slice(h * GLA_DK, (h + 1) * GLA_DK)
        vs_ = slice(h * GLA_DV, (h + 1) * GLA_DV)
        d.p_s[c, h] = jnp.where(d.mask, s, 0.0).astype(BF16)
        d.u_s[c, h] = _dot_tn(d.kd_s[rows, ks_], d.v[rows, vs_])

    pending = []
    for u in units:
        pending.append((u, score(u)))
        if len(pending) > MXU_LAG:
            finish(*pending.pop(0))
    for item in pending:
        finish(*item)


def _gla_carry(d, c):
    rows = slice(c * CHUNK, (c + 1) * CHUNK)
    for h in range(GLA_HEADS):
        ks_ = slice(h * GLA_DK, (h + 1) * GLA_DK)
        vs_ = slice(h * GLA_DV, (h + 1) * GLA_DV)
        st = d.st[h]
        o = _dot(d.p_s[c, h], d.v[rows, vs_]) + _dot(d.qe_s[rows, ks_], st.astype(BF16))
        d.o[rows, vs_] = o.astype(d.o.dtype)
        e_col = jnp.transpose(jnp.broadcast_to(d.el_s[c:c + 1, ks_], (8, GLA_DK)))[:, 0:1]
        d.st[h] = st * e_col + d.u_s[c, h]


def _gla_kernel(*refs, n_chunks):
    ins_f, ins_b, (fw2f, fbf, fw2b, fbb), (of_ref, ob_ref, stf, stb) = refs[0:4], refs[4:8], refs[8:12], refs[12:16]
    scr_f, scr_b = refs[16:24], refs[24:32]

    @pl.when(pl.program_id(1) == 0)
    def _():
        stf[...] = jnp.zeros_like(stf)
        stb[...] = jnp.zeros_like(stb)

    lower, upper = _tri_masks(CHUNK)
    fwd = _GlaDir((*ins_f, fw2f, fbf), (of_ref, stf), scr_f, lower, S_LRF, CHUNK // 2, CHUNK - 1)
    bwd = _GlaDir((*ins_b, fw2b, fbb), (ob_ref, stb), scr_b, upper, S_LRB, CHUNK // 2 - 1, 0)
    _gla_decay(fwd, n_chunks)
    _gla_decay(bwd, n_chunks)
    for c in range(n_chunks):
        _gla_scale(fwd, c)
        _gla_scale(bwd, c)
    _gla_local([(d, c, h) for c in range(n_chunks) for d in (fwd, bwd) for h in range(GLA_HEADS)])
    for i in range(n_chunks):
        _gla_carry(fwd, i)
        _gla_carry(bwd, n_chunks - 1 - i)


def _gla(proj, small, fw2f, fbf, fw2b, fbb, bsz, seq, lb):
    t = bsz * seq
    nb = seq // lb

    def fwd(w, col):
        return pl.BlockSpec((lb, w), lambda b, n: (b * nb + n, col))

    def bwd(w, col):
        return pl.BlockSpec((lb, w), lambda b, n: (b * nb + nb - 1 - n, col))

    const = lambda shape: pl.BlockSpec(shape, lambda b, n: (0, 0))
    per_dir_scratch = ([pltpu.VMEM((lb, GLA_QK), F32)] + [pltpu.VMEM((lb, GLA_QK), BF16)] * 4
                       + [pltpu.VMEM((lb // CHUNK, GLA_HEADS, CHUNK, CHUNK), BF16),
                          pltpu.VMEM((lb // CHUNK, GLA_HEADS, GLA_DK, GLA_DV), F32),
                          pltpu.VMEM((lb // CHUNK, GLA_QK), F32)])
    return pl.pallas_call(
        functools.partial(_gla_kernel, n_chunks=lb // CHUNK),
        grid=(bsz, nb),
        in_specs=[
            fwd(GLA_QK, C_Q // GLA_QK), fwd(GLA_QK, C_K // GLA_QK), fwd(GLA_V, C_V // GLA_V), fwd(N_SMALL, 0),
            bwd(GLA_QK, C_Q // GLA_QK), bwd(GLA_QK, C_K // GLA_QK), bwd(GLA_V, C_V // GLA_V), bwd(N_SMALL, 0),
            const((GLA_RANK, GLA_QK)), const((1, GLA_QK)), const((GLA_RANK, GLA_QK)), const((1, GLA_QK)),
        ],
        out_specs=[fwd(GLA_V, 0), bwd(GLA_V, 0)],
        out_shape=[jax.ShapeDtypeStruct((t, GLA_V), BF16)] * 2,
        scratch_shapes=[pltpu.VMEM((GLA_HEADS, GLA_DK, GLA_DV), F32)] * 2 + per_dir_scratch * 2,
        compiler_params=pltpu.CompilerParams(
            dimension_semantics=("arbitrary", "arbitrary"), vmem_limit_bytes=VMEM_LIMIT),
        name="gla_scan",
    )(proj, proj, proj, small, proj, proj, proj, small, fw2f, fbf, fw2b, fbb)


HALO = 16


CONV_SUB = 256
CONV_COLS = 512


def _conv_taps(x, xm1, xp1, xp2, w, b):
    return _silu(xm1 * w[0:1, :] + x * w[1:2, :] + xp1 * w[2:3, :] + xp2 * w[3:4, :] + b)


def _shift_matrix(n):
    r = lax.broadcasted_iota(I32, (3 * n, n), 0)
    c = lax.broadcasted_iota(I32, (3 * n, n), 1)
    src = jnp.where(r < n, r - 1, jnp.where(r < 2 * n, r - n + 1, r - 2 * n + 2))
    return jnp.where(c == src, 1.0, 0.0).astype(BF16)


def _conv_one(x_ref, p_ref, n_ref, w_ref, b_ref, o_ref, shift, has_prev, has_next):
    rb, wd = x_ref.shape
    w, b = w_ref[...], b_ref[...]
    for s in range(rb // CONV_SUB):
        rows = slice(s * CONV_SUB, (s + 1) * CONV_SUB)
        for c0 in range(0, wd, CONV_COLS):
            cols = slice(c0, c0 + CONV_COLS)
            xb = x_ref[rows, cols]
            sx = _dot(shift, xb)
            y = _conv_taps(xb.astype(F32), sx[0:CONV_SUB], sx[CONV_SUB:2 * CONV_SUB], sx[2 * CONV_SUB:],
                           w[:, cols], b[:, cols])
            o_ref[rows, cols] = y.astype(o_ref.dtype)
    prev = jnp.where(has_prev, p_ref[HALO - 1:HALO, :].astype(F32), 0.0)
    nxt = jnp.where(has_next, n_ref[0:2, :].astype(F32), 0.0)
    row = lax.broadcasted_iota(I32, (HALO, wd), 0)
    head = x_ref[0:2 * HALO, :].astype(F32)
    xm1 = jnp.where(row == 0, prev, pltpu.roll(head, 1, 0)[0:HALO])
    o_ref[0:HALO, :] = _conv_taps(head[0:HALO], xm1, pltpu.roll(head, 2 * HALO - 1, 0)[0:HALO],
                                  pltpu.roll(head, 2 * HALO - 2, 0)[0:HALO], w, b).astype(o_ref.dtype)
    tail = x_ref[rb - 2 * HALO:rb, :].astype(F32)
    xp1 = jnp.where(row == HALO - 1, nxt[0:1, :], pltpu.roll(tail, 2 * HALO - 1, 0)[HALO:])
    xp2 = jnp.where(row == HALO - 2, nxt[0:1, :],
                    jnp.where(row == HALO - 1, nxt[1:2, :], pltpu.roll(tail, 2 * HALO - 2, 0)[HALO:]))
    o_ref[rb - HALO:rb, :] = _conv_taps(tail[HALO:], pltpu.roll(tail, 1, 0)[HALO:], xp1, xp2, w, b).astype(o_ref.dtype)
    for s in range(1, rb // CONV_SUB):
        e = s * CONV_SUB
        win = x_ref[e - 2 * HALO:e + 2 * HALO, :].astype(F32)
        mid = slice(HALO, 3 * HALO)
        o_ref[e - HALO:e + HALO, :] = _conv_taps(
            win[mid], pltpu.roll(win, 1, 0)[mid], pltpu.roll(win, 4 * HALO - 1, 0)[mid],
            pltpu.roll(win, 4 * HALO - 2, 0)[mid], w, b).astype(o_ref.dtype)


def _conv_kernel(xs, xsp, xsn, bm, bmp, bmn, cm, cmp_, cmn, wx, bx, wb, bb, wc, bc, oxs, obm, ocm, *, rb, seq):
    t0 = pl.program_id(0) * rb
    has_prev = (t0 % seq) != 0
    has_next = ((t0 + rb) % seq) != 0
    shift = _shift_matrix(CONV_SUB)
    _conv_one(xs, xsp, xsn, wx, bx, oxs, shift, has_prev, has_next)
    _conv_one(bm, bmp, bmn, wb, bb, obm, shift, has_prev, has_next)
    _conv_one(cm, cmp_, cmn, wc, bc, ocm, shift, has_prev, has_next)


def _conv(proj, conv_w, conv_b, seq, rb):
    t = proj.shape[0]
    nh = t // HALO
    per = rb // HALO

    def trio(w, col):
        cb = col // w
        return [
            pl.BlockSpec((rb, w), lambda i: (i, cb)),
            pl.BlockSpec((HALO, w), lambda i: (jnp.maximum(i * per - 1, 0), cb)),
            pl.BlockSpec((HALO, w), lambda i: (jnp.minimum((i + 1) * per, nh - 1), cb)),
        ]

    def wspecs(w):
        return [pl.BlockSpec((SSD_CONV, w), lambda i: (0, 0)), pl.BlockSpec((1, w), lambda i: (0, 0))]

    wx, wb, wc = conv_w[:, :SSD_DINNER], conv_w[:, SSD_DINNER:SSD_DINNER + SSD_BC], conv_w[:, SSD_DINNER + SSD_BC:]
    bx, bb, bc = (conv_b[None, :SSD_DINNER], conv_b[None, SSD_DINNER:SSD_DINNER + SSD_BC],
                  conv_b[None, SSD_DINNER + SSD_BC:])
    return pl.pallas_call(
        functools.partial(_conv_kernel, rb=rb, seq=seq),
        grid=(t // rb,),
        in_specs=trio(SSD_DINNER, C_XS) + trio(SSD_BC, C_B) + trio(SSD_BC, C_C)
        + wspecs(SSD_DINNER) + wspecs(SSD_BC) + wspecs(SSD_BC),
        out_specs=[pl.BlockSpec((rb, SSD_DINNER), lambda i: (i, 0)),
                   pl.BlockSpec((rb, SSD_BC), lambda i: (i, 0)),
                   pl.BlockSpec((rb, SSD_BC), lambda i: (i, 0))],
        out_shape=[jax.ShapeDtypeStruct((t, SSD_DINNER), BF16),
                   jax.ShapeDtypeStruct((t, SSD_BC), BF16),
                   jax.ShapeDtypeStruct((t, SSD_BC), BF16)],
        compiler_params=pltpu.CompilerParams(dimension_semantics=("arbitrary",), vmem_limit_bytes=VMEM_LIMIT),
        name="ssd_conv",
    )(proj, proj, proj, proj, proj, proj, proj, proj, proj, wx, bx, wb, bb, wc, bc)


def _softplus(x):
    return jnp.maximum(x, 0.0) + jnp.log(1.0 + jnp.exp(-jnp.abs(x)))


class _SsdDir:
    def __init__(self, xs, bm, cm, sm, dtb_row, alog_row, y, st, reverse, dt_off, last_row):
        self.xs, self.bm, self.cm, self.sm, self.dtb_row, self.alog_row = xs, bm, cm, sm, dtb_row, alog_row
        self.y, self.st, self.reverse, self.dt_off, self.last_row = y, st, reverse, dt_off, last_row


def _ssd_chunks(work):
    hh = lax.broadcasted_iota(I32, (SSD_HEADS, SSD_GW), 0)
    cc = lax.broadcasted_iota(I32, (SSD_HEADS, SSD_GW), 1)
    lane = lax.broadcasted_iota(I32, (CHUNK, 2 * SSD_HEADDIM), 1)
    row2 = lax.broadcasted_iota(I32, (CHUNK, 2 * SSD_HEADDIM), 0)
    left = lane < SSD_HEADDIM
    col2 = jnp.where(left, lane, lane - SSD_HEADDIM)
    lower, upper = _tri_masks(CHUNK)

    pre = []
    for d, c0 in work:
        rows = pl.ds(c0, CHUNK)
        tri = jnp.where(upper if d.reverse else lower, 1.0, 0.0).astype(BF16)
        a_row = -jnp.exp(d.alog_row[...])
        dt = _softplus(d.sm[rows, d.dt_off:d.dt_off + SSD_HEADS] + d.dtb_row[...])
        pre.append((dt, _dot_exact_lhs(tri, dt * a_row, 3)))
    heads = []
    for (d, c0), (dt, cum) in zip(work, pre):
        total = cum[d.last_row:d.last_row + 1, :]
        to_end = jnp.exp(total - cum) * dt
        fac = jnp.concatenate([to_end, jnp.exp(cum)], axis=0).astype(BF16)
        e_tot = jnp.broadcast_to(jnp.exp(total), (8, SSD_HEADS))
        heads.append((cum, jnp.transpose(cum), jnp.transpose(dt), fac, e_tot))

    units = [(w, g) for w in range(len(work)) for g in range(SSD_GROUPS)]
    groups = {}
    for w, g in units:
        d, c0 = work[w]
        rows, ns = pl.ds(c0, CHUNK), slice(g * SSD_STATE, (g + 1) * SSD_STATE)
        fac, e_tot = heads[w][3], heads[w][4]
        expand = jnp.where(cc // SSD_HEADDIM + g * SSD_HPG == hh, 1.0, 0.0).astype(BF16)
        fac_x = _dot(fac, expand)
        et_x = _dot_exact_rhs(e_tot, expand, 2)[0:1]
        groups[w, g] = (fac_x, et_x, _dot_nt(d.cm[rows, ns], d.bm[rows, ns]))

    for w, g in units:
        d, c0 = work[w]
        rows, ns, cs = pl.ds(c0, CHUNK), slice(g * SSD_STATE, (g + 1) * SSD_STATE), slice(g * SSD_GW, (g + 1) * SSD_GW)
        cum, cum_t, dt_t = heads[w][0:3]
        fac_x, _, cb = groups[w, g]
        mask2 = (col2 >= row2) if d.reverse else (row2 >= col2)
        y_inter = _dot(d.cm[rows, ns], d.st[g].astype(BF16))
        cb2 = jnp.concatenate([cb, cb], axis=1)
        parts = []
        for p in range(SSD_HPG // 2):
            h0 = g * SSD_HPG + 2 * p
            ps = slice(h0 * SSD_HEADDIM, (h0 + 2) * SSD_HEADDIM)
            col = jnp.where(left, cum[:, h0:h0 + 1], cum[:, h0 + 1:h0 + 2])
            rowv = jnp.concatenate([cum_t[h0:h0 + 1, :], cum_t[h0 + 1:h0 + 2, :]], axis=1)
            dtv = jnp.concatenate([dt_t[h0:h0 + 1, :], dt_t[h0 + 1:h0 + 2, :]], axis=1)
            decay = jnp.exp(jnp.where(mask2, col - rowv, -jnp.inf))
            wgt = (cb2 * decay * dtv).astype(BF16)
            xp = d.xs[rows, ps]
            zero = jnp.zeros_like(xp)
            xbd = jnp.concatenate([jnp.where(left, xp, zero), jnp.where(left, zero, xp)], axis=0)
            parts.append(_dot(wgt, xbd))
        y = jnp.concatenate(parts, axis=1) + y_inter * fac_x[CHUNK:2 * CHUNK]
        d.y[rows, cs] = y.astype(d.y.dtype)

    def increment(w, g):
        d, c0 = work[w]
        rows, ns, cs = pl.ds(c0, CHUNK), slice(g * SSD_STATE, (g + 1) * SSD_STATE), slice(g * SSD_GW, (g + 1) * SSD_GW)
        return _dot_tn(d.bm[rows, ns], d.xs[rows, cs] * groups[w, g][0][0:CHUNK].astype(BF16))

    def update(w, g, inc):
        d = work[w][0]
        d.st[g] = d.st[g] * groups[w, g][1] + inc

    pending = None
    for w, g in units:
        inc = increment(w, g)
        if pending is not None:
            update(*pending)
        pending = (w, g, inc)
    update(*pending)


def _ssd_kernel(*refs, n_chunks, inner):
    ins, (dbf_r, dbb_r, alf_r, alb_r) = refs[:8 * inner], refs[8 * inner:8 * inner + 4]
    (yf_ref, yb_ref), states = refs[8 * inner + 4:8 * inner + 6], refs[8 * inner + 6:]

    @pl.when(pl.program_id(1) == 0)
    def _():
        for st in states:
            st[...] = jnp.zeros_like(st)

    fwd = [_SsdDir(*ins[8 * e:8 * e + 4], dbf_r, alf_r, yf_ref.at[e], states[2 * e], False, S_DTF, CHUNK - 1)
           for e in range(inner)]
    bwd = [_SsdDir(*ins[8 * e + 4:8 * e + 8], dbb_r, alb_r, yb_ref.at[e], states[2 * e + 1], True, S_DTB, 0)
           for e in range(inner)]

    def body(i, carry):
        c_f = pl.multiple_of(i * CHUNK, CHUNK)
        c_b = pl.multiple_of((n_chunks - 1 - i) * CHUNK, CHUNK)
        _ssd_chunks([(d, c_f) for d in fwd] + [(d, c_b) for d in bwd])
        return carry

    lax.fori_loop(0, n_chunks, body, 0)


def _ssd(xs_c, bm_c, cm_c, small, dtb_f, dtb_b, alog_f, alog_b, bsz, seq, lb):
    t = bsz * seq
    nb = seq // lb
    inner = SSD_INNER if bsz % SSD_INNER == 0 else 1

    def specs(e, reverse):
        idx = lambda p, n: (p * inner + e) * nb + (nb - 1 - n if reverse else n)
        return [pl.BlockSpec((lb, w), lambda p, n: (idx(p, n), 0)) for w in (SSD_DINNER, SSD_BC, SSD_BC, N_SMALL)]

    row = pl.BlockSpec((1, SSD_HEADS), lambda p, n: (0, 0))
    args = (xs_c, bm_c, cm_c, small)
    y_f, y_b = pl.pallas_call(
        functools.partial(_ssd_kernel, n_chunks=lb // CHUNK, inner=inner),
        grid=(bsz // inner, nb),
        in_specs=[s for e in range(inner) for rev in (False, True) for s in specs(e, rev)] + [row] * 4,
        out_specs=[pl.BlockSpec((None, inner, lb, SSD_DINNER), lambda p, n: (p, 0, n, 0)),
                   pl.BlockSpec((None, inner, lb, SSD_DINNER), lambda p, n: (p, 0, nb - 1 - n, 0))],
        out_shape=[jax.ShapeDtypeStruct((bsz // inner, inner, seq, SSD_DINNER), BF16)] * 2,
        scratch_shapes=[pltpu.VMEM((SSD_GROUPS, SSD_STATE, SSD_GW), F32)] * (2 * inner),
        compiler_params=pltpu.CompilerParams(
            dimension_semantics=("arbitrary", "arbitrary"), vmem_limit_bytes=VMEM_LIMIT),
        name="ssd_scan",
    )(*(args * (2 * inner)), dtb_f[None, :], dtb_b[None, :], alog_f[None, :], alog_b[None, :])
    return y_f.reshape(t, SSD_DINNER), y_b.reshape(t, SSD_DINNER)


def _merge_kernel(x_ref, of_ref, ob_ref, r_ref, gg_ref, yf_ref, yb_ref, xs_ref, z_ref, gs_ref,
                  gng_ref, dsk_ref, sng_ref, wug_ref, wus_ref, wo_ref, nfg_ref, wr_ref, br_ref,
                  h_ref, xp_ref, route_ref, gate_ref, cnt_out_ref, cnt_ref):
    @pl.when(pl.program_id(0) == 0)
    def _():
        cnt_ref[...] = jnp.zeros_like(cnt_ref)

    tm = x_ref.shape[0]
    subs = [slice(s * (tm // MERGE_SUBTILES), (s + 1) * (tm // MERGE_SUBTILES)) for s in range(MERGE_SUBTILES)]

    def gla_branch(rows):
        o = of_ref[rows, :].astype(F32) + ob_ref[rows, :].astype(F32)
        gng = gng_ref[...]
        o_parts = []
        for h in range(GLA_HEADS):
            oh = o[:, h * GLA_DV:(h + 1) * GLA_DV]
            oh = oh * lax.rsqrt(jnp.mean(oh * oh, axis=-1, keepdims=True) + EPS)
            o_parts.append(oh * gng)
        return (jnp.concatenate(o_parts, axis=1) * _silu(r_ref[rows, :]).astype(F32)).astype(BF16)

    def ssd_branch(rows):
        y = (yf_ref[rows, :].astype(F32) + yb_ref[rows, :].astype(F32)
             + dsk_ref[...] * xs_ref[rows, :].astype(F32))
        y = y * _silu(z_ref[rows, :]).astype(F32)
        sng = sng_ref[...]
        y_parts = []
        for g in range(SSD_GROUPS):
            yg = y[:, g * SSD_GW:(g + 1) * SSD_GW]
            yg = yg * lax.rsqrt(jnp.mean(yg * yg, axis=-1, keepdims=True) + EPS)
            y_parts.append(yg * sng[:, g * SSD_GW:(g + 1) * SSD_GW])
        return jnp.concatenate(y_parts, axis=1).astype(BF16)

    up_g = [_dot(gla_branch(rows), wug_ref[...]) for rows in subs]
    up_s = [_dot(ssd_branch(rows), wus_ref[...]) for rows in subs]
    mix = [(_sigmoid(gg_ref[rows, :]).astype(F32) * ug + _sigmoid(gs_ref[rows, :]).astype(F32) * us).astype(BF16)
           for rows, ug, us in zip(subs, up_g, up_s)]
    hs = [x_ref[rows, :] + _dot(m, wo_ref[...]) for rows, m in zip(subs, mix)]
    logit_parts = []
    for rows, h in zip(subs, hs):
        h_ref[rows, :] = h
        hn = (h * lax.rsqrt(jnp.mean(h * h, axis=-1, keepdims=True) + EPS)) * nfg_ref[...]
        words = _pack_bf16_pair(hn[:, :D_MODEL // 2], hn[:, D_MODEL // 2:])
        xp_ref[0, rows, :] = words[:, :SC_W]
        xp_ref[1, rows, :] = words[:, SC_W:]
        hn_hi, hn_lo = _split_bf16(hn, 2)
        logit_parts.append(_dot(hn_hi, wr_ref[0]) + (_dot(hn_hi, wr_ref[1]) + _dot(hn_lo, wr_ref[0])))
    logits = jnp.concatenate(logit_parts, axis=0) + br_ref[...]
    lane = lax.broadcasted_iota(I32, (tm, N_EXPERTS), 1).astype(F32)
    work = logits
    idxs, vals = [], []
    for _ in range(TOP_K):
        m = jnp.max(work, axis=-1, keepdims=True)
        idx = jnp.min(jnp.where(work == m, lane, float(N_EXPERTS)), axis=-1, keepdims=True)
        idxs.append(idx)
        vals.append(m)
        work = jnp.where(lane == idx, -jnp.inf, work)
    exps = [jnp.exp(v - vals[0]) for v in vals]
    denom = exps[0] + exps[1] + exps[2] + exps[3]
    gates = [e / denom for e in exps]
    sel = jnp.zeros((tm, N_EXPERTS), F32)
    for idx in idxs:
        sel = sel + jnp.where(lane == idx, 1.0, 0.0)
    rr = lax.broadcasted_iota(I32, (tm, tm), 0)
    cc = lax.broadcasted_iota(I32, (tm, tm), 1)
    strict = jnp.where(rr > cc, 1.0, 0.0).astype(BF16)
    pos = _dot(strict, sel.astype(BF16)) + cnt_ref[0:1, :]
    ranks = [jnp.sum(jnp.where(lane == idx, pos, 0.0), axis=-1, keepdims=True).astype(I32) for idx in idxs]
    cnt_new = cnt_ref[0:1, :] + jnp.sum(sel, axis=0, keepdims=True)
    cnt_ref[...] = jnp.broadcast_to(cnt_new, cnt_ref.shape)
    cnt_out_ref[...] = jnp.broadcast_to(cnt_new, cnt_ref.shape).astype(I32)
    lane128 = lax.broadcasted_iota(I32, (tm, 128), 1)
    route = jnp.zeros((tm, 128), I32)
    gate_o = jnp.zeros((tm, 128), F32)
    for k in range(TOP_K):
        route = jnp.where(lane128 == k, idxs[k].astype(I32), route)
        route = jnp.where(lane128 == TOP_K + k, ranks[k], route)
        gate_o = jnp.where(lane128 == k, gates[k], gate_o)
    route_ref[...] = route
    gate_ref[...] = gate_o


def _merge_params(gla_norm_g, ssd_d, ssd_norm_g, w_up_gla, w_up_ssd, w_out, norm_ffn_g, w_router, b_router):
    d_skip = jnp.repeat(ssd_d, SSD_HEADDIM)[None, :]
    return [gla_norm_g[None, :], d_skip, ssd_norm_g[None, :], w_up_gla.astype(BF16), w_up_ssd.astype(BF16),
            w_out.astype(BF16), norm_ffn_g[None, :], jnp.stack(_split_bf16(w_router, 2)), b_router[None, :]]


def _merge(x2, row0, o_f, o_b, proj, y_f, y_b, xs_c, params, tm):
    t = o_f.shape[0]
    blk0 = row0 // tm
    rowblk = lambda w, col=0: pl.BlockSpec((tm, w), lambda i: (i, col))
    const = lambda a: pl.BlockSpec(a.shape, lambda i: (0,) * a.ndim)
    return pl.pallas_call(
        _merge_kernel,
        grid=(t // tm,),
        in_specs=[pl.BlockSpec((tm, D_MODEL), lambda i: (blk0 + i, 0)),
                  rowblk(GLA_V), rowblk(GLA_V), rowblk(GLA_V, C_R // GLA_V),
                  rowblk(D_MODEL, C_GG // D_MODEL), rowblk(SSD_DINNER), rowblk(SSD_DINNER), rowblk(SSD_DINNER),
                  rowblk(SSD_DINNER, C_Z // SSD_DINNER), rowblk(D_MODEL, C_GS // D_MODEL)]
        + [const(p) for p in params],
        out_specs=[rowblk(D_MODEL), pl.BlockSpec((2, tm, SC_W), lambda i: (0, i, 0)), rowblk(128), rowblk(128),
                   pl.BlockSpec((8, N_EXPERTS), lambda i: (0, 0))],
        out_shape=[jax.ShapeDtypeStruct((t, D_MODEL), F32), jax.ShapeDtypeStruct((2, t, SC_W), U32),
                   jax.ShapeDtypeStruct((t, 128), I32), jax.ShapeDtypeStruct((t, 128), F32),
                   jax.ShapeDtypeStruct((8, N_EXPERTS), I32)],
        scratch_shapes=[pltpu.VMEM((8, N_EXPERTS), F32)],
        compiler_params=pltpu.CompilerParams(dimension_semantics=("arbitrary",), vmem_limit_bytes=VMEM_LIMIT),
        name="merge_router",
    )(x2, o_f, o_b, proj, proj, y_f, y_b, xs_c, proj, proj, *params)


def _sc_mesh():
    return plsc.VectorSubcoreMesh(core_axis_name="c", subcore_axis_name="s")


def _sc_scatter_rows(x, idx, n_out):
    n, m = x.shape[0], idx.shape[1]
    n_win = n // SC_WIN

    @functools.partial(pl.kernel, out_type=jax.ShapeDtypeStruct((n_out, SC_W), x.dtype), mesh=_sc_mesh())
    def scatter(x_hbm, i_hbm, o_hbm):
        def body(x_vmem, i_vmem):
            pltpu.sync_copy(x_vmem, o_hbm.at[i_vmem.at[0]])

        pltpu.emit_pipeline(
            body, grid=(m // SC_WIN,),
            in_specs=[pl.BlockSpec((SC_WIN, SC_W), lambda i: (i % n_win, 0)),
                      pl.BlockSpec((1, SC_WIN), lambda i: (0, i))],
            out_specs=[], core_axis_name=("c", "s"), dimension_semantics=(pltpu.PARALLEL,),
        )(x_hbm, i_hbm)

    return scatter(x, idx)


def _sc_gather_rows(table, idx):
    m = idx.shape[1]

    @functools.partial(pl.kernel, out_type=jax.ShapeDtypeStruct((m, SC_W), table.dtype), mesh=_sc_mesh())
    def gather(t_hbm, i_hbm, o_hbm):
        def body(i_vmem, o_vmem):
            pltpu.sync_copy(t_hbm.at[i_vmem.at[0]], o_vmem)

        pltpu.emit_pipeline(
            body, grid=(m // SC_WIN,),
            in_specs=[pl.BlockSpec((1, SC_WIN), lambda i: (0, i))],
            out_specs=[pl.BlockSpec((SC_WIN, SC_W), lambda i: (i, 0))],
            core_axis_name=("c", "s"), dimension_semantics=(pltpu.PARALLEL,),
        )(i_hbm, o_hbm)

    return gather(table, idx)


def _expert_kernel(blk_e_ref, nvalid_ref, nused_ref, x_ref, w1_ref, b1_ref, w2_ref, b2_ref, o_ref, w1_s, w2_s):
    del nused_ref
    i = pl.program_id(0)
    nvalid = nvalid_ref[i]

    @pl.when((nvalid > 0) & ((i == 0) | (blk_e_ref[i] != blk_e_ref[jnp.maximum(i - 1, 0)])))
    def _():
        w1_s[...] = w1_ref[0].astype(BF16)
        w2_s[...] = w2_ref[0].astype(BF16)

    @pl.when(nvalid > 0)
    def _():
        live = lax.broadcasted_iota(I32, (MOE_BM, SC_W), 0) < nvalid
        lo0, hi0 = _unpack_bf16_pair(jnp.where(live, x_ref[0], jnp.uint32(0)))
        lo1, hi1 = _unpack_bf16_pair(jnp.where(live, x_ref[1], jnp.uint32(0)))
        x = jnp.concatenate([lo0, lo1, hi0, hi1], axis=1).astype(BF16)
        hdn = _dot(x, w1_s[...]) + b1_ref[0]
        gate = jnp.minimum(hdn[:, :D_FF], SWIGLU_LIMIT)
        lin = jnp.clip(hdn[:, D_FF:], -SWIGLU_LIMIT, SWIGLU_LIMIT)
        act = gate * _sigmoid(SWIGLU_ALPHA * gate) * (lin + 1.0)
        y = _dot(act.astype(BF16), w2_s[...]) + b2_ref[0]
        words = _pack_bf16_pair(y[:, :D_MODEL // 2], y[:, D_MODEL // 2:])
        o_ref[0] = words[:, :SC_W]
        o_ref[1] = words[:, SC_W:]

    @pl.when(nvalid == 0)
    def _():
        o_ref[...] = jnp.zeros_like(o_ref)


def _experts(xb, blk_e, nvalid, n_used, w1, b1, w2, b2):
    n_rows = xb.shape[1]
    n_blocks = n_rows // MOE_BM
    xidx = lambda i, be, nv, nu: (0, jnp.minimum(i, nu[0] - 1), 0)
    eidx = lambda i, be, nv, nu: (be[i], 0, 0)
    return pl.pallas_call(
        _expert_kernel,
        grid_spec=pltpu.PrefetchScalarGridSpec(
            num_scalar_prefetch=3,
            grid=(n_blocks,),
            in_specs=[pl.BlockSpec((2, MOE_BM, SC_W), xidx),
                      pl.BlockSpec((1, D_MODEL, 2 * D_FF), eidx),
                      pl.BlockSpec((1, 1, 2 * D_FF), eidx),
                      pl.BlockSpec((1, D_FF, D_MODEL), eidx),
                      pl.BlockSpec((1, 1, D_MODEL), eidx)],
            out_specs=pl.BlockSpec((2, MOE_BM, SC_W), lambda i, be, nv, nu: (0, i, 0)),
            scratch_shapes=[pltpu.VMEM((D_MODEL, 2 * D_FF), BF16), pltpu.VMEM((D_FF, D_MODEL), BF16)],
        ),
        out_shape=jax.ShapeDtypeStruct((2, n_rows, SC_W), U32),
        compiler_params=pltpu.CompilerParams(dimension_semantics=("arbitrary",), vmem_limit_bytes=VMEM_LIMIT),
        name="moe_experts",
    )(blk_e, nvalid, n_used, xb, w1, b1, w2, b2)


def _combine_kernel(h_ref, gate_ref, y_ref, g_ref, *rest):
    o_ref = rest[-1]
    gates = gate_ref[...]
    acc = None
    for k in range(TOP_K):
        lo0, hi0 = _unpack_bf16_pair(y_ref[k, 0])
        lo1, hi1 = _unpack_bf16_pair(y_ref[k, 1])
        term = gates[:, k:k + 1] * jnp.concatenate([lo0, lo1, hi0, hi1], axis=1)
        acc = term if acc is None else acc + term
    h = h_ref[...] + acc
    o_ref[...] = (h * lax.rsqrt(jnp.mean(h * h, axis=-1, keepdims=True) + EPS)) * g_ref[...]


def _combine(h, gates, y_rows, norm_final_g, t_total, row0, prev, tm):
    t = h.shape[0]
    blk0 = row0 // tm
    in_specs = [pl.BlockSpec((tm, D_MODEL), lambda i: (i, 0)),
                pl.BlockSpec((tm, 128), lambda i: (i, 0)),
                pl.BlockSpec((TOP_K, 2, tm, SC_W), lambda i: (0, 0, i, 0)),
                pl.BlockSpec((1, D_MODEL), lambda i: (0, 0))]
    args = [h, gates, y_rows, norm_final_g[None, :]]
    if prev is not None:
        in_specs.append(pl.BlockSpec(memory_space=pl.ANY))
        args.append(prev)
    return pl.pallas_call(
        _combine_kernel,
        grid=(t // tm,),
        in_specs=in_specs,
        out_specs=pl.BlockSpec((tm, D_MODEL), lambda i: (blk0 + i, 0)),
        out_shape=jax.ShapeDtypeStruct((t_total, D_MODEL), F32),
        input_output_aliases={} if prev is None else {4: 0},
        compiler_params=pltpu.CompilerParams(dimension_semantics=("arbitrary",), vmem_limit_bytes=VMEM_LIMIT),
        name="moe_combine",
    )(*args)


def _pick(n, pref):
    b = min(pref, n)
    while n % b:
        b -= CHUNK
    return b


def _prep_w_in(w_in):
    widths = (GLA_QK, GLA_QK, GLA_V, GLA_V, GLA_RANK, GLA_RANK, SSD_DINNER, SSD_DINNER, SSD_BC, SSD_BC,
              2 * SSD_HEADS, D_MODEL, D_MODEL)
    pts, acc = [], 0
    for w in widths[:-1]:
        acc += w
        pts.append(acc)
    q, k, v, r, lrf, lrb, z, xs, bm, cm, dtr, gg, gs = jnp.split(w_in, pts, axis=1)
    main = jnp.concatenate([z, xs, q, k, v, r, gg, gs, bm, cm], axis=1).astype(BF16)
    pad = jnp.zeros((D_MODEL, N_SMALL - 2 * GLA_RANK - 2 * SSD_HEADS), w_in.dtype)
    small = jnp.concatenate([lrf, lrb, dtr, pad], axis=1).astype(BF16)
    return main, small


def _layer(h, norm_mix_g, w_in, gla_fw2_f, gla_fb_f, gla_fw2_b, gla_fb_b, gla_norm_g, conv_w, conv_b,
           dt_bias_f, dt_bias_b, a_log_f, a_log_b, ssd_d, ssd_norm_g, w_up_gla, w_up_ssd, w_out,
           norm_ffn_g, w_router, b_router, w1, b1, w2, b2, out_norm_g):
    bsz, seq, _ = h.shape
    t_total = bsz * seq
    x2 = h.reshape(t_total, D_MODEL)
    w_main, w_small = _prep_w_in(w_in)
    fw2f, fw2b = gla_fw2_f.astype(BF16), gla_fw2_b.astype(BF16)
    mparams = _merge_params(gla_norm_g, ssd_d, ssd_norm_g, w_up_gla, w_up_ssd, w_out, norm_ffn_g, w_router, b_router)
    n_groups = TOKEN_GROUPS if bsz % TOKEN_GROUPS == 0 else 1
    gb = bsz // n_groups
    t = gb * seq
    lb = _pick(seq, 512)
    out = None
    for grp in range(n_groups):
        row0 = grp * t
        proj, small = _inproj(x2, norm_mix_g[None, :], w_main, w_small, row0, t, tm=_pick(t, 2048), tn=1024)
        o_f, o_b = _gla(proj, small, fw2f, gla_fb_f[None, :], fw2b, gla_fb_b[None, :], gb, seq, lb)
        xs_c, bm_c, cm_c = _conv(proj, conv_w, conv_b, seq, _pick(seq, 512))
        y_f, y_b = _ssd(xs_c, bm_c, cm_c, small, dt_bias_f, dt_bias_b, a_log_f, a_log_b, gb, seq, lb)
        hres, xp, route, gates, counts8 = _merge(x2, row0, o_f, o_b, proj, y_f, y_b, xs_c, mparams, tm=_pick(t, 256))
        counts = counts8[0]
        padded = (counts + MOE_BM - 1) // MOE_BM * MOE_BM
        pend = jnp.cumsum(padded)
        pstart = (pend - padded).astype(I32)
        n_rows = t * TOP_K + N_EXPERTS * MOE_BM
        blk_row = jnp.arange(n_rows // MOE_BM, dtype=I32) * MOE_BM
        blk_e = jnp.minimum(jnp.sum(pend[None, :] <= blk_row[:, None], axis=1), N_EXPERTS - 1).astype(I32)
        nvalid = jnp.clip(pstart[blk_e] + counts[blk_e] - blk_row, 0, MOE_BM).astype(I32)
        n_used = (pend[-1:] // MOE_BM).astype(I32)
        top_e, rank = route[:, :TOP_K], route[:, TOP_K:2 * TOP_K]
        dest = jnp.sum(jnp.where(top_e[:, :, None] == jnp.arange(N_EXPERTS, dtype=I32), pstart, 0), axis=-1) + rank
        idx = (dest.T[:, None, :] + (jnp.arange(2, dtype=I32) * n_rows)[None, :, None]).reshape(1, 2 * TOP_K * t)
        xb = _sc_scatter_rows(xp.reshape(2 * t, SC_W), idx, 2 * n_rows).reshape(2, n_rows, SC_W)
        yb = _experts(xb, blk_e, nvalid, n_used, w1, b1[:, None, :], w2, b2[:, None, :])
        y_rows = _sc_gather_rows(yb.reshape(2 * n_rows, SC_W), idx).reshape(TOP_K, 2, t, SC_W)
        out = _combine(hres, gates, y_rows, out_norm_g, t_total, row0, out, tm=_pick(t, 512))
    return out.reshape(bsz, seq, D_MODEL)


def kernel(x, norm_mix_g, w_in, gla_fw2_f, gla_fb_f, gla_fw2_b, gla_fb_b, gla_norm_g, conv_w, conv_b, dt_bias_f,
           dt_bias_b, a_log_f, a_log_b, ssd_d, ssd_norm_g, w_up_gla, w_up_ssd, w_out, norm_ffn_g, w_router,
           b_router, w1, b1, w2, b2, norm_final_g):
    assert x.shape[-1] == D_MODEL and norm_mix_g.shape[0] == 1
    return _layer(x, norm_mix_g[0], w_in[0], gla_fw2_f[0], gla_fb_f[0], gla_fw2_b[0], gla_fb_b[0], gla_norm_g[0],
                  conv_w[0], conv_b[0], dt_bias_f[0], dt_bias_b[0], a_log_f[0], a_log_b[0], ssd_d[0],
                  ssd_norm_g[0], w_up_gla[0], w_up_ssd[0], w_out[0], norm_ffn_g[0], w_router[0], b_router[0],
                  w1[0], b1[0], w2[0], b2[0], norm_final_g)
```

```python
import functools

import jax
import jax.numpy as jnp
from jax import lax
from jax.experimental import pallas as pl
from jax.experimental.pallas import tpu as pltpu
from jax.experimental.pallas import tpu_sc as plsc

F32 = jnp.float32
BF16 = jnp.bfloat16
I32 = jnp.int32
U32 = jnp.uint32

D_MODEL = 1024
EPS = 1e-5
GLA_HEADS = 4
GLA_DK = 128
GLA_DV = 256
GLA_RANK = 16
GLA_TAU = 16.0
GLA_QK = GLA_HEADS * GLA_DK
GLA_V = GLA_HEADS * GLA_DV
SSD_DINNER = 2048
SSD_HEADDIM = 64
SSD_HEADS = 32
SSD_GROUPS = 4
SSD_HPG = 8
SSD_STATE = 128
SSD_CONV = 4
SSD_BC = SSD_GROUPS * SSD_STATE
SSD_GW = SSD_HPG * SSD_HEADDIM
N_EXPERTS = 32
TOP_K = 4
D_FF = 1024
SWIGLU_LIMIT = 7.0
SWIGLU_ALPHA = 1.702
CHUNK = 64
TOKEN_GROUPS = 2
MERGE_SUBTILES = 2
SSD_INNER = 2
MXU_LAG = 6

C_Z, C_XS, C_Q, C_K, C_V, C_R, C_GG, C_GS, C_B, C_C = 0, 2048, 4096, 4608, 5120, 6144, 7168, 8192, 9216, 9728
N_MAIN = 10240
N_SMALL = 128
S_LRF, S_LRB, S_DTF, S_DTB = 0, 16, 32, 64

VMEM_LIMIT = 56 * 1024 * 1024
MOE_BM = 512
SC_WIN = 128
SC_W = D_MODEL // 4


def _dot(a, b):
    return jnp.dot(a, b, preferred_element_type=F32)


def _dot_nt(a, b):
    return lax.dot_general(a, b, (((1,), (1,)), ((), ())), preferred_element_type=F32)


def _dot_tn(a, b):
    return lax.dot_general(a, b, (((0,), (0,)), ((), ())), preferred_element_type=F32)


def _split_bf16(x, n):
    parts = []
    r = x
    for _ in range(n):
        p = r.astype(BF16)
        parts.append(p)
        r = r - p.astype(F32)
    return parts


def _dot_exact_lhs(m_bf16, x, n):
    acc = None
    for p in _split_bf16(x, n):
        t = _dot(m_bf16, p)
        acc = t if acc is None else acc + t
    return acc


def _dot_exact_rhs(x, m_bf16, n):
    acc = None
    for p in _split_bf16(x, n):
        t = _dot(p, m_bf16)
        acc = t if acc is None else acc + t
    return acc


def _sigmoid(x):
    return 1.0 / (1.0 + jnp.exp2(x * (-1.4426950408889634)))


def _silu(x):
    return x * _sigmoid(x)


def _pack_bf16_pair(lo, hi):
    lo_b = lax.bitcast_convert_type(lo.astype(BF16).astype(F32), U32)
    hi_b = lax.bitcast_convert_type(hi.astype(BF16).astype(F32), U32)
    return (lo_b >> 16) | (hi_b & jnp.uint32(0xFFFF0000))


def _unpack_bf16_pair(u):
    lo = lax.bitcast_convert_type(u << 16, F32)
    hi = lax.bitcast_convert_type(u & jnp.uint32(0xFFFF0000), F32)
    return lo, hi


def _inproj_kernel(x_ref, g_ref, w_ref, ws_ref, o_ref, os_ref, xn_ref):
    @pl.when(pl.program_id(1) == 0)
    def _():
        x = x_ref[...]
        ms = jnp.mean(x * x, axis=-1, keepdims=True)
        xb = ((x * lax.rsqrt(ms + EPS)) * g_ref[...]).astype(BF16)
        xn_ref[...] = xb
        os_ref[...] = _dot(xb, ws_ref[...])

    o_ref[...] = _dot(xn_ref[...], w_ref[...]).astype(BF16)


def _inproj(x2, g, w_main, w_small, row0, t, tm, tn):
    blk0 = row0 // tm
    return pl.pallas_call(
        _inproj_kernel,
        grid=(t // tm, N_MAIN // tn),
        in_specs=[
            pl.BlockSpec((tm, D_MODEL), lambda i, j: (blk0 + i, 0)),
            pl.BlockSpec((1, D_MODEL), lambda i, j: (0, 0)),
            pl.BlockSpec((D_MODEL, tn), lambda i, j: (0, j)),
            pl.BlockSpec((D_MODEL, N_SMALL), lambda i, j: (0, 0)),
        ],
        out_specs=[
            pl.BlockSpec((tm, tn), lambda i, j: (i, j)),
            pl.BlockSpec((tm, N_SMALL), lambda i, j: (i, 0)),
        ],
        out_shape=[
            jax.ShapeDtypeStruct((t, N_MAIN), BF16),
            jax.ShapeDtypeStruct((t, N_SMALL), F32),
        ],
        scratch_shapes=[pltpu.VMEM((tm, D_MODEL), BF16)],
        compiler_params=pltpu.CompilerParams(
            dimension_semantics=("arbitrary", "arbitrary"), vmem_limit_bytes=VMEM_LIMIT),
        name="inproj",
    )(x2, g, w_main, w_small)


def _tri_masks(n):
    r = lax.broadcasted_iota(I32, (n, n), 0)
    c = lax.broadcasted_iota(I32, (n, n), 1)
    return r >= c, c >= r


class _GlaDir:
    def __init__(self, ins, outs, scratch, mask, lr_off, mid_row, last_row):
        self.q, self.k, self.v, self.sm, self.fw2, self.fb = ins
        self.o, self.st = outs
        self.b_s, self.qs_s, self.ks_s, self.kd_s, self.qe_s, self.p_s, self.u_s, self.el_s = scratch
        self.mask, self.lr_off, self.mid_row, self.last_row = mask, lr_off, mid_row, last_row


def _gla_decay(d, n_chunks):
    tri = jnp.where(d.mask, 1.0, 0.0).astype(BF16)
    lr = d.sm[:, d.lr_off:d.lr_off + GLA_RANK].astype(BF16)
    xg = _dot(lr, d.fw2[...]) + d.fb[...]
    d.b_s[...] = (jnp.minimum(xg, 0.0) - jnp.log(1.0 + jnp.exp(-jnp.abs(xg)))) * (1.0 / GLA_TAU)
    for c in range(n_chunks):
        rows = slice(c * CHUNK, (c + 1) * CHUNK)
        d.b_s[rows, :] = _dot_exact_lhs(tri, d.b_s[rows, :], 2)


def _gla_scale(d, c):
    rows = slice(c * CHUNK, (c + 1) * CHUNK)
    b = d.b_s[rows, :]
    b_mid = b[d.mid_row:d.mid_row + 1, :]
    b_last = b[d.last_row:d.last_row + 1, :]
    q = d.q[rows, :].astype(F32) * (GLA_DK ** -0.5)
    k = d.k[rows, :].astype(F32)
    d.qs_s[rows, :] = (q * jnp.exp(b - b_mid)).astype(BF16)
    d.ks_s[rows, :] = (k * jnp.exp(b_mid - b)).astype(BF16)
    d.kd_s[rows, :] = (k * jnp.exp(b_last - b)).astype(BF16)
    d.qe_s[rows, :] = (q * jnp.exp(b)).astype(BF16)
    d.el_s[c:c + 1, :] = jnp.exp(b_last)


def _gla_local(units):
    def score(u):
        d, c, h = u
        rows, ks_ = slice(c * CHUNK, (c + 1) * CHUNK), slice(h * GLA_DK, (h + 1) * GLA_DK)
        return _dot_nt(d.qs_s[rows, ks_], d.ks_s[rows, ks_])

    def finish(u, s):
        d, c, h = u
        rows, ks_ = slice(c * CHUNK, (c + 1) * CHUNK), slice(h * GLA_DK, (h + 1) * GLA_DK)
        vs_ = slice(h * GLA_DV, (h + 1) * GLA_DV)
        d.p_s[c, h] = jnp.where(d.mask, s, 0.0).astype(BF16)
        d.u_s[c, h] = _dot_tn(d.kd_s[rows, ks_], d.v[rows, vs_])

    pending = []
    for u in units:
        pending.append((u, score(u)))
        if len(pending) > MXU_LAG:
            finish(*pending.pop(0))
    for item in pending:
        finish(*item)


def _gla_carry(d, c):
    rows = slice(c * CHUNK, (c + 1) * CHUNK)
    for h in range(GLA_HEADS):
        ks_ = slice(h * GLA_DK, (h + 1) * GLA_DK)
        vs_ = slice(h * GLA_DV, (h + 1) * GLA_DV)
        st = d.st[h]
        o = _dot(d.p_s[c, h], d.v[rows, vs_]) + _dot(d.qe_s[rows, ks_], st.astype(BF16))
        d.o[rows, vs_] = o.astype(d.o.dtype)
        e_col = jnp.transpose(jnp.broadcast_to(d.el_s[c:c + 1, ks_], (8, GLA_DK)))[:, 0:1]
        d.st[h] = st * e_col + d.u_s[c, h]


def _gla_kernel(*refs, n_chunks):
    ins_f, ins_b, (fw2f, fbf, fw2b, fbb), (of_ref, ob_ref, stf, stb) = refs[0:4], refs[4:8], refs[8:12], refs[12:16]
    scr_f, scr_b = refs[16:24], refs[24:32]

    @pl.when(pl.program_id(1) == 0)
    def _():
        stf[...] = jnp.zeros_like(stf)
        stb[...] = jnp.zeros_like(stb)

    lower, upper = _tri_masks(CHUNK)
    fwd = _GlaDir((*ins_f, fw2f, fbf), (of_ref, stf), scr_f, lower, S_LRF, CHUNK // 2, CHUNK - 1)
    bwd = _GlaDir((*ins_b, fw2b, fbb), (ob_ref, stb), scr_b, upper, S_LRB, CHUNK // 2 - 1, 0)
    _gla_decay(fwd, n_chunks)
    _gla_decay(bwd, n_chunks)
    for c in range(n_chunks):
        _gla_scale(fwd, c)
        _gla_scale(bwd, c)
    _gla_local([(d, c, h) for c in range(n_chunks) for d in (fwd, bwd) for h in range(GLA_HEADS)])
    for i in range(n_chunks):
        _gla_carry(fwd, i)
        _gla_carry(bwd, n_chunks - 1 - i)


def _gla(proj, small, fw2f, fbf, fw2b, fbb, bsz, seq, lb):
    t = bsz * seq
    nb = seq // lb

    def fwd(w, col):
        return pl.BlockSpec((lb, w), lambda b, n: (b * nb + n, col))

    def bwd(w, col):
        return pl.BlockSpec((lb, w), lambda b, n: (b * nb + nb - 1 - n, col))

    const = lambda shape: pl.BlockSpec(shape, lambda b, n: (0, 0))
    per_dir_scratch = ([pltpu.VMEM((lb, GLA_QK), F32)] + [pltpu.VMEM((lb, GLA_QK), BF16)] * 4
                       + [pltpu.VMEM((lb // CHUNK, GLA_HEADS, CHUNK, CHUNK), BF16),
                          pltpu.VMEM((lb // CHUNK, GLA_HEADS, GLA_DK, GLA_DV), F32),
                          pltpu.VMEM((lb // CHUNK, GLA_QK), F32)])
    return pl.pallas_call(
        functools.partial(_gla_kernel, n_chunks=lb // CHUNK),
        grid=(bsz, nb),
        in_specs=[
            fwd(GLA_QK, C_Q // GLA_QK), fwd(GLA_QK, C_K // GLA_QK), fwd(GLA_V, C_V // GLA_V), fwd(N_SMALL, 0),
            bwd(GLA_QK, C_Q // GLA_QK), bwd(GLA_QK, C_K // GLA_QK), bwd(GLA_V, C_V // GLA_V), bwd(N_SMALL, 0),
            const((GLA_RANK, GLA_QK)), const((1, GLA_QK)), const((GLA_RANK, GLA_QK)), const((1, GLA_QK)),
        ],
        out_specs=[fwd(GLA_V, 0), bwd(GLA_V, 0)],
        out_shape=[jax.ShapeDtypeStruct((t, GLA_V), BF16)] * 2,
        scratch_shapes=[pltpu.VMEM((GLA_HEADS, GLA_DK, GLA_DV), F32)] * 2 + per_dir_scratch * 2,
        compiler_params=pltpu.CompilerParams(
            dimension_semantics=("arbitrary", "arbitrary"), vmem_limit_bytes=VMEM_LIMIT),
        name="gla_scan",
    )(proj, proj, proj, small, proj, proj, proj, small, fw2f, fbf, fw2b, fbb)


HALO = 16


CONV_SUB = 256
CONV_COLS = 512


def _conv_taps(x, xm1, xp1, xp2, w, b):
    return _silu(xm1 * w[0:1, :] + x * w[1:2, :] + xp1 * w[2:3, :] + xp2 * w[3:4, :] + b)


def _shift_matrix(n):
    r = lax.broadcasted_iota(I32, (3 * n, n), 0)
    c = lax.broadcasted_iota(I32, (3 * n, n), 1)
    src = jnp.where(r < n, r - 1, jnp.where(r < 2 * n, r - n + 1, r - 2 * n + 2))
    return jnp.where(c == src, 1.0, 0.0).astype(BF16)


def _conv_one(x_ref, p_ref, n_ref, w_ref, b_ref, o_ref, shift, has_prev, has_next):
    rb, wd = x_ref.shape
    w, b = w_ref[...], b_ref[...]
    for s in range(rb // CONV_SUB):
        rows = slice(s * CONV_SUB, (s + 1) * CONV_SUB)
        for c0 in range(0, wd, CONV_COLS):
            cols = slice(c0, c0 + CONV_COLS)
            xb = x_ref[rows, cols]
            sx = _dot(shift, xb)
            y = _conv_taps(xb.astype(F32), sx[0:CONV_SUB], sx[CONV_SUB:2 * CONV_SUB], sx[2 * CONV_SUB:],
                           w[:, cols], b[:, cols])
            o_ref[rows, cols] = y.astype(o_ref.dtype)
    prev = jnp.where(has_prev, p_ref[HALO - 1:HALO, :].astype(F32), 0.0)
    nxt = jnp.where(has_next, n_ref[0:2, :].astype(F32), 0.0)
    row = lax.broadcasted_iota(I32, (HALO, wd), 0)
    head = x_ref[0:2 * HALO, :].astype(F32)
    xm1 = jnp.where(row == 0, prev, pltpu.roll(head, 1, 0)[0:HALO])
    o_ref[0:HALO, :] = _conv_taps(head[0:HALO], xm1, pltpu.roll(head, 2 * HALO - 1, 0)[0:HALO],
                                  pltpu.roll(head, 2 * HALO - 2, 0)[0:HALO], w, b).astype(o_ref.dtype)
    tail = x_ref[rb - 2 * HALO:rb, :].astype(F32)
    xp1 = jnp.where(row == HALO - 1, nxt[0:1, :], pltpu.roll(tail, 2 * HALO - 1, 0)[HALO:])
    xp2 = jnp.where(row == HALO - 2, nxt[0:1, :],
                    jnp.where(row == HALO - 1, nxt[1:2, :], pltpu.roll(tail, 2 * HALO - 2, 0)[HALO:]))
    o_ref[rb - HALO:rb, :] = _conv_taps(tail[HALO:], pltpu.roll(tail, 1, 0)[HALO:], xp1, xp2, w, b).astype(o_ref.dtype)
    for s in range(1, rb // CONV_SUB):
        e = s * CONV_SUB
        win = x_ref[e - 2 * HALO:e + 2 * HALO, :].astype(F32)
        mid = slice(HALO, 3 * HALO)
        o_ref[e - HALO:e + HALO, :] = _conv_taps(
            win[mid], pltpu.roll(win, 1, 0)[mid], pltpu.roll(win, 4 * HALO - 1, 0)[mid],
            pltpu.roll(win, 4 * HALO - 2, 0)[mid], w, b).astype(o_ref.dtype)


def _conv_kernel(xs, xsp, xsn, bm, bmp, bmn, cm, cmp_, cmn, wx, bx, wb, bb, wc, bc, oxs, obm, ocm, *, rb, seq):
    t0 = pl.program_id(0) * rb
    has_prev = (t0 % seq) != 0
    has_next = ((t0 + rb) % seq) != 0
    shift = _shift_matrix(CONV_SUB)
    _conv_one(xs, xsp, xsn, wx, bx, oxs, shift, has_prev, has_next)
    _conv_one(bm, bmp, bmn, wb, bb, obm, shift, has_prev, has_next)
    _conv_one(cm, cmp_, cmn, wc, bc, ocm, shift, has_prev, has_next)


def _conv(proj, conv_w, conv_b, seq, rb):
    t = proj.shape[0]
    nh = t // HALO
    per = rb // HALO

    def trio(w, col):
        cb = col // w
        return [
            pl.BlockSpec((rb, w), lambda i: (i, cb)),
            pl.BlockSpec((HALO, w), lambda i: (jnp.maximum(i * per - 1, 0), cb)),
            pl.BlockSpec((HALO, w), lambda i: (jnp.minimum((i + 1) * per, nh - 1), cb)),
        ]

    def wspecs(w):
        return [pl.BlockSpec((SSD_CONV, w), lambda i: (0, 0)), pl.BlockSpec((1, w), lambda i: (0, 0))]

    wx, wb, wc = conv_w[:, :SSD_DINNER], conv_w[:, SSD_DINNER:SSD_DINNER + SSD_BC], conv_w[:, SSD_DINNER + SSD_BC:]
    bx, bb, bc = (conv_b[None, :SSD_DINNER], conv_b[None, SSD_DINNER:SSD_DINNER + SSD_BC],
                  conv_b[None, SSD_DINNER + SSD_BC:])
    return pl.pallas_call(
        functools.partial(_conv_kernel, rb=rb, seq=seq),
        grid=(t // rb,),
        in_specs=trio(SSD_DINNER, C_XS) + trio(SSD_BC, C_B) + trio(SSD_BC, C_C)
        + wspecs(SSD_DINNER) + wspecs(SSD_BC) + wspecs(SSD_BC),
        out_specs=[pl.BlockSpec((rb, SSD_DINNER), lambda i: (i, 0)),
                   pl.BlockSpec((rb, SSD_BC), lambda i: (i, 0)),
                   pl.BlockSpec((rb, SSD_BC), lambda i: (i, 0))],
        out_shape=[jax.ShapeDtypeStruct((t, SSD_DINNER), BF16),
                   jax.ShapeDtypeStruct((t, SSD_BC), BF16),
                   jax.ShapeDtypeStruct((t, SSD_BC), BF16)],
        compiler_params=pltpu.CompilerParams(dimension_semantics=("arbitrary",), vmem_limit_bytes=VMEM_LIMIT),
        name="ssd_conv",
    )(proj, proj, proj, proj, proj, proj, proj, proj, proj, wx, bx, wb, bb, wc, bc)


def _softplus(x):
    return jnp.maximum(x, 0.0) + jnp.log(1.0 + jnp.exp(-jnp.abs(x)))


class _SsdDir:
    def __init__(self, xs, bm, cm, sm, dtb_row, alog_row, y, st, reverse, dt_off, last_row):
        self.xs, self.bm, self.cm, self.sm, self.dtb_row, self.alog_row = xs, bm, cm, sm, dtb_row, alog_row
        self.y, self.st, self.reverse, self.dt_off, self.last_row = y, st, reverse, dt_off, last_row


def _ssd_chunks(work):
    hh = lax.broadcasted_iota(I32, (SSD_HEADS, SSD_GW), 0)
    cc = lax.broadcasted_iota(I32, (SSD_HEADS, SSD_GW), 1)
    lane = lax.broadcasted_iota(I32, (CHUNK, 2 * SSD_HEADDIM), 1)
    row2 = lax.broadcasted_iota(I32, (CHUNK, 2 * SSD_HEADDIM), 0)
    left = lane < SSD_HEADDIM
    col2 = jnp.where(left, lane, lane - SSD_HEADDIM)
    lower, upper = _tri_masks(CHUNK)

    pre = []
    for d, c0 in work:
        rows = pl.ds(c0, CHUNK)
        tri = jnp.where(upper if d.reverse else lower, 1.0, 0.0).astype(BF16)
        a_row = -jnp.exp(d.alog_row[...])
        dt = _softplus(d.sm[rows, d.dt_off:d.dt_off + SSD_HEADS] + d.dtb_row[...])
        pre.append((dt, _dot_exact_lhs(tri, dt * a_row, 3)))
    heads = []
    for (d, c0), (dt, cum) in zip(work, pre):
        total = cum[d.last_row:d.last_row + 1, :]
        to_end = jnp.exp(total - cum) * dt
        fac = jnp.concatenate([to_end, jnp.exp(cum)], axis=0).astype(BF16)
        e_tot = jnp.broadcast_to(jnp.exp(total), (8, SSD_HEADS))
        heads.append((cum, jnp.transpose(cum), jnp.transpose(dt), fac, e_tot))

    units = [(w, g) for w in range(len(work)) for g in range(SSD_GROUPS)]
    groups = {}
    for w, g in units:
        d, c0 = work[w]
        rows, ns = pl.ds(c0, CHUNK), slice(g * SSD_STATE, (g + 1) * SSD_STATE)
        fac, e_tot = heads[w][3], heads[w][4]
        expand = jnp.where(cc // SSD_HEADDIM + g * SSD_HPG == hh, 1.0, 0.0).astype(BF16)
        fac_x = _dot(fac, expand)
        et_x = _dot_exact_rhs(e_tot, expand, 2)[0:1]
        groups[w, g] = (fac_x, et_x, _dot_nt(d.cm[rows, ns], d.bm[rows, ns]))

    for w, g in units:
        d, c0 = work[w]
        rows, ns, cs = pl.ds(c0, CHUNK), slice(g * SSD_STATE, (g + 1) * SSD_STATE), slice(g * SSD_GW, (g + 1) * SSD_GW)
        cum, cum_t, dt_t = heads[w][0:3]
        fac_x, _, cb = groups[w, g]
        mask2 = (col2 >= row2) if d.reverse else (row2 >= col2)
        y_inter = _dot(d.cm[rows, ns], d.st[g].astype(BF16))
        cb2 = jnp.concatenate([cb, cb], axis=1)
        parts = []
        for p in range(SSD_HPG // 2):
            h0 = g * SSD_HPG + 2 * p
            ps = slice(h0 * SSD_HEADDIM, (h0 + 2) * SSD_HEADDIM)
            col = jnp.where(left, cum[:, h0:h0 + 1], cum[:, h0 + 1:h0 + 2])
            rowv = jnp.concatenate([cum_t[h0:h0 + 1, :], cum_t[h0 + 1:h0 + 2, :]], axis=1)
            dtv = jnp.concatenate([dt_t[h0:h0 + 1, :], dt_t[h0 + 1:h0 + 2, :]], axis=1)
            decay = jnp.exp(jnp.where(mask2, col - rowv, -jnp.inf))
            wgt = (cb2 * decay * dtv).astype(BF16)
            xp = d.xs[rows, ps]
            zero = jnp.zeros_like(xp)
            xbd = jnp.concatenate([jnp.where(left, xp, zero), jnp.where(left, zero, xp)], axis=0)
            parts.append(_dot(wgt, xbd))
        y = jnp.concatenate(parts, axis=1) + y_inter * fac_x[CHUNK:2 * CHUNK]
        d.y[rows, cs] = y.astype(d.y.dtype)

    def increment(w, g):
        d, c0 = work[w]
        rows, ns, cs = pl.ds(c0, CHUNK), slice(g * SSD_STATE, (g + 1) * SSD_STATE), slice(g * SSD_GW, (g + 1) * SSD_GW)
        return _dot_tn(d.bm[rows, ns], d.xs[rows, cs] * groups[w, g][0][0:CHUNK].astype(BF16))

    def update(w, g, inc):
        d = work[w][0]
        d.st[g] = d.st[g] * groups[w, g][1] + inc

    pending = None
    for w, g in units:
        inc = increment(w, g)
        if pending is not None:
            update(*pending)
        pending = (w, g, inc)
    update(*pending)


def _ssd_kernel(*refs, n_chunks, inner):
    ins, (dbf_r, dbb_r, alf_r, alb_r) = refs[:8 * inner], refs[8 * inner:8 * inner + 4]
    (yf_ref, yb_ref), states = refs[8 * inner + 4:8 * inner + 6], refs[8 * inner + 6:]

    @pl.when(pl.program_id(1) == 0)
    def _():
        for st in states:
            st[...] = jnp.zeros_like(st)

    fwd = [_SsdDir(*ins[8 * e:8 * e + 4], dbf_r, alf_r, yf_ref.at[e], states[2 * e], False, S_DTF, CHUNK - 1)
           for e in range(inner)]
    bwd = [_SsdDir(*ins[8 * e + 4:8 * e + 8], dbb_r, alb_r, yb_ref.at[e], states[2 * e + 1], True, S_DTB, 0)
           for e in range(inner)]

    def body(i, carry):
        c_f = pl.multiple_of(i * CHUNK, CHUNK)
        c_b = pl.multiple_of((n_chunks - 1 - i) * CHUNK, CHUNK)
        _ssd_chunks([(d, c_f) for d in fwd] + [(d, c_b) for d in bwd])
        return carry

    lax.fori_loop(0, n_chunks, body, 0)


def _ssd(xs_c, bm_c, cm_c, small, dtb_f, dtb_b, alog_f, alog_b, bsz, seq, lb):
    t = bsz * seq
    nb = seq // lb
    inner = SSD_INNER if bsz % SSD_INNER == 0 else 1

    def specs(e, reverse):
        idx = lambda p, n: (p * inner + e) * nb + (nb - 1 - n if reverse else n)
        return [pl.BlockSpec((lb, w), lambda p, n: (idx(p, n), 0)) for w in (SSD_DINNER, SSD_BC, SSD_BC, N_SMALL)]

    row = pl.BlockSpec((1, SSD_HEADS), lambda p, n: (0, 0))
    args = (xs_c, bm_c, cm_c, small)
    y_f, y_b = pl.pallas_call(
        functools.partial(_ssd_kernel, n_chunks=lb // CHUNK, inner=inner),
        grid=(bsz // inner, nb),
        in_specs=[s for e in range(inner) for rev in (False, True) for s in specs(e, rev)] + [row] * 4,
        out_specs=[pl.BlockSpec((None, inner, lb, SSD_DINNER), lambda p, n: (p, 0, n, 0)),
                   pl.BlockSpec((None, inner, lb, SSD_DINNER), lambda p, n: (p, 0, nb - 1 - n, 0))],
        out_shape=[jax.ShapeDtypeStruct((bsz // inner, inner, seq, SSD_DINNER), BF16)] * 2,
        scratch_shapes=[pltpu.VMEM((SSD_GROUPS, SSD_STATE, SSD_GW), F32)] * (2 * inner),
        compiler_params=pltpu.CompilerParams(
            dimension_semantics=("arbitrary", "arbitrary"), vmem_limit_bytes=VMEM_LIMIT),
        name="ssd_scan",
    )(*(args * (2 * inner)), dtb_f[None, :], dtb_b[None, :], alog_f[None, :], alog_b[None, :])
    return y_f.reshape(t, SSD_DINNER), y_b.reshape(t, SSD_DINNER)


def _merge_kernel(x_ref, of_ref, ob_ref, r_ref, gg_ref, yf_ref, yb_ref, xs_ref, z_ref, gs_ref,
                  gng_ref, dsk_ref, sng_ref, wug_ref, wus_ref, wo_ref, nfg_ref, wr_ref, br_ref,
                  h_ref, xp_ref, route_ref, gate_ref, cnt_out_ref, cnt_ref):
    @pl.when(pl.program_id(0) == 0)
    def _():
        cnt_ref[...] = jnp.zeros_like(cnt_ref)

    tm = x_ref.shape[0]
    subs = [slice(s * (tm // MERGE_SUBTILES), (s + 1) * (tm // MERGE_SUBTILES)) for s in range(MERGE_SUBTILES)]

    def gla_branch(rows):
        o = of_ref[rows, :].astype(F32) + ob_ref[rows, :].astype(F32)
        gng = gng_ref[...]
        o_parts = []
        for h in range(GLA_HEADS):
            oh = o[:, h * GLA_DV:(h + 1) * GLA_DV]
            oh = oh * lax.rsqrt(jnp.mean(oh * oh, axis=-1, keepdims=True) + EPS)
            o_parts.append(oh * gng)
        return (jnp.concatenate(o_parts, axis=1) * _silu(r_ref[rows, :]).astype(F32)).astype(BF16)

    def ssd_branch(rows):
        y = (yf_ref[rows, :].astype(F32) + yb_ref[rows, :].astype(F32)
             + dsk_ref[...] * xs_ref[rows, :].astype(F32))
        y = y * _silu(z_ref[rows, :]).astype(F32)
        sng = sng_ref[...]
        y_parts = []
        for g in range(SSD_GROUPS):
            yg = y[:, g * SSD_GW:(g + 1) * SSD_GW]
            yg = yg * lax.rsqrt(jnp.mean(yg * yg, axis=-1, keepdims=True) + EPS)
            y_parts.append(yg * sng[:, g * SSD_GW:(g + 1) * SSD_GW])
        return jnp.concatenate(y_parts, axis=1).astype(BF16)

    up_g = [_dot(gla_branch(rows), wug_ref[...]) for rows in subs]
    up_s = [_dot(ssd_branch(rows), wus_ref[...]) for rows in subs]
    mix = [(_sigmoid(gg_ref[rows, :]).astype(F32) * ug + _sigmoid(gs_ref[rows, :]).astype(F32) * us).astype(BF16)
           for rows, ug, us in zip(subs, up_g, up_s)]
    hs = [x_ref[rows, :] + _dot(m, wo_ref[...]) for rows, m in zip(subs, mix)]
    logit_parts = []
    for rows, h in zip(subs, hs):
        h_ref[rows, :] = h
        hn = (h * lax.rsqrt(jnp.mean(h * h, axis=-1, keepdims=True) + EPS)) * nfg_ref[...]
        words = _pack_bf16_pair(hn[:, :D_MODEL // 2], hn[:, D_MODEL // 2:])
        xp_ref[0, rows, :] = words[:, :SC_W]
        xp_ref[1, rows, :] = words[:, SC_W:]
        hn_hi, hn_lo = _split_bf16(hn, 2)
        logit_parts.append(_dot(hn_hi, wr_ref[0]) + (_dot(hn_hi, wr_ref[1]) + _dot(hn_lo, wr_ref[0])))
    logits = jnp.concatenate(logit_parts, axis=0) + br_ref[...]
    lane = lax.broadcasted_iota(I32, (tm, N_EXPERTS), 1).astype(F32)
    work = logits
    idxs, vals = [], []
    for _ in range(TOP_K):
        m = jnp.max(work, axis=-1, keepdims=True)
        idx = jnp.min(jnp.where(work == m, lane, float(N_EXPERTS)), axis=-1, keepdims=True)
        idxs.append(idx)
        vals.append(m)
        work = jnp.where(lane == idx, -jnp.inf, work)
    exps = [jnp.exp(v - vals[0]) for v in vals]
    denom = exps[0] + exps[1] + exps[2] + exps[3]
    gates = [e / denom for e in exps]
    sel = jnp.zeros((tm, N_EXPERTS), F32)
    for idx in idxs:
        sel = sel + jnp.where(lane == idx, 1.0, 0.0)
    rr = lax.broadcasted_iota(I32, (tm, tm), 0)
    cc = lax.broadcasted_iota(I32, (tm, tm), 1)
    strict = jnp.where(rr > cc, 1.0, 0.0).astype(BF16)
    pos = _dot(strict, sel.astype(BF16)) + cnt_ref[0:1, :]
    ranks = [jnp.sum(jnp.where(lane == idx, pos, 0.0), axis=-1, keepdims=True).astype(I32) for idx in idxs]
    cnt_new = cnt_ref[0:1, :] + jnp.sum(sel, axis=0, keepdims=True)
    cnt_ref[...] = jnp.broadcast_to(cnt_new, cnt_ref.shape)
    cnt_out_ref[...] = jnp.broadcast_to(cnt_new, cnt_ref.shape).astype(I32)
    lane128 = lax.broadcasted_iota(I32, (tm, 128), 1)
    route = jnp.zeros((tm, 128), I32)
    gate_o = jnp.zeros((tm, 128), F32)
    for k in range(TOP_K):
        route = jnp.where(lane128 == k, idxs[k].astype(I32), route)
        route = jnp.where(lane128 == TOP_K + k, ranks[k], route)
        gate_o = jnp.where(lane128 == k, gates[k], gate_o)
    route_ref[...] = route
    gate_ref[...] = gate_o


def _merge_params(gla_norm_g, ssd_d, ssd_norm_g, w_up_gla, w_up_ssd, w_out, norm_ffn_g, w_router, b_router):
    d_skip = jnp.repeat(ssd_d, SSD_HEADDIM)[None, :]
    return [gla_norm_g[None, :], d_skip, ssd_norm_g[None, :], w_up_gla.astype(BF16), w_up_ssd.astype(BF16),
            w_out.astype(BF16), norm_ffn_g[None, :], jnp.stack(_split_bf16(w_router, 2)), b_router[None, :]]


def _merge(x2, row0, o_f, o_b, proj, y_f, y_b, xs_c, params, tm):
    t = o_f.shape[0]
    blk0 = row0 // tm
    rowblk = lambda w, col=0: pl.BlockSpec((tm, w), lambda i: (i, col))
    const = lambda a: pl.BlockSpec(a.shape, lambda i: (0,) * a.ndim, pipeline_mode=pl.Buffered(1))
    return pl.pallas_call(
        _merge_kernel,
        grid=(t // tm,),
        in_specs=[pl.BlockSpec((tm, D_MODEL), lambda i: (blk0 + i, 0)),
                  rowblk(GLA_V), rowblk(GLA_V), rowblk(GLA_V, C_R // GLA_V),
                  rowblk(D_MODEL, C_GG // D_MODEL), rowblk(SSD_DINNER), rowblk(SSD_DINNER), rowblk(SSD_DINNER),
                  rowblk(SSD_DINNER, C_Z // SSD_DINNER), rowblk(D_MODEL, C_GS // D_MODEL)]
        + [const(p) for p in params],
        out_specs=[rowblk(D_MODEL), pl.BlockSpec((2, tm, SC_W), lambda i: (0, i, 0)), rowblk(128), rowblk(128),
                   pl.BlockSpec((8, N_EXPERTS), lambda i: (0, 0))],
        out_shape=[jax.ShapeDtypeStruct((t, D_MODEL), F32), jax.ShapeDtypeStruct((2, t, SC_W), U32),
                   jax.ShapeDtypeStruct((t, 128), I32), jax.ShapeDtypeStruct((t, 128), F32),
                   jax.ShapeDtypeStruct((8, N_EXPERTS), I32)],
        scratch_shapes=[pltpu.VMEM((8, N_EXPERTS), F32)],
        compiler_params=pltpu.CompilerParams(dimension_semantics=("arbitrary",), vmem_limit_bytes=VMEM_LIMIT),
        name="merge_router",
    )(x2, o_f, o_b, proj, proj, y_f, y_b, xs_c, proj, proj, *params)


def _sc_mesh():
    return plsc.VectorSubcoreMesh(core_axis_name="c", subcore_axis_name="s")


def _sc_scatter_rows(x, idx, n_out):
    n, m = x.shape[0], idx.shape[1]
    n_win = n // SC_WIN

    @functools.partial(pl.kernel, out_type=jax.ShapeDtypeStruct((n_out, SC_W), x.dtype), mesh=_sc_mesh())
    def scatter(x_hbm, i_hbm, o_hbm):
        def body(x_vmem, i_vmem):
            pltpu.sync_copy(x_vmem, o_hbm.at[i_vmem.at[0]])

        pltpu.emit_pipeline(
            body, grid=(m // SC_WIN,),
            in_specs=[pl.BlockSpec((SC_WIN, SC_W), lambda i: (i % n_win, 0)),
                      pl.BlockSpec((1, SC_WIN), lambda i: (0, i))],
            out_specs=[], core_axis_name=("c", "s"), dimension_semantics=(pltpu.PARALLEL,),
        )(x_hbm, i_hbm)

    return scatter(x, idx)


def _sc_gather_rows(table, idx):
    m = idx.shape[1]

    @functools.partial(pl.kernel, out_type=jax.ShapeDtypeStruct((m, SC_W), table.dtype), mesh=_sc_mesh())
    def gather(t_hbm, i_hbm, o_hbm):
        def body(i_vmem, o_vmem):
            pltpu.sync_copy(t_hbm.at[i_vmem.at[0]], o_vmem)

        pltpu.emit_pipeline(
            body, grid=(m // SC_WIN,),
            in_specs=[pl.BlockSpec((1, SC_WIN), lambda i: (0, i))],
            out_specs=[pl.BlockSpec((SC_WIN, SC_W), lambda i: (i, 0))],
            core_axis_name=("c", "s"), dimension_semantics=(pltpu.PARALLEL,),
        )(i_hbm, o_hbm)

    return gather(table, idx)


def _expert_kernel(blk_e_ref, nvalid_ref, nused_ref, x_ref, w1_ref, b1_ref, w2_ref, b2_ref, o_ref, w1_s, w2_s):
    del nused_ref
    i = pl.program_id(0)
    nvalid = nvalid_ref[i]

    @pl.when((nvalid > 0) & ((i == 0) | (blk_e_ref[i] != blk_e_ref[jnp.maximum(i - 1, 0)])))
    def _():
        w1_s[...] = w1_ref[0].astype(BF16)
        w2_s[...] = w2_ref[0].astype(BF16)

    @pl.when(nvalid > 0)
    def _():
        live = lax.broadcasted_iota(I32, (MOE_BM, SC_W), 0) < nvalid
        lo0, hi0 = _unpack_bf16_pair(jnp.where(live, x_ref[0], jnp.uint32(0)))
        lo1, hi1 = _unpack_bf16_pair(jnp.where(live, x_ref[1], jnp.uint32(0)))
        x = jnp.concatenate([lo0, lo1, hi0, hi1], axis=1).astype(BF16)
        hdn = _dot(x, w1_s[...]) + b1_ref[0]
        gate = jnp.minimum(hdn[:, :D_FF], SWIGLU_LIMIT)
        lin = jnp.clip(hdn[:, D_FF:], -SWIGLU_LIMIT, SWIGLU_LIMIT)
        act = gate * _sigmoid(SWIGLU_ALPHA * gate) * (lin + 1.0)
        y = _dot(act.astype(BF16), w2_s[...]) + b2_ref[0]
        words = _pack_bf16_pair(y[:, :D_MODEL // 2], y[:, D_MODEL // 2:])
        o_ref[0] = words[:, :SC_W]
        o_ref[1] = words[:, SC_W:]

    @pl.when(nvalid == 0)
    def _():
        o_ref[...] = jnp.zeros_like(o_ref)


def _experts(xb, blk_e, nvalid, n_used, w1, b1, w2, b2):
    n_rows = xb.shape[1]
    n_blocks = n_rows // MOE_BM
    xidx = lambda i, be, nv, nu: (0, jnp.minimum(i, nu[0] - 1), 0)
    eidx = lambda i, be, nv, nu: (be[i], 0, 0)
    return pl.pallas_call(
        _expert_kernel,
        grid_spec=pltpu.PrefetchScalarGridSpec(
            num_scalar_prefetch=3,
            grid=(n_blocks,),
            in_specs=[pl.BlockSpec((2, MOE_BM, SC_W), xidx),
                      pl.BlockSpec((1, D_MODEL, 2 * D_FF), eidx),
                      pl.BlockSpec((1, 1, 2 * D_FF), eidx),
                      pl.BlockSpec((1, D_FF, D_MODEL), eidx),
                      pl.BlockSpec((1, 1, D_MODEL), eidx)],
            out_specs=pl.BlockSpec((2, MOE_BM, SC_W), lambda i, be, nv, nu: (0, i, 0)),
            scratch_shapes=[pltpu.VMEM((D_MODEL, 2 * D_FF), BF16), pltpu.VMEM((D_FF, D_MODEL), BF16)],
        ),
        out_shape=jax.ShapeDtypeStruct((2, n_rows, SC_W), U32),
        compiler_params=pltpu.CompilerParams(dimension_semantics=("arbitrary",), vmem_limit_bytes=VMEM_LIMIT),
        name="moe_experts",
    )(blk_e, nvalid, n_used, xb, w1, b1, w2, b2)


def _combine_kernel(h_ref, gate_ref, y_ref, g_ref, *rest):
    o_ref = rest[-1]
    gates = gate_ref[...]
    acc = None
    for k in range(TOP_K):
        lo0, hi0 = _unpack_bf16_pair(y_ref[k, 0])
        lo1, hi1 = _unpack_bf16_pair(y_ref[k, 1])
        term = gates[:, k:k + 1] * jnp.concatenate([lo0, lo1, hi0, hi1], axis=1)
        acc = term if acc is None else acc + term
    h = h_ref[...] + acc
    o_ref[...] = (h * lax.rsqrt(jnp.mean(h * h, axis=-1, keepdims=True) + EPS)) * g_ref[...]


def _combine(h, gates, y_rows, norm_final_g, t_total, row0, prev, tm):
    t = h.shape[0]
    blk0 = row0 // tm
    in_specs = [pl.BlockSpec((tm, D_MODEL), lambda i: (i, 0)),
                pl.BlockSpec((tm, 128), lambda i: (i, 0)),
                pl.BlockSpec((TOP_K, 2, tm, SC_W), lambda i: (0, 0, i, 0)),
                pl.BlockSpec((1, D_MODEL), lambda i: (0, 0))]
    args = [h, gates, y_rows, norm_final_g[None, :]]
    if prev is not None:
        in_specs.append(pl.BlockSpec(memory_space=pl.ANY))
        args.append(prev)
    return pl.pallas_call(
        _combine_kernel,
        grid=(t // tm,),
        in_specs=in_specs,
        out_specs=pl.BlockSpec((tm, D_MODEL), lambda i: (blk0 + i, 0)),
        out_shape=jax.ShapeDtypeStruct((t_total, D_MODEL), F32),
        input_output_aliases={} if prev is None else {4: 0},
        compiler_params=pltpu.CompilerParams(dimension_semantics=("arbitrary",), vmem_limit_bytes=VMEM_LIMIT),
        name="moe_combine",
    )(*args)


def _pick(n, pref):
    b = min(pref, n)
    while n % b:
        b -= CHUNK
    return b


def _prep_w_in(w_in):
    widths = (GLA_QK, GLA_QK, GLA_V, GLA_V, GLA_RANK, GLA_RANK, SSD_DINNER, SSD_DINNER, SSD_BC, SSD_BC,
              2 * SSD_HEADS, D_MODEL, D_MODEL)
    pts, acc = [], 0
    for w in widths[:-1]:
        acc += w
        pts.append(acc)
    q, k, v, r, lrf, lrb, z, xs, bm, cm, dtr, gg, gs = jnp.split(w_in, pts, axis=1)
    main = jnp.concatenate([z, xs, q, k, v, r, gg, gs, bm, cm], axis=1).astype(BF16)
    pad = jnp.zeros((D_MODEL, N_SMALL - 2 * GLA_RANK - 2 * SSD_HEADS), w_in.dtype)
    small = jnp.concatenate([lrf, lrb, dtr, pad], axis=1).astype(BF16)
    return main, small


def _layer(h, norm_mix_g, w_in, gla_fw2_f, gla_fb_f, gla_fw2_b, gla_fb_b, gla_norm_g, conv_w, conv_b,
           dt_bias_f, dt_bias_b, a_log_f, a_log_b, ssd_d, ssd_norm_g, w_up_gla, w_up_ssd, w_out,
           norm_ffn_g, w_router, b_router, w1, b1, w2, b2, out_norm_g):
    bsz, seq, _ = h.shape
    t_total = bsz * seq
    x2 = h.reshape(t_total, D_MODEL)
    w_main, w_small = _prep_w_in(w_in)
    fw2f, fw2b = gla_fw2_f.astype(BF16), gla_fw2_b.astype(BF16)
    mparams = _merge_params(gla_norm_g, ssd_d, ssd_norm_g, w_up_gla, w_up_ssd, w_out, norm_ffn_g, w_router, b_router)
    n_groups = TOKEN_GROUPS if bsz % TOKEN_GROUPS == 0 else 1
    gb = bsz // n_groups
    t = gb * seq
    lb = _pick(seq, 512)
    out = None
    for grp in range(n_groups):
        row0 = grp * t
        proj, small = _inproj(x2, norm_mix_g[None, :], w_main, w_small, row0, t, tm=_pick(t, 2048), tn=1024)
        o_f, o_b = _gla(proj, small, fw2f, gla_fb_f[None, :], fw2b, gla_fb_b[None, :], gb, seq, lb)
        xs_c, bm_c, cm_c = _conv(proj, conv_w, conv_b, seq, _pick(seq, 512))
        y_f, y_b = _ssd(xs_c, bm_c, cm_c, small, dt_bias_f, dt_bias_b, a_log_f, a_log_b, gb, seq, lb)
        hres, xp, route, gates, counts8 = _merge(x2, row0, o_f, o_b, proj, y_f, y_b, xs_c, mparams, tm=_pick(t, 512))
        counts = counts8[0]
        padded = (counts + MOE_BM - 1) // MOE_BM * MOE_BM
        pend = jnp.cumsum(padded)
        pstart = (pend - padded).astype(I32)
        n_rows = t * TOP_K + N_EXPERTS * MOE_BM
        blk_row = jnp.arange(n_rows // MOE_BM, dtype=I32) * MOE_BM
        blk_e = jnp.minimum(jnp.sum(pend[None, :] <= blk_row[:, None], axis=1), N_EXPERTS - 1).astype(I32)
        nvalid = jnp.clip(pstart[blk_e] + counts[blk_e] - blk_row, 0, MOE_BM).astype(I32)
        n_used = (pend[-1:] // MOE_BM).astype(I32)
        top_e, rank = route[:, :TOP_K], route[:, TOP_K:2 * TOP_K]
        dest = jnp.sum(jnp.where(top_e[:, :, None] == jnp.arange(N_EXPERTS, dtype=I32), pstart, 0), axis=-1) + rank
        idx = (dest.T[:, None, :] + (jnp.arange(2, dtype=I32) * n_rows)[None, :, None]).reshape(1, 2 * TOP_K * t)
        xb = _sc_scatter_rows(xp.reshape(2 * t, SC_W), idx, 2 * n_rows).reshape(2, n_rows, SC_W)
        yb = _experts(xb, blk_e, nvalid, n_used, w1, b1[:, None, :], w2, b2[:, None, :])
        y_rows = _sc_gather_rows(yb.reshape(2 * n_rows, SC_W), idx).reshape(TOP_K, 2, t, SC_W)
        out = _combine(hres, gates, y_rows, out_norm_g, t_total, row0, out, tm=_pick(t, 512))
    return out.reshape(bsz, seq, D_MODEL)


def kernel(x, norm_mix_g, w_in, gla_fw2_f, gla_fb_f, gla_fw2_b, gla_fb_b, gla_norm_g, conv_w, conv_b, dt_bias_f,
           dt_bias_b, a_log_f, a_log_b, ssd_d, ssd_norm_g, w_up_gla, w_up_ssd, w_out, norm_ffn_g, w_router,
           b_router, w1, b1, w2, b2, norm_final_g):
    assert x.shape[-1] == D_MODEL and norm_mix_g.shape[0] == 1
    return _layer(x, norm_mix_g[0], w_in[0], gla_fw2_f[0], gla_fb_f[0], gla_fw2_b[0], gla_fb_b[0], gla_norm_g[0],
                  conv_w[0], conv_b[0], dt_bias_f[0], dt_bias_b[0], a_log_f[0], a_log_b[0], ssd_d[0],
                  ssd_norm_g[0], w_up_gla[0], w_up_ssd[0], w_out[0], norm_ffn_g[0], w_router[0], b_router[0],
                  w1[0], b1[0], w2[0], b2[0], norm_final_g)
```

```python
import functools
from typing import NamedTuple

import jax
import jax.numpy as jnp
from jax import lax
from jax.experimental import pallas as pl
from jax.experimental.pallas import tpu as pltpu
from jax.experimental.pallas import tpu_sc as plsc

F32 = jnp.float32
BF16 = jnp.bfloat16
I32 = jnp.int32
U32 = jnp.uint32

D_MODEL = 1024
EPS = 1e-5
GLA_HEADS = 4
GLA_DK = 128
GLA_DV = 256
GLA_RANK = 16
GLA_TAU = 16.0
GLA_QK = GLA_HEADS * GLA_DK
GLA_V = GLA_HEADS * GLA_DV
SSD_DINNER = 2048
SSD_HEADDIM = 64
SSD_HEADS = 32
SSD_GROUPS = 4
SSD_HPG = 8
SSD_STATE = 128
SSD_CONV = 4
SSD_BC = SSD_GROUPS * SSD_STATE
SSD_GW = SSD_HPG * SSD_HEADDIM
N_EXPERTS = 32
TOP_K = 4
D_FF = 1024
SWIGLU_LIMIT = 7.0
SWIGLU_ALPHA = 1.702
CHUNK = 64
TOKEN_GROUPS = 2
MERGE_SUBTILES = 2
SSD_INNER = 2
MXU_LAG = 6

C_Z, C_XS, C_Q, C_K, C_V, C_R, C_GG, C_GS, C_B, C_C = 0, 2048, 4096, 4608, 5120, 6144, 7168, 8192, 9216, 9728
N_MAIN = 10240
N_SMALL = 128
S_LRF, S_LRB, S_DTF, S_DTB = 0, 16, 32, 64

VMEM_LIMIT = 56 * 1024 * 1024
MOE_BM = 512
SC_WIN = 128
SC_W = D_MODEL // 4


def _dot(a, b):
    return jnp.dot(a, b, preferred_element_type=F32)


def _dot_nt(a, b):
    return lax.dot_general(a, b, (((1,), (1,)), ((), ())), preferred_element_type=F32)


def _dot_tn(a, b):
    return lax.dot_general(a, b, (((0,), (0,)), ((), ())), preferred_element_type=F32)


def _split_bf16(x, n):
    parts = []
    r = x
    for _ in range(n):
        p = r.astype(BF16)
        parts.append(p)
        r = r - p.astype(F32)
    return parts


def _dot_exact_lhs(m_bf16, x, n):
    acc = None
    for p in _split_bf16(x, n):
        t = _dot(m_bf16, p)
        acc = t if acc is None else acc + t
    return acc


def _dot_exact_rhs(x, m_bf16, n):
    acc = None
    for p in _split_bf16(x, n):
        t = _dot(p, m_bf16)
        acc = t if acc is None else acc + t
    return acc


def _sigmoid(x):
    return 1.0 / (1.0 + jnp.exp2(x * (-1.4426950408889634)))


def _silu(x):
    return x * _sigmoid(x)


def _pack_bf16_pair(lo, hi):
    lo_b = lax.bitcast_convert_type(lo.astype(BF16).astype(F32), U32)
    hi_b = lax.bitcast_convert_type(hi.astype(BF16).astype(F32), U32)
    return (lo_b >> 16) | (hi_b & jnp.uint32(0xFFFF0000))


def _unpack_bf16_pair(u):
    lo = lax.bitcast_convert_type(u << 16, F32)
    hi = lax.bitcast_convert_type(u & jnp.uint32(0xFFFF0000), F32)
    return lo, hi


def _inproj_kernel(x_ref, g_ref, w_ref, ws_ref, o_ref, os_ref, xn_ref):
    @pl.when(pl.program_id(1) == 0)
    def _():
        x = x_ref[...]
        ms = jnp.mean(x * x, axis=-1, keepdims=True)
        xb = ((x * lax.rsqrt(ms + EPS)) * g_ref[...]).astype(BF16)
        xn_ref[...] = xb
        os_ref[...] = _dot(xb, ws_ref[...])

    o_ref[...] = _dot(xn_ref[...], w_ref[...]).astype(BF16)


def _inproj(x2, g, w_main, w_small, row0, t, tm, tn):
    blk0 = row0 // tm
    return pl.pallas_call(
        _inproj_kernel,
        grid=(t // tm, N_MAIN // tn),
        in_specs=[
            pl.BlockSpec((tm, D_MODEL), lambda i, j: (blk0 + i, 0)),
            pl.BlockSpec((1, D_MODEL), lambda i, j: (0, 0)),
            pl.BlockSpec((D_MODEL, tn), lambda i, j: (0, j)),
            pl.BlockSpec((D_MODEL, N_SMALL), lambda i, j: (0, 0)),
        ],
        out_specs=[
            pl.BlockSpec((tm, tn), lambda i, j: (i, j)),
            pl.BlockSpec((tm, N_SMALL), lambda i, j: (i, 0)),
        ],
        out_shape=[
            jax.ShapeDtypeStruct((t, N_MAIN), BF16),
            jax.ShapeDtypeStruct((t, N_SMALL), F32),
        ],
        scratch_shapes=[pltpu.VMEM((tm, D_MODEL), BF16)],
        compiler_params=pltpu.CompilerParams(
            dimension_semantics=("arbitrary", "arbitrary"), vmem_limit_bytes=VMEM_LIMIT),
        name="inproj",
    )(x2, g, w_main, w_small)


def _tri_masks(n):
    r = lax.broadcasted_iota(I32, (n, n), 0)
    c = lax.broadcasted_iota(I32, (n, n), 1)
    return r >= c, c >= r


class _GlaDir:
    def __init__(self, ins, outs, scratch, mask, lr_off, mid_row, last_row):
        self.q, self.k, self.v, self.sm, self.fw2, self.fb = ins
        self.o, self.st = outs
        self.b_s, self.qs_s, self.ks_s, self.kd_s, self.qe_s, self.p_s, self.u_s, self.el_s = scratch
        self.mask, self.lr_off, self.mid_row, self.last_row = mask, lr_off, mid_row, last_row


def _gla_decay(d, n_chunks):
    tri = jnp.where(d.mask, 1.0, 0.0).astype(BF16)
    lr = d.sm[:, d.lr_off:d.lr_off + GLA_RANK].astype(BF16)
    xg = _dot(lr, d.fw2[...]) + d.fb[...]
    d.b_s[...] = (jnp.minimum(xg, 0.0) - jnp.log(1.0 + jnp.exp(-jnp.abs(xg)))) * (1.0 / GLA_TAU)
    for c in range(n_chunks):
        rows = slice(c * CHUNK, (c + 1) * CHUNK)
        d.b_s[rows, :] = _dot_exact_lhs(tri, d.b_s[rows, :], 2)


def _gla_scale(d, c):
    rows = slice(c * CHUNK, (c + 1) * CHUNK)
    b = d.b_s[rows, :]
    b_mid = b[d.mid_row:d.mid_row + 1, :]
    b_last = b[d.last_row:d.last_row + 1, :]
    q = d.q[rows, :].astype(F32) * (GLA_DK ** -0.5)
    k = d.k[rows, :].astype(F32)
    d.qs_s[rows, :] = (q * jnp.exp(b - b_mid)).astype(BF16)
    d.ks_s[rows, :] = (k * jnp.exp(b_mid - b)).astype(BF16)
    d.kd_s[rows, :] = (k * jnp.exp(b_last - b)).astype(BF16)
    d.qe_s[rows, :] = (q * jnp.exp(b)).astype(BF16)
    d.el_s[c:c + 1, :] = jnp.exp(b_last)


def _gla_local(units):
    def score(u):
        d, c, h = u
        rows, ks_ = slice(c * CHUNK, (c + 1) * CHUNK), slice(h * GLA_DK, (h + 1) * GLA_DK)
        return _dot_nt(d.qs_s[rows, ks_], d.ks_s[rows, ks_])

    def finish(u, s):
        d, c, h = u
        rows, ks_ = slice(c * CHUNK, (c + 1) * CHUNK), slice(h * GLA_DK, (h + 1) * GLA_DK)
        vs_ = slice(h * GLA_DV, (h + 1) * GLA_DV)
        d.p_s[c, h] = jnp.where(d.mask, s, 0.0).astype(BF16)
        d.u_s[c, h] = _dot_tn(d.kd_s[rows, ks_], d.v[rows, vs_])

    pending = []
    for u in units:
        pending.append((u, score(u)))
        if len(pending) > MXU_LAG:
            finish(*pending.pop(0))
    for item in pending:
        finish(*item)


def _gla_carry(d, c):
    rows = slice(c * CHUNK, (c + 1) * CHUNK)
    for h in range(GLA_HEADS):
        ks_ = slice(h * GLA_DK, (h + 1) * GLA_DK)
        vs_ = slice(h * GLA_DV, (h + 1) * GLA_DV)
        st = d.st[h]
        o = _dot(d.p_s[c, h], d.v[rows, vs_]) + _dot(d.qe_s[rows, ks_], st.astype(BF16))
        d.o[rows, vs_] = o.astype(d.o.dtype)
        e_col = jnp.transpose(jnp.broadcast_to(d.el_s[c:c + 1, ks_], (8, GLA_DK)))[:, 0:1]
        d.st[h] = st * e_col + d.u_s[c, h]


def _gla_kernel(*refs, n_chunks):
    ins_f, ins_b, (fw2f, fbf, fw2b, fbb), (of_ref, ob_ref, stf, stb) = refs[0:4], refs[4:8], refs[8:12], refs[12:16]
    scr_f, scr_b = refs[16:24], refs[24:32]

    @pl.when(pl.program_id(1) == 0)
    def _():
        stf[...] = jnp.zeros_like(stf)
        stb[...] = jnp.zeros_like(stb)

    lower, upper = _tri_masks(CHUNK)
    fwd = _GlaDir((*ins_f, fw2f, fbf), (of_ref, stf), scr_f, lower, S_LRF, CHUNK // 2, CHUNK - 1)
    bwd = _GlaDir((*ins_b, fw2b, fbb), (ob_ref, stb), scr_b, upper, S_LRB, CHUNK // 2 - 1, 0)
    _gla_decay(fwd, n_chunks)
    _gla_decay(bwd, n_chunks)
    for c in range(n_chunks):
        _gla_scale(fwd, c)
        _gla_scale(bwd, c)
    _gla_local([(d, c, h) for c in range(n_chunks) for d in (fwd, bwd) for h in range(GLA_HEADS)])
    for i in range(n_chunks):
        _gla_carry(fwd, i)
        _gla_carry(bwd, n_chunks - 1 - i)


def _gla(proj, small, fw2f, fbf, fw2b, fbb, bsz, seq, lb):
    t = bsz * seq
    nb = seq // lb

    def fwd(w, col):
        return pl.BlockSpec((lb, w), lambda b, n: (b * nb + n, col))

    def bwd(w, col):
        return pl.BlockSpec((lb, w), lambda b, n: (b * nb + nb - 1 - n, col))

    const = lambda shape: pl.BlockSpec(shape, lambda b, n: (0, 0))
    per_dir_scratch = ([pltpu.VMEM((lb, GLA_QK), F32)] + [pltpu.VMEM((lb, GLA_QK), BF16)] * 4
                       + [pltpu.VMEM((lb // CHUNK, GLA_HEADS, CHUNK, CHUNK), BF16),
                          pltpu.VMEM((lb // CHUNK, GLA_HEADS, GLA_DK, GLA_DV), F32),
                          pltpu.VMEM((lb // CHUNK, GLA_QK), F32)])
    return pl.pallas_call(
        functools.partial(_gla_kernel, n_chunks=lb // CHUNK),
        grid=(bsz, nb),
        in_specs=[
            fwd(GLA_QK, C_Q // GLA_QK), fwd(GLA_QK, C_K // GLA_QK), fwd(GLA_V, C_V // GLA_V), fwd(N_SMALL, 0),
            bwd(GLA_QK, C_Q // GLA_QK), bwd(GLA_QK, C_K // GLA_QK), bwd(GLA_V, C_V // GLA_V), bwd(N_SMALL, 0),
            const((GLA_RANK, GLA_QK)), const((1, GLA_QK)), const((GLA_RANK, GLA_QK)), const((1, GLA_QK)),
        ],
        out_specs=[fwd(GLA_V, 0), bwd(GLA_V, 0)],
        out_shape=[jax.ShapeDtypeStruct((t, GLA_V), BF16)] * 2,
        scratch_shapes=[pltpu.VMEM((GLA_HEADS, GLA_DK, GLA_DV), F32)] * 2 + per_dir_scratch * 2,
        compiler_params=pltpu.CompilerParams(
            dimension_semantics=("arbitrary", "arbitrary"), vmem_limit_bytes=VMEM_LIMIT),
        name="gla_scan",
    )(proj, proj, proj, small, proj, proj, proj, small, fw2f, fbf, fw2b, fbb)


HALO = 16


CONV_SUB = 256
CONV_COLS = 512


def _conv_taps(x, xm1, xp1, xp2, w, b):
    return _silu(xm1 * w[0:1, :] + x * w[1:2, :] + xp1 * w[2:3, :] + xp2 * w[3:4, :] + b)


def _shift_matrix(n):
    r = lax.broadcasted_iota(I32, (3 * n, n), 0)
    c = lax.broadcasted_iota(I32, (3 * n, n), 1)
    src = jnp.where(r < n, r - 1, jnp.where(r < 2 * n, r - n + 1, r - 2 * n + 2))
    return jnp.where(c == src, 1.0, 0.0).astype(BF16)


def _conv_one(x_ref, p_ref, n_ref, w_ref, b_ref, o_ref, shift, has_prev, has_next):
    rb, wd = x_ref.shape
    w, b = w_ref[...], b_ref[...]
    for s in range(rb // CONV_SUB):
        rows = slice(s * CONV_SUB, (s + 1) * CONV_SUB)
        for c0 in range(0, wd, CONV_COLS):
            cols = slice(c0, c0 + CONV_COLS)
            xb = x_ref[rows, cols]
            sx = _dot(shift, xb)
            y = _conv_taps(xb.astype(F32), sx[0:CONV_SUB], sx[CONV_SUB:2 * CONV_SUB], sx[2 * CONV_SUB:],
                           w[:, cols], b[:, cols])
            o_ref[rows, cols] = y.astype(o_ref.dtype)
    prev = jnp.where(has_prev, p_ref[HALO - 1:HALO, :].astype(F32), 0.0)
    nxt = jnp.where(has_next, n_ref[0:2, :].astype(F32), 0.0)
    row = lax.broadcasted_iota(I32, (HALO, wd), 0)
    head = x_ref[0:2 * HALO, :].astype(F32)
    xm1 = jnp.where(row == 0, prev, pltpu.roll(head, 1, 0)[0:HALO])
    o_ref[0:HALO, :] = _conv_taps(head[0:HALO], xm1, pltpu.roll(head, 2 * HALO - 1, 0)[0:HALO],
                                  pltpu.roll(head, 2 * HALO - 2, 0)[0:HALO], w, b).astype(o_ref.dtype)
    tail = x_ref[rb - 2 * HALO:rb, :].astype(F32)
    xp1 = jnp.where(row == HALO - 1, nxt[0:1, :], pltpu.roll(tail, 2 * HALO - 1, 0)[HALO:])
    xp2 = jnp.where(row == HALO - 2, nxt[0:1, :],
                    jnp.where(row == HALO - 1, nxt[1:2, :], pltpu.roll(tail, 2 * HALO - 2, 0)[HALO:]))
    o_ref[rb - HALO:rb, :] = _conv_taps(tail[HALO:], pltpu.roll(tail, 1, 0)[HALO:], xp1, xp2, w, b).astype(o_ref.dtype)
    for s in range(1, rb // CONV_SUB):
        e = s * CONV_SUB
        win = x_ref[e - 2 * HALO:e + 2 * HALO, :].astype(F32)
        mid = slice(HALO, 3 * HALO)
        o_ref[e - HALO:e + HALO, :] = _conv_taps(
            win[mid], pltpu.roll(win, 1, 0)[mid], pltpu.roll(win, 4 * HALO - 1, 0)[mid],
            pltpu.roll(win, 4 * HALO - 2, 0)[mid], w, b).astype(o_ref.dtype)


def _conv_kernel(xs, xsp, xsn, bm, bmp, bmn, cm, cmp_, cmn, wx, bx, wb, bb, wc, bc, oxs, obm, ocm, *, rb, seq):
    t0 = pl.program_id(0) * rb
    has_prev = (t0 % seq) != 0
    has_next = ((t0 + rb) % seq) != 0
    shift = _shift_matrix(CONV_SUB)
    _conv_one(xs, xsp, xsn, wx, bx, oxs, shift, has_prev, has_next)
    _conv_one(bm, bmp, bmn, wb, bb, obm, shift, has_prev, has_next)
    _conv_one(cm, cmp_, cmn, wc, bc, ocm, shift, has_prev, has_next)


def _conv(proj, conv_w, conv_b, seq, rb):
    t = proj.shape[0]
    nh = t // HALO
    per = rb // HALO

    def trio(w, col):
        cb = col // w
        return [
            pl.BlockSpec((rb, w), lambda i: (i, cb)),
            pl.BlockSpec((HALO, w), lambda i: (jnp.maximum(i * per - 1, 0), cb)),
            pl.BlockSpec((HALO, w), lambda i: (jnp.minimum((i + 1) * per, nh - 1), cb)),
        ]

    def wspecs(w):
        return [pl.BlockSpec((SSD_CONV, w), lambda i: (0, 0)), pl.BlockSpec((1, w), lambda i: (0, 0))]

    wx, wb, wc = conv_w[:, :SSD_DINNER], conv_w[:, SSD_DINNER:SSD_DINNER + SSD_BC], conv_w[:, SSD_DINNER + SSD_BC:]
    bx, bb, bc = (conv_b[None, :SSD_DINNER], conv_b[None, SSD_DINNER:SSD_DINNER + SSD_BC],
                  conv_b[None, SSD_DINNER + SSD_BC:])
    return pl.pallas_call(
        functools.partial(_conv_kernel, rb=rb, seq=seq),
        grid=(t // rb,),
        in_specs=trio(SSD_DINNER, C_XS) + trio(SSD_BC, C_B) + trio(SSD_BC, C_C)
        + wspecs(SSD_DINNER) + wspecs(SSD_BC) + wspecs(SSD_BC),
        out_specs=[pl.BlockSpec((rb, SSD_DINNER), lambda i: (i, 0)),
                   pl.BlockSpec((rb, SSD_BC), lambda i: (i, 0)),
                   pl.BlockSpec((rb, SSD_BC), lambda i: (i, 0))],
        out_shape=[jax.ShapeDtypeStruct((t, SSD_DINNER), BF16),
                   jax.ShapeDtypeStruct((t, SSD_BC), BF16),
                   jax.ShapeDtypeStruct((t, SSD_BC), BF16)],
        compiler_params=pltpu.CompilerParams(dimension_semantics=("arbitrary",), vmem_limit_bytes=VMEM_LIMIT),
        name="ssd_conv",
    )(proj, proj, proj, proj, proj, proj, proj, proj, proj, wx, bx, wb, bb, wc, bc)


def _softplus(x):
    return jnp.maximum(x, 0.0) + jnp.log(1.0 + jnp.exp(-jnp.abs(x)))


class _SsdDir:
    def __init__(self, xs, bm, cm, sm, dtb_row, alog_row, y, st, reverse, dt_off, last_row):
        self.xs, self.bm, self.cm, self.sm, self.dtb_row, self.alog_row = xs, bm, cm, sm, dtb_row, alog_row
        self.y, self.st, self.reverse, self.dt_off, self.last_row = y, st, reverse, dt_off, last_row


def _ssd_chunks(work):
    hh = lax.broadcasted_iota(I32, (SSD_HEADS, SSD_GW), 0)
    cc = lax.broadcasted_iota(I32, (SSD_HEADS, SSD_GW), 1)
    lane = lax.broadcasted_iota(I32, (CHUNK, 2 * SSD_HEADDIM), 1)
    row2 = lax.broadcasted_iota(I32, (CHUNK, 2 * SSD_HEADDIM), 0)
    left = lane < SSD_HEADDIM
    col2 = jnp.where(left, lane, lane - SSD_HEADDIM)
    lower, upper = _tri_masks(CHUNK)

    pre = []
    for d, c0 in work:
        rows = pl.ds(c0, CHUNK)
        tri = jnp.where(upper if d.reverse else lower, 1.0, 0.0).astype(BF16)
        a_row = -jnp.exp(d.alog_row[...])
        dt = _softplus(d.sm[rows, d.dt_off:d.dt_off + SSD_HEADS] + d.dtb_row[...])
        pre.append((dt, _dot_exact_lhs(tri, dt * a_row, 3)))
    heads = []
    for (d, c0), (dt, cum) in zip(work, pre):
        total = cum[d.last_row:d.last_row + 1, :]
        to_end = jnp.exp(total - cum) * dt
        fac = jnp.concatenate([to_end, jnp.exp(cum)], axis=0).astype(BF16)
        e_tot = jnp.broadcast_to(jnp.exp(total), (8, SSD_HEADS))
        heads.append((cum, jnp.transpose(cum), jnp.transpose(dt), fac, e_tot))

    units = [(w, g) for w in range(len(work)) for g in range(SSD_GROUPS)]
    groups = {}
    for w, g in units:
        d, c0 = work[w]
        rows, ns = pl.ds(c0, CHUNK), slice(g * SSD_STATE, (g + 1) * SSD_STATE)
        fac, e_tot = heads[w][3], heads[w][4]
        expand = jnp.where(cc // SSD_HEADDIM + g * SSD_HPG == hh, 1.0, 0.0).astype(BF16)
        fac_x = _dot(fac, expand)
        et_x = _dot_exact_rhs(e_tot, expand, 2)[0:1]
        groups[w, g] = (fac_x, et_x, _dot_nt(d.cm[rows, ns], d.bm[rows, ns]))

    for w, g in units:
        d, c0 = work[w]
        rows, ns, cs = pl.ds(c0, CHUNK), slice(g * SSD_STATE, (g + 1) * SSD_STATE), slice(g * SSD_GW, (g + 1) * SSD_GW)
        cum, cum_t, dt_t = heads[w][0:3]
        fac_x, _, cb = groups[w, g]
        mask2 = (col2 >= row2) if d.reverse else (row2 >= col2)
        y_inter = _dot(d.cm[rows, ns], d.st[g].astype(BF16))
        cb2 = jnp.concatenate([cb, cb], axis=1)
        parts = []
        for p in range(SSD_HPG // 2):
            h0 = g * SSD_HPG + 2 * p
            ps = slice(h0 * SSD_HEADDIM, (h0 + 2) * SSD_HEADDIM)
            col = jnp.where(left, cum[:, h0:h0 + 1], cum[:, h0 + 1:h0 + 2])
            rowv = jnp.concatenate([cum_t[h0:h0 + 1, :], cum_t[h0 + 1:h0 + 2, :]], axis=1)
            dtv = jnp.concatenate([dt_t[h0:h0 + 1, :], dt_t[h0 + 1:h0 + 2, :]], axis=1)
            decay = jnp.exp(jnp.where(mask2, col - rowv, -jnp.inf))
            wgt = (cb2 * decay * dtv).astype(BF16)
            xp = d.xs[rows, ps]
            zero = jnp.zeros_like(xp)
            xbd = jnp.concatenate([jnp.where(left, xp, zero), jnp.where(left, zero, xp)], axis=0)
            parts.append(_dot(wgt, xbd))
        y = jnp.concatenate(parts, axis=1) + y_inter * fac_x[CHUNK:2 * CHUNK]
        d.y[rows, cs] = y.astype(d.y.dtype)

    def increment(w, g):
        d, c0 = work[w]
        rows, ns, cs = pl.ds(c0, CHUNK), slice(g * SSD_STATE, (g + 1) * SSD_STATE), slice(g * SSD_GW, (g + 1) * SSD_GW)
        return _dot_tn(d.bm[rows, ns], d.xs[rows, cs] * groups[w, g][0][0:CHUNK].astype(BF16))

    def update(w, g, inc):
        d = work[w][0]
        d.st[g] = d.st[g] * groups[w, g][1] + inc

    pending = None
    for w, g in units:
        inc = increment(w, g)
        if pending is not None:
            update(*pending)
        pending = (w, g, inc)
    update(*pending)


def _ssd_kernel(*refs, n_chunks, inner):
    ins, (dbf_r, dbb_r, alf_r, alb_r) = refs[:8 * inner], refs[8 * inner:8 * inner + 4]
    (yf_ref, yb_ref), states = refs[8 * inner + 4:8 * inner + 6], refs[8 * inner + 6:]

    @pl.when(pl.program_id(1) == 0)
    def _():
        for st in states:
            st[...] = jnp.zeros_like(st)

    fwd = [_SsdDir(*ins[8 * e:8 * e + 4], dbf_r, alf_r, yf_ref.at[e], states[2 * e], False, S_DTF, CHUNK - 1)
           for e in range(inner)]
    bwd = [_SsdDir(*ins[8 * e + 4:8 * e + 8], dbb_r, alb_r, yb_ref.at[e], states[2 * e + 1], True, S_DTB, 0)
           for e in range(inner)]

    def body(i, carry):
        c_f = pl.multiple_of(i * CHUNK, CHUNK)
        c_b = pl.multiple_of((n_chunks - 1 - i) * CHUNK, CHUNK)
        _ssd_chunks([(d, c_f) for d in fwd] + [(d, c_b) for d in bwd])
        return carry

    lax.fori_loop(0, n_chunks, body, 0)


def _ssd(xs_c, bm_c, cm_c, small, dtb_f, dtb_b, alog_f, alog_b, bsz, seq, lb):
    t = bsz * seq
    nb = seq // lb
    inner = SSD_INNER if bsz % SSD_INNER == 0 else 1

    def specs(e, reverse):
        idx = lambda p, n: (p * inner + e) * nb + (nb - 1 - n if reverse else n)
        return [pl.BlockSpec((lb, w), lambda p, n: (idx(p, n), 0)) for w in (SSD_DINNER, SSD_BC, SSD_BC, N_SMALL)]

    row = pl.BlockSpec((1, SSD_HEADS), lambda p, n: (0, 0))
    args = (xs_c, bm_c, cm_c, small)
    y_f, y_b = pl.pallas_call(
        functools.partial(_ssd_kernel, n_chunks=lb // CHUNK, inner=inner),
        grid=(bsz // inner, nb),
        in_specs=[s for e in range(inner) for rev in (False, True) for s in specs(e, rev)] + [row] * 4,
        out_specs=[pl.BlockSpec((None, inner, lb, SSD_DINNER), lambda p, n: (p, 0, n, 0)),
                   pl.BlockSpec((None, inner, lb, SSD_DINNER), lambda p, n: (p, 0, nb - 1 - n, 0))],
        out_shape=[jax.ShapeDtypeStruct((bsz // inner, inner, seq, SSD_DINNER), BF16)] * 2,
        scratch_shapes=[pltpu.VMEM((SSD_GROUPS, SSD_STATE, SSD_GW), F32)] * (2 * inner),
        compiler_params=pltpu.CompilerParams(
            dimension_semantics=("arbitrary", "arbitrary"), vmem_limit_bytes=VMEM_LIMIT),
        name="ssd_scan",
    )(*(args * (2 * inner)), dtb_f[None, :], dtb_b[None, :], alog_f[None, :], alog_b[None, :])
    return y_f.reshape(t, SSD_DINNER), y_b.reshape(t, SSD_DINNER)


def _merge_kernel(x_ref, of_ref, ob_ref, r_ref, gg_ref, yf_ref, yb_ref, xs_ref, z_ref, gs_ref,
                  gng_ref, dsk_ref, sng_ref, wug_ref, wus_ref, wo_ref, nfg_ref, wr_ref, br_ref,
                  h_ref, xp_ref, route_ref, gate_ref, cnt_out_ref, cnt_ref):
    @pl.when(pl.program_id(0) == 0)
    def _():
        cnt_ref[...] = jnp.zeros_like(cnt_ref)

    tm = x_ref.shape[0]
    subs = [slice(s * (tm // MERGE_SUBTILES), (s + 1) * (tm // MERGE_SUBTILES)) for s in range(MERGE_SUBTILES)]

    def gla_branch(rows):
        o = of_ref[rows, :].astype(F32) + ob_ref[rows, :].astype(F32)
        gng = gng_ref[...]
        o_parts = []
        for h in range(GLA_HEADS):
            oh = o[:, h * GLA_DV:(h + 1) * GLA_DV]
            oh = oh * lax.rsqrt(jnp.mean(oh * oh, axis=-1, keepdims=True) + EPS)
            o_parts.append(oh * gng)
        return (jnp.concatenate(o_parts, axis=1) * _silu(r_ref[rows, :]).astype(F32)).astype(BF16)

    def ssd_branch(rows):
        y = (yf_ref[rows, :].astype(F32) + yb_ref[rows, :].astype(F32)
             + dsk_ref[...] * xs_ref[rows, :].astype(F32))
        y = y * _silu(z_ref[rows, :]).astype(F32)
        sng = sng_ref[...]
        y_parts = []
        for g in range(SSD_GROUPS):
            yg = y[:, g * SSD_GW:(g + 1) * SSD_GW]
            yg = yg * lax.rsqrt(jnp.mean(yg * yg, axis=-1, keepdims=True) + EPS)
            y_parts.append(yg * sng[:, g * SSD_GW:(g + 1) * SSD_GW])
        return jnp.concatenate(y_parts, axis=1).astype(BF16)

    up_g = [_dot(gla_branch(rows), wug_ref[...]) for rows in subs]
    up_s = [_dot(ssd_branch(rows), wus_ref[...]) for rows in subs]
    mix = [(_sigmoid(gg_ref[rows, :]).astype(F32) * ug + _sigmoid(gs_ref[rows, :]).astype(F32) * us).astype(BF16)
           for rows, ug, us in zip(subs, up_g, up_s)]
    hs = [x_ref[rows, :] + _dot(m, wo_ref[...]) for rows, m in zip(subs, mix)]
    logit_parts = []
    for rows, h in zip(subs, hs):
        h_ref[rows, :] = h
        hn = (h * lax.rsqrt(jnp.mean(h * h, axis=-1, keepdims=True) + EPS)) * nfg_ref[...]
        words = _pack_bf16_pair(hn[:, :D_MODEL // 2], hn[:, D_MODEL // 2:])
        xp_ref[0, rows, :] = words[:, :SC_W]
        xp_ref[1, rows, :] = words[:, SC_W:]
        hn_hi, hn_lo = _split_bf16(hn, 2)
        logit_parts.append(_dot(hn_hi, wr_ref[0]) + (_dot(hn_hi, wr_ref[1]) + _dot(hn_lo, wr_ref[0])))
    logits = jnp.concatenate(logit_parts, axis=0) + br_ref[...]
    lane = lax.broadcasted_iota(I32, (tm, N_EXPERTS), 1).astype(F32)
    work = logits
    idxs, vals = [], []
    for _ in range(TOP_K):
        m = jnp.max(work, axis=-1, keepdims=True)
        idx = jnp.min(jnp.where(work == m, lane, float(N_EXPERTS)), axis=-1, keepdims=True)
        idxs.append(idx)
        vals.append(m)
        work = jnp.where(lane == idx, -jnp.inf, work)
    exps = [jnp.exp(v - vals[0]) for v in vals]
    denom = exps[0] + exps[1] + exps[2] + exps[3]
    gates = [e / denom for e in exps]
    sel = jnp.zeros((tm, N_EXPERTS), F32)
    for idx in idxs:
        sel = sel + jnp.where(lane == idx, 1.0, 0.0)
    rr = lax.broadcasted_iota(I32, (tm, tm), 0)
    cc = lax.broadcasted_iota(I32, (tm, tm), 1)
    strict = jnp.where(rr > cc, 1.0, 0.0).astype(BF16)
    pos = _dot(strict, sel.astype(BF16)) + cnt_ref[0:1, :]
    ranks = [jnp.sum(jnp.where(lane == idx, pos, 0.0), axis=-1, keepdims=True).astype(I32) for idx in idxs]
    cnt_new = cnt_ref[0:1, :] + jnp.sum(sel, axis=0, keepdims=True)
    cnt_ref[...] = jnp.broadcast_to(cnt_new, cnt_ref.shape)
    cnt_out_ref[...] = jnp.broadcast_to(cnt_new, cnt_ref.shape).astype(I32)
    lane128 = lax.broadcasted_iota(I32, (tm, 128), 1)
    route = jnp.zeros((tm, 128), I32)
    gate_o = jnp.zeros((tm, 128), F32)
    for k in range(TOP_K):
        route = jnp.where(lane128 == k, idxs[k].astype(I32), route)
        route = jnp.where(lane128 == TOP_K + k, ranks[k], route)
        gate_o = jnp.where(lane128 == k, gates[k], gate_o)
    route_ref[...] = route
    gate_ref[...] = gate_o


def _merge_params(gla_norm_g, ssd_d, ssd_norm_g, w_up_gla, w_up_ssd, w_out, norm_ffn_g, w_router, b_router):
    d_skip = jnp.repeat(ssd_d, SSD_HEADDIM)[None, :]
    return [gla_norm_g[None, :], d_skip, ssd_norm_g[None, :], w_up_gla.astype(BF16), w_up_ssd.astype(BF16),
            w_out.astype(BF16), norm_ffn_g[None, :], jnp.stack(_split_bf16(w_router, 2)), b_router[None, :]]


def _merge(x2, row0, o_f, o_b, proj, y_f, y_b, xs_c, params, tm):
    t = o_f.shape[0]
    blk0 = row0 // tm
    rowblk = lambda w, col=0: pl.BlockSpec((tm, w), lambda i: (i, col))
    const = lambda a: pl.BlockSpec(a.shape, lambda i: (0,) * a.ndim, pipeline_mode=pl.Buffered(1))
    return pl.pallas_call(
        _merge_kernel,
        grid=(t // tm,),
        in_specs=[pl.BlockSpec((tm, D_MODEL), lambda i: (blk0 + i, 0)),
                  rowblk(GLA_V), rowblk(GLA_V), rowblk(GLA_V, C_R // GLA_V),
                  rowblk(D_MODEL, C_GG // D_MODEL), rowblk(SSD_DINNER), rowblk(SSD_DINNER), rowblk(SSD_DINNER),
                  rowblk(SSD_DINNER, C_Z // SSD_DINNER), rowblk(D_MODEL, C_GS // D_MODEL)]
        + [const(p) for p in params],
        out_specs=[rowblk(D_MODEL), pl.BlockSpec((2, tm, SC_W), lambda i: (0, i, 0)), rowblk(128), rowblk(128),
                   pl.BlockSpec((8, N_EXPERTS), lambda i: (0, 0))],
        out_shape=[jax.ShapeDtypeStruct((t, D_MODEL), F32), jax.ShapeDtypeStruct((2, t, SC_W), U32),
                   jax.ShapeDtypeStruct((t, 128), I32), jax.ShapeDtypeStruct((t, 128), F32),
                   jax.ShapeDtypeStruct((8, N_EXPERTS), I32)],
        scratch_shapes=[pltpu.VMEM((8, N_EXPERTS), F32)],
        compiler_params=pltpu.CompilerParams(dimension_semantics=("arbitrary",), vmem_limit_bytes=VMEM_LIMIT),
        name="merge_router",
    )(x2, o_f, o_b, proj, proj, y_f, y_b, xs_c, proj, proj, *params)


def _sc_mesh():
    return plsc.VectorSubcoreMesh(core_axis_name="c", subcore_axis_name="s")


def _sc_scatter_rows(x, idx, n_out):
    n, m = x.shape[0], idx.shape[1]
    n_win = n // SC_WIN

    @functools.partial(pl.kernel, out_type=jax.ShapeDtypeStruct((n_out, SC_W), x.dtype), mesh=_sc_mesh())
    def scatter(x_hbm, i_hbm, o_hbm):
        def body(x_vmem, i_vmem):
            pltpu.sync_copy(x_vmem, o_hbm.at[i_vmem.at[0]])

        pltpu.emit_pipeline(
            body, grid=(m // SC_WIN,),
            in_specs=[pl.BlockSpec((SC_WIN, SC_W), lambda i: (i % n_win, 0)),
                      pl.BlockSpec((1, SC_WIN), lambda i: (0, i))],
            out_specs=[], core_axis_name=("c", "s"), dimension_semantics=(pltpu.PARALLEL,),
        )(x_hbm, i_hbm)

    return scatter(x, idx)


def _sc_gather_rows(table, idx):
    m = idx.shape[1]

    @functools.partial(pl.kernel, out_type=jax.ShapeDtypeStruct((m, SC_W), table.dtype), mesh=_sc_mesh())
    def gather(t_hbm, i_hbm, o_hbm):
        def body(i_vmem, o_vmem):
            pltpu.sync_copy(t_hbm.at[i_vmem.at[0]], o_vmem)

        pltpu.emit_pipeline(
            body, grid=(m // SC_WIN,),
            in_specs=[pl.BlockSpec((1, SC_WIN), lambda i: (0, i))],
            out_specs=[pl.BlockSpec((SC_WIN, SC_W), lambda i: (i, 0))],
            core_axis_name=("c", "s"), dimension_semantics=(pltpu.PARALLEL,),
        )(i_hbm, o_hbm)

    return gather(table, idx)


def _expert_kernel(blk_e_ref, nvalid_ref, nused_ref, x_ref, w1_ref, b1_ref, w2_ref, b2_ref, o_ref, w1_s, w2_s):
    del nused_ref
    i = pl.program_id(0)
    nvalid = nvalid_ref[i]

    @pl.when((nvalid > 0) & ((i == 0) | (blk_e_ref[i] != blk_e_ref[jnp.maximum(i - 1, 0)])))
    def _():
        w1_s[...] = w1_ref[0].astype(BF16)
        w2_s[...] = w2_ref[0].astype(BF16)

    @pl.when(nvalid > 0)
    def _():
        live = lax.broadcasted_iota(I32, (MOE_BM, SC_W), 0) < nvalid
        lo0, hi0 = _unpack_bf16_pair(jnp.where(live, x_ref[0], jnp.uint32(0)))
        lo1, hi1 = _unpack_bf16_pair(jnp.where(live, x_ref[1], jnp.uint32(0)))
        x = jnp.concatenate([lo0, lo1, hi0, hi1], axis=1).astype(BF16)
        hdn = _dot(x, w1_s[...]) + b1_ref[0]
        gate = jnp.minimum(hdn[:, :D_FF], SWIGLU_LIMIT)
        lin = jnp.clip(hdn[:, D_FF:], -SWIGLU_LIMIT, SWIGLU_LIMIT)
        act = gate * _sigmoid(SWIGLU_ALPHA * gate) * (lin + 1.0)
        y = _dot(act.astype(BF16), w2_s[...]) + b2_ref[0]
        words = _pack_bf16_pair(y[:, :D_MODEL // 2], y[:, D_MODEL // 2:])
        o_ref[0] = words[:, :SC_W]
        o_ref[1] = words[:, SC_W:]

    @pl.when(nvalid == 0)
    def _():
        o_ref[...] = jnp.zeros_like(o_ref)


def _experts(xb, blk_e, nvalid, n_used, w1, b1, w2, b2):
    n_rows = xb.shape[1]
    n_blocks = n_rows // MOE_BM
    xidx = lambda i, be, nv, nu: (0, jnp.minimum(i, nu[0] - 1), 0)
    eidx = lambda i, be, nv, nu: (be[i], 0, 0)
    return pl.pallas_call(
        _expert_kernel,
        grid_spec=pltpu.PrefetchScalarGridSpec(
            num_scalar_prefetch=3,
            grid=(n_blocks,),
            in_specs=[pl.BlockSpec((2, MOE_BM, SC_W), xidx),
                      pl.BlockSpec((1, D_MODEL, 2 * D_FF), eidx),
                      pl.BlockSpec((1, 1, 2 * D_FF), eidx),
                      pl.BlockSpec((1, D_FF, D_MODEL), eidx),
                      pl.BlockSpec((1, 1, D_MODEL), eidx)],
            out_specs=pl.BlockSpec((2, MOE_BM, SC_W), lambda i, be, nv, nu: (0, i, 0)),
            scratch_shapes=[pltpu.VMEM((D_MODEL, 2 * D_FF), BF16), pltpu.VMEM((D_FF, D_MODEL), BF16)],
        ),
        out_shape=jax.ShapeDtypeStruct((2, n_rows, SC_W), U32),
        compiler_params=pltpu.CompilerParams(dimension_semantics=("arbitrary",), vmem_limit_bytes=VMEM_LIMIT),
        name="moe_experts",
    )(blk_e, nvalid, n_used, xb, w1, b1, w2, b2)


def _combine_kernel(h_ref, gate_ref, y_ref, g_ref, *rest):
    o_ref = rest[-1]
    gates = gate_ref[...]
    acc = None
    for k in range(TOP_K):
        lo0, hi0 = _unpack_bf16_pair(y_ref[k, 0])
        lo1, hi1 = _unpack_bf16_pair(y_ref[k, 1])
        term = gates[:, k:k + 1] * jnp.concatenate([lo0, lo1, hi0, hi1], axis=1)
        acc = term if acc is None else acc + term
    h = h_ref[...] + acc
    o_ref[...] = (h * lax.rsqrt(jnp.mean(h * h, axis=-1, keepdims=True) + EPS)) * g_ref[...]


def _combine(h, gates, y_rows, norm_final_g, t_total, row0, prev, tm):
    t = h.shape[0]
    blk0 = row0 // tm
    in_specs = [pl.BlockSpec((tm, D_MODEL), lambda i: (i, 0)),
                pl.BlockSpec((tm, 128), lambda i: (i, 0)),
                pl.BlockSpec((TOP_K, 2, tm, SC_W), lambda i: (0, 0, i, 0)),
                pl.BlockSpec((1, D_MODEL), lambda i: (0, 0))]
    args = [h, gates, y_rows, norm_final_g[None, :]]
    if prev is not None:
        in_specs.append(pl.BlockSpec(memory_space=pl.ANY))
        args.append(prev)
    return pl.pallas_call(
        _combine_kernel,
        grid=(t // tm,),
        in_specs=in_specs,
        out_specs=pl.BlockSpec((tm, D_MODEL), lambda i: (blk0 + i, 0)),
        out_shape=jax.ShapeDtypeStruct((t_total, D_MODEL), F32),
        input_output_aliases={} if prev is None else {4: 0},
        compiler_params=pltpu.CompilerParams(dimension_semantics=("arbitrary",), vmem_limit_bytes=VMEM_LIMIT),
        name="moe_combine",
    )(*args)


def _pick(n, pref):
    b = min(pref, n)
    while n % b:
        b -= CHUNK
    return b


class _Tiles(NamedTuple):
    inproj_rows: int
    inproj_cols: int
    scan_rows: int
    conv_rows: int
    merge_rows: int
    combine_rows: int

    @classmethod
    def choose(cls, t, seq):
        return cls(inproj_rows=_pick(t, 2048), inproj_cols=1024, scan_rows=_pick(seq, 512),
                   conv_rows=_pick(seq, 512), merge_rows=_pick(t, 512), combine_rows=_pick(t, 512))


def _prep_w_in(w_in):
    widths = (GLA_QK, GLA_QK, GLA_V, GLA_V, GLA_RANK, GLA_RANK, SSD_DINNER, SSD_DINNER, SSD_BC, SSD_BC,
              2 * SSD_HEADS, D_MODEL, D_MODEL)
    pts, acc = [], 0
    for w in widths[:-1]:
        acc += w
        pts.append(acc)
    q, k, v, r, lrf, lrb, z, xs, bm, cm, dtr, gg, gs = jnp.split(w_in, pts, axis=1)
    main = jnp.concatenate([z, xs, q, k, v, r, gg, gs, bm, cm], axis=1).astype(BF16)
    pad = jnp.zeros((D_MODEL, N_SMALL - 2 * GLA_RANK - 2 * SSD_HEADS), w_in.dtype)
    small = jnp.concatenate([lrf, lrb, dtr, pad], axis=1).astype(BF16)
    return main, small


def _layer(h, norm_mix_g, w_in, gla_fw2_f, gla_fb_f, gla_fw2_b, gla_fb_b, gla_norm_g, conv_w, conv_b,
           dt_bias_f, dt_bias_b, a_log_f, a_log_b, ssd_d, ssd_norm_g, w_up_gla, w_up_ssd, w_out,
           norm_ffn_g, w_router, b_router, w1, b1, w2, b2, out_norm_g):
    bsz, seq, _ = h.shape
    t_total = bsz * seq
    x2 = h.reshape(t_total, D_MODEL)
    w_main, w_small = _prep_w_in(w_in)
    fw2f, fw2b = gla_fw2_f.astype(BF16), gla_fw2_b.astype(BF16)
    mparams = _merge_params(gla_norm_g, ssd_d, ssd_norm_g, w_up_gla, w_up_ssd, w_out, norm_ffn_g, w_router, b_router)
    n_groups = TOKEN_GROUPS if bsz % TOKEN_GROUPS == 0 else 1
    gb = bsz // n_groups
    t = gb * seq
    tiles = _Tiles.choose(t, seq)
    out = None
    for grp in range(n_groups):
        row0 = grp * t
        proj, small = _inproj(x2, norm_mix_g[None, :], w_main, w_small, row0, t,
                              tm=tiles.inproj_rows, tn=tiles.inproj_cols)
        o_f, o_b = _gla(proj, small, fw2f, gla_fb_f[None, :], fw2b, gla_fb_b[None, :], gb, seq, tiles.scan_rows)
        xs_c, bm_c, cm_c = _conv(proj, conv_w, conv_b, seq, tiles.conv_rows)
        y_f, y_b = _ssd(xs_c, bm_c, cm_c, small, dt_bias_f, dt_bias_b, a_log_f, a_log_b, gb, seq, tiles.scan_rows)
        hres, xp, route, gates, counts8 = _merge(x2, row0, o_f, o_b, proj, y_f, y_b, xs_c, mparams,
                                                 tm=tiles.merge_rows)
        counts = counts8[0]
        padded = (counts + MOE_BM - 1) // MOE_BM * MOE_BM
        pend = jnp.cumsum(padded)
        pstart = (pend - padded).astype(I32)
        n_rows = t * TOP_K + N_EXPERTS * MOE_BM
        blk_row = jnp.arange(n_rows // MOE_BM, dtype=I32) * MOE_BM
        blk_e = jnp.minimum(jnp.sum(pend[None, :] <= blk_row[:, None], axis=1), N_EXPERTS - 1).astype(I32)
        nvalid = jnp.clip(pstart[blk_e] + counts[blk_e] - blk_row, 0, MOE_BM).astype(I32)
        n_used = (pend[-1:] // MOE_BM).astype(I32)
        top_e, rank = route[:, :TOP_K], route[:, TOP_K:2 * TOP_K]
        dest = jnp.sum(jnp.where(top_e[:, :, None] == jnp.arange(N_EXPERTS, dtype=I32), pstart, 0), axis=-1) + rank
        idx = (dest.T[:, None, :] + (jnp.arange(2, dtype=I32) * n_rows)[None, :, None]).reshape(1, 2 * TOP_K * t)
        xb = _sc_scatter_rows(xp.reshape(2 * t, SC_W), idx, 2 * n_rows).reshape(2, n_rows, SC_W)
        yb = _experts(xb, blk_e, nvalid, n_used, w1, b1[:, None, :], w2, b2[:, None, :])
        y_rows = _sc_gather_rows(yb.reshape(2 * n_rows, SC_W), idx).reshape(TOP_K, 2, t, SC_W)
        out = _combine(hres, gates, y_rows, out_norm_g, t_total, row0, out, tm=tiles.combine_rows)
    return out.reshape(bsz, seq, D_MODEL)


def kernel(x, norm_mix_g, w_in, gla_fw2_f, gla_fb_f, gla_fw2_b, gla_fb_b, gla_norm_g, conv_w, conv_b, dt_bias_f,
           dt_bias_b, a_log_f, a_log_b, ssd_d, ssd_norm_g, w_up_gla, w_up_ssd, w_out, norm_ffn_g, w_router,
           b_router, w1, b1, w2, b2, norm_final_g):
    assert x.shape[-1] == D_MODEL and norm_mix_g.shape[0] == 1
    return _layer(x, norm_mix_g[0], w_in[0], gla_fw2_f[0], gla_fb_f[0], gla_fw2_b[0], gla_fb_b[0], gla_norm_g[0],
                  conv_w[0], conv_b[0], dt_bias_f[0], dt_bias_b[0], a_log_f[0], a_log_b[0], ssd_d[0],
                  ssd_norm_g[0], w_up_gla[0], w_up_ssd[0], w_out[0], norm_ffn_g[0], w_router[0], b_router[0],
                  w1[0], b1[0], w2[0], b2[0], norm_final_g)
```

```python
import functools
from typing import NamedTuple

import jax
import jax.numpy as jnp
from jax import lax
from jax.experimental import pallas as pl
from jax.experimental.pallas import tpu as pltpu
from jax.experimental.pallas import tpu_sc as plsc

F32 = jnp.float32
BF16 = jnp.bfloat16
I32 = jnp.int32
U32 = jnp.uint32

D_MODEL = 1024
EPS = 1e-5
GLA_HEADS = 4
GLA_DK = 128
GLA_DV = 256
GLA_RANK = 16
GLA_TAU = 16.0
GLA_QK = GLA_HEADS * GLA_DK
GLA_V = GLA_HEADS * GLA_DV
SSD_DINNER = 2048
SSD_HEADDIM = 64
SSD_HEADS = 32
SSD_GROUPS = 4
SSD_HPG = 8
SSD_STATE = 128
SSD_CONV = 4
SSD_BC = SSD_GROUPS * SSD_STATE
SSD_GW = SSD_HPG * SSD_HEADDIM
N_EXPERTS = 32
TOP_K = 4
D_FF = 1024
SWIGLU_LIMIT = 7.0
SWIGLU_ALPHA = 1.702
CHUNK = 64
TOKEN_GROUPS = 2
MERGE_SUBTILES = 4
SSD_INNER = 2
MXU_LAG = 6

C_Z, C_XS, C_Q, C_K, C_V, C_R, C_GG, C_GS, C_B, C_C = 0, 2048, 4096, 4608, 5120, 6144, 7168, 8192, 9216, 9728
N_MAIN = 10240
N_SMALL = 128
S_LRF, S_LRB, S_DTF, S_DTB = 0, 16, 32, 64

VMEM_LIMIT = 56 * 1024 * 1024
MOE_BM = 512
SC_WIN = 128
SC_W = D_MODEL // 4


def _dot(a, b):
    return jnp.dot(a, b, preferred_element_type=F32)


def _dot_nt(a, b):
    return lax.dot_general(a, b, (((1,), (1,)), ((), ())), preferred_element_type=F32)


def _dot_tn(a, b):
    return lax.dot_general(a, b, (((0,), (0,)), ((), ())), preferred_element_type=F32)


def _split_bf16(x, n):
    parts = []
    r = x
    for _ in range(n):
        p = r.astype(BF16)
        parts.append(p)
        r = r - p.astype(F32)
    return parts


def _dot_exact_lhs(m_bf16, x, n):
    acc = None
    for p in _split_bf16(x, n):
        t = _dot(m_bf16, p)
        acc = t if acc is None else acc + t
    return acc


def _dot_exact_rhs(x, m_bf16, n):
    acc = None
    for p in _split_bf16(x, n):
        t = _dot(p, m_bf16)
        acc = t if acc is None else acc + t
    return acc


def _sigmoid(x):
    return 1.0 / (1.0 + jnp.exp2(x * (-1.4426950408889634)))


def _silu(x):
    return x * _sigmoid(x)


def _pack_bf16_pair(lo, hi):
    lo_b = lax.bitcast_convert_type(lo.astype(BF16).astype(F32), U32)
    hi_b = lax.bitcast_convert_type(hi.astype(BF16).astype(F32), U32)
    return (lo_b >> 16) | (hi_b & jnp.uint32(0xFFFF0000))


def _unpack_bf16_pair(u):
    lo = lax.bitcast_convert_type(u << 16, F32)
    hi = lax.bitcast_convert_type(u & jnp.uint32(0xFFFF0000), F32)
    return lo, hi


def _inproj_kernel(x_ref, g_ref, w_ref, ws_ref, o_ref, os_ref, xn_ref):
    @pl.when(pl.program_id(1) == 0)
    def _():
        x = x_ref[...]
        ms = jnp.mean(x * x, axis=-1, keepdims=True)
        xb = ((x * lax.rsqrt(ms + EPS)) * g_ref[...]).astype(BF16)
        xn_ref[...] = xb
        os_ref[...] = _dot(xb, ws_ref[...])

    o_ref[...] = _dot(xn_ref[...], w_ref[...]).astype(BF16)


def _inproj(x2, g, w_main, w_small, row0, t, tm, tn):
    blk0 = row0 // tm
    return pl.pallas_call(
        _inproj_kernel,
        grid=(t // tm, N_MAIN // tn),
        in_specs=[
            pl.BlockSpec((tm, D_MODEL), lambda i, j: (blk0 + i, 0)),
            pl.BlockSpec((1, D_MODEL), lambda i, j: (0, 0)),
            pl.BlockSpec((D_MODEL, tn), lambda i, j: (0, j)),
            pl.BlockSpec((D_MODEL, N_SMALL), lambda i, j: (0, 0)),
        ],
        out_specs=[
            pl.BlockSpec((tm, tn), lambda i, j: (i, j)),
            pl.BlockSpec((tm, N_SMALL), lambda i, j: (i, 0)),
        ],
        out_shape=[
            jax.ShapeDtypeStruct((t, N_MAIN), BF16),
            jax.ShapeDtypeStruct((t, N_SMALL), F32),
        ],
        scratch_shapes=[pltpu.VMEM((tm, D_MODEL), BF16)],
        compiler_params=pltpu.CompilerParams(
            dimension_semantics=("arbitrary", "arbitrary"), vmem_limit_bytes=VMEM_LIMIT),
        name="inproj",
    )(x2, g, w_main, w_small)


def _tri_masks(n):
    r = lax.broadcasted_iota(I32, (n, n), 0)
    c = lax.broadcasted_iota(I32, (n, n), 1)
    return r >= c, c >= r


class _GlaDir:
    def __init__(self, ins, outs, scratch, mask, lr_off, mid_row, last_row):
        self.q, self.k, self.v, self.sm, self.fw2, self.fb = ins
        self.o, self.st = outs
        self.b_s, self.qs_s, self.ks_s, self.kd_s, self.qe_s, self.p_s, self.u_s, self.el_s = scratch
        self.mask, self.lr_off, self.mid_row, self.last_row = mask, lr_off, mid_row, last_row


def _gla_decay(d, n_chunks):
    tri = jnp.where(d.mask, 1.0, 0.0).astype(BF16)
    lr = d.sm[:, d.lr_off:d.lr_off + GLA_RANK].astype(BF16)
    xg = _dot(lr, d.fw2[...]) + d.fb[...]
    d.b_s[...] = (jnp.minimum(xg, 0.0) - jnp.log(1.0 + jnp.exp(-jnp.abs(xg)))) * (1.0 / GLA_TAU)
    for c in range(n_chunks):
        rows = slice(c * CHUNK, (c + 1) * CHUNK)
        d.b_s[rows, :] = _dot_exact_lhs(tri, d.b_s[rows, :], 2)


def _gla_scale(d, c):
    rows = slice(c * CHUNK, (c + 1) * CHUNK)
    b = d.b_s[rows, :]
    b_mid = b[d.mid_row:d.mid_row + 1, :]
    b_last = b[d.last_row:d.last_row + 1, :]
    q = d.q[rows, :].astype(F32) * (GLA_DK ** -0.5)
    k = d.k[rows, :].astype(F32)
    d.qs_s[rows, :] = (q * jnp.exp(b - b_mid)).astype(BF16)
    d.ks_s[rows, :] = (k * jnp.exp(b_mid - b)).astype(BF16)
    d.kd_s[rows, :] = (k * jnp.exp(b_last - b)).astype(BF16)
    d.qe_s[rows, :] = (q * jnp.exp(b)).astype(BF16)
    d.el_s[c:c + 1, :] = jnp.exp(b_last)


def _gla_local(units):
    def score(u):
        d, c, h = u
        rows, ks_ = slice(c * CHUNK, (c + 1) * CHUNK), slice(h * GLA_DK, (h + 1) * GLA_DK)
        return _dot_nt(d.qs_s[rows, ks_], d.ks_s[rows, ks_])

    def finish(u, s):
        d, c, h = u
        rows, ks_ = slice(c * CHUNK, (c + 1) * CHUNK), slice(h * GLA_DK, (h + 1) * GLA_DK)
        vs_ = slice(h * GLA_DV, (h + 1) * GLA_DV)
        d.p_s[c, h] = jnp.where(d.mask, s, 0.0).astype(BF16)
        d.u_s[c, h] = _dot_tn(d.kd_s[rows, ks_], d.v[rows, vs_])

    pending = []
    for u in units:
        pending.append((u, score(u)))
        if len(pending) > MXU_LAG:
            finish(*pending.pop(0))
    for item in pending:
        finish(*item)


def _gla_carry(d, c):
    rows = slice(c * CHUNK, (c + 1) * CHUNK)
    for h in range(GLA_HEADS):
        ks_ = slice(h * GLA_DK, (h + 1) * GLA_DK)
        vs_ = slice(h * GLA_DV, (h + 1) * GLA_DV)
        st = d.st[h]
        o = _dot(d.p_s[c, h], d.v[rows, vs_]) + _dot(d.qe_s[rows, ks_], st.astype(BF16))
        d.o[rows, vs_] = o.astype(d.o.dtype)
        e_col = jnp.transpose(jnp.broadcast_to(d.el_s[c:c + 1, ks_], (8, GLA_DK)))[:, 0:1]
        d.st[h] = st * e_col + d.u_s[c, h]


def _gla_kernel(*refs, n_chunks):
    ins_f, ins_b, (fw2f, fbf, fw2b, fbb), (of_ref, ob_ref, stf, stb) = refs[0:4], refs[4:8], refs[8:12], refs[12:16]
    scr_f, scr_b = refs[16:24], refs[24:32]

    @pl.when(pl.program_id(1) == 0)
    def _():
        stf[...] = jnp.zeros_like(stf)
        stb[...] = jnp.zeros_like(stb)

    lower, upper = _tri_masks(CHUNK)
    fwd = _GlaDir((*ins_f, fw2f, fbf), (of_ref, stf), scr_f, lower, S_LRF, CHUNK // 2, CHUNK - 1)
    bwd = _GlaDir((*ins_b, fw2b, fbb), (ob_ref, stb), scr_b, upper, S_LRB, CHUNK // 2 - 1, 0)
    _gla_decay(fwd, n_chunks)
    _gla_decay(bwd, n_chunks)
    for c in range(n_chunks):
        _gla_scale(fwd, c)
        _gla_scale(bwd, c)
    _gla_local([(d, c, h) for c in range(n_chunks) for d in (fwd, bwd) for h in range(GLA_HEADS)])
    for i in range(n_chunks):
        _gla_carry(fwd, i)
        _gla_carry(bwd, n_chunks - 1 - i)


def _gla(proj, small, fw2f, fbf, fw2b, fbb, bsz, seq, lb):
    t = bsz * seq
    nb = seq // lb

    def fwd(w, col):
        return pl.BlockSpec((lb, w), lambda b, n: (b * nb + n, col))

    def bwd(w, col):
        return pl.BlockSpec((lb, w), lambda b, n: (b * nb + nb - 1 - n, col))

    const = lambda shape: pl.BlockSpec(shape, lambda b, n: (0, 0))
    per_dir_scratch = ([pltpu.VMEM((lb, GLA_QK), F32)] + [pltpu.VMEM((lb, GLA_QK), BF16)] * 4
                       + [pltpu.VMEM((lb // CHUNK, GLA_HEADS, CHUNK, CHUNK), BF16),
                          pltpu.VMEM((lb // CHUNK, GLA_HEADS, GLA_DK, GLA_DV), F32),
                          pltpu.VMEM((lb // CHUNK, GLA_QK), F32)])
    return pl.pallas_call(
        functools.partial(_gla_kernel, n_chunks=lb // CHUNK),
        grid=(bsz, nb),
        in_specs=[
            fwd(GLA_QK, C_Q // GLA_QK), fwd(GLA_QK, C_K // GLA_QK), fwd(GLA_V, C_V // GLA_V), fwd(N_SMALL, 0),
            bwd(GLA_QK, C_Q // GLA_QK), bwd(GLA_QK, C_K // GLA_QK), bwd(GLA_V, C_V // GLA_V), bwd(N_SMALL, 0),
            const((GLA_RANK, GLA_QK)), const((1, GLA_QK)), const((GLA_RANK, GLA_QK)), const((1, GLA_QK)),
        ],
        out_specs=[fwd(GLA_V, 0), bwd(GLA_V, 0)],
        out_shape=[jax.ShapeDtypeStruct((t, GLA_V), BF16)] * 2,
        scratch_shapes=[pltpu.VMEM((GLA_HEADS, GLA_DK, GLA_DV), F32)] * 2 + per_dir_scratch * 2,
        compiler_params=pltpu.CompilerParams(
            dimension_semantics=("arbitrary", "arbitrary"), vmem_limit_bytes=VMEM_LIMIT),
        name="gla_scan",
    )(proj, proj, proj, small, proj, proj, proj, small, fw2f, fbf, fw2b, fbb)


HALO = 16


CONV_SUB = 256
CONV_COLS = 512


def _conv_taps(x, xm1, xp1, xp2, w, b):
    return _silu(xm1 * w[0:1, :] + x * w[1:2, :] + xp1 * w[2:3, :] + xp2 * w[3:4, :] + b)


def _shift_matrix(n):
    r = lax.broadcasted_iota(I32, (3 * n, n), 0)
    c = lax.broadcasted_iota(I32, (3 * n, n), 1)
    src = jnp.where(r < n, r - 1, jnp.where(r < 2 * n, r - n + 1, r - 2 * n + 2))
    return jnp.where(c == src, 1.0, 0.0).astype(BF16)


def _conv_one(x_ref, p_ref, n_ref, w_ref, b_ref, o_ref, shift, has_prev, has_next):
    rb, wd = x_ref.shape
    w, b = w_ref[...], b_ref[...]
    for s in range(rb // CONV_SUB):
        rows = slice(s * CONV_SUB, (s + 1) * CONV_SUB)
        for c0 in range(0, wd, CONV_COLS):
            cols = slice(c0, c0 + CONV_COLS)
            xb = x_ref[rows, cols]
            sx = _dot(shift, xb)
            y = _conv_taps(xb.astype(F32), sx[0:CONV_SUB], sx[CONV_SUB:2 * CONV_SUB], sx[2 * CONV_SUB:],
                           w[:, cols], b[:, cols])
            o_ref[rows, cols] = y.astype(o_ref.dtype)
    prev = jnp.where(has_prev, p_ref[HALO - 1:HALO, :].astype(F32), 0.0)
    nxt = jnp.where(has_next, n_ref[0:2, :].astype(F32), 0.0)
    row = lax.broadcasted_iota(I32, (HALO, wd), 0)
    head = x_ref[0:2 * HALO, :].astype(F32)
    xm1 = jnp.where(row == 0, prev, pltpu.roll(head, 1, 0)[0:HALO])
    o_ref[0:HALO, :] = _conv_taps(head[0:HALO], xm1, pltpu.roll(head, 2 * HALO - 1, 0)[0:HALO],
                                  pltpu.roll(head, 2 * HALO - 2, 0)[0:HALO], w, b).astype(o_ref.dtype)
    tail = x_ref[rb - 2 * HALO:rb, :].astype(F32)
    xp1 = jnp.where(row == HALO - 1, nxt[0:1, :], pltpu.roll(tail, 2 * HALO - 1, 0)[HALO:])
    xp2 = jnp.where(row == HALO - 2, nxt[0:1, :],
                    jnp.where(row == HALO - 1, nxt[1:2, :], pltpu.roll(tail, 2 * HALO - 2, 0)[HALO:]))
    o_ref[rb - HALO:rb, :] = _conv_taps(tail[HALO:], pltpu.roll(tail, 1, 0)[HALO:], xp1, xp2, w, b).astype(o_ref.dtype)
    for s in range(1, rb // CONV_SUB):
        e = s * CONV_SUB
        win = x_ref[e - 2 * HALO:e + 2 * HALO, :].astype(F32)
        mid = slice(HALO, 3 * HALO)
        o_ref[e - HALO:e + HALO, :] = _conv_taps(
            win[mid], pltpu.roll(win, 1, 0)[mid], pltpu.roll(win, 4 * HALO - 1, 0)[mid],
            pltpu.roll(win, 4 * HALO - 2, 0)[mid], w, b).astype(o_ref.dtype)


def _conv_kernel(xs, xsp, xsn, bm, bmp, bmn, cm, cmp_, cmn, wx, bx, wb, bb, wc, bc, oxs, obm, ocm, *, rb, seq):
    t0 = pl.program_id(0) * rb
    has_prev = (t0 % seq) != 0
    has_next = ((t0 + rb) % seq) != 0
    shift = _shift_matrix(CONV_SUB)
    _conv_one(xs, xsp, xsn, wx, bx, oxs, shift, has_prev, has_next)
    _conv_one(bm, bmp, bmn, wb, bb, obm, shift, has_prev, has_next)
    _conv_one(cm, cmp_, cmn, wc, bc, ocm, shift, has_prev, has_next)


def _conv(proj, conv_w, conv_b, seq, rb):
    t = proj.shape[0]
    nh = t // HALO
    per = rb // HALO

    def trio(w, col):
        cb = col // w
        return [
            pl.BlockSpec((rb, w), lambda i: (i, cb)),
            pl.BlockSpec((HALO, w), lambda i: (jnp.maximum(i * per - 1, 0), cb)),
            pl.BlockSpec((HALO, w), lambda i: (jnp.minimum((i + 1) * per, nh - 1), cb)),
        ]

    def wspecs(w):
        return [pl.BlockSpec((SSD_CONV, w), lambda i: (0, 0)), pl.BlockSpec((1, w), lambda i: (0, 0))]

    wx, wb, wc = conv_w[:, :SSD_DINNER], conv_w[:, SSD_DINNER:SSD_DINNER + SSD_BC], conv_w[:, SSD_DINNER + SSD_BC:]
    bx, bb, bc = (conv_b[None, :SSD_DINNER], conv_b[None, SSD_DINNER:SSD_DINNER + SSD_BC],
                  conv_b[None, SSD_DINNER + SSD_BC:])
    return pl.pallas_call(
        functools.partial(_conv_kernel, rb=rb, seq=seq),
        grid=(t // rb,),
        in_specs=trio(SSD_DINNER, C_XS) + trio(SSD_BC, C_B) + trio(SSD_BC, C_C)
        + wspecs(SSD_DINNER) + wspecs(SSD_BC) + wspecs(SSD_BC),
        out_specs=[pl.BlockSpec((rb, SSD_DINNER), lambda i: (i, 0)),
                   pl.BlockSpec((rb, SSD_BC), lambda i: (i, 0)),
                   pl.BlockSpec((rb, SSD_BC), lambda i: (i, 0))],
        out_shape=[jax.ShapeDtypeStruct((t, SSD_DINNER), BF16),
                   jax.ShapeDtypeStruct((t, SSD_BC), BF16),
                   jax.ShapeDtypeStruct((t, SSD_BC), BF16)],
        compiler_params=pltpu.CompilerParams(dimension_semantics=("arbitrary",), vmem_limit_bytes=VMEM_LIMIT),
        name="ssd_conv",
    )(proj, proj, proj, proj, proj, proj, proj, proj, proj, wx, bx, wb, bb, wc, bc)


def _softplus(x):
    return jnp.maximum(x, 0.0) + jnp.log(1.0 + jnp.exp(-jnp.abs(x)))


class _SsdDir:
    def __init__(self, xs, bm, cm, sm, dtb_row, alog_row, y, st, reverse, dt_off, last_row):
        self.xs, self.bm, self.cm, self.sm, self.dtb_row, self.alog_row = xs, bm, cm, sm, dtb_row, alog_row
        self.y, self.st, self.reverse, self.dt_off, self.last_row = y, st, reverse, dt_off, last_row


def _ssd_chunks(work):
    hh = lax.broadcasted_iota(I32, (SSD_HEADS, SSD_GW), 0)
    cc = lax.broadcasted_iota(I32, (SSD_HEADS, SSD_GW), 1)
    lane = lax.broadcasted_iota(I32, (CHUNK, 2 * SSD_HEADDIM), 1)
    row2 = lax.broadcasted_iota(I32, (CHUNK, 2 * SSD_HEADDIM), 0)
    left = lane < SSD_HEADDIM
    col2 = jnp.where(left, lane, lane - SSD_HEADDIM)
    lower, upper = _tri_masks(CHUNK)

    pre = []
    for d, c0 in work:
        rows = pl.ds(c0, CHUNK)
        tri = jnp.where(upper if d.reverse else lower, 1.0, 0.0).astype(BF16)
        a_row = -jnp.exp(d.alog_row[...])
        dt = _softplus(d.sm[rows, d.dt_off:d.dt_off + SSD_HEADS] + d.dtb_row[...])
        pre.append((dt, _dot_exact_lhs(tri, dt * a_row, 3)))
    heads = []
    for (d, c0), (dt, cum) in zip(work, pre):
        total = cum[d.last_row:d.last_row + 1, :]
        to_end = jnp.exp(total - cum) * dt
        fac = jnp.concatenate([to_end, jnp.exp(cum)], axis=0).astype(BF16)
        e_tot = jnp.broadcast_to(jnp.exp(total), (8, SSD_HEADS))
        heads.append((cum, jnp.transpose(cum), jnp.transpose(dt), fac, e_tot))

    units = [(w, g) for w in range(len(work)) for g in range(SSD_GROUPS)]
    groups = {}
    for w, g in units:
        d, c0 = work[w]
        rows, ns = pl.ds(c0, CHUNK), slice(g * SSD_STATE, (g + 1) * SSD_STATE)
        fac, e_tot = heads[w][3], heads[w][4]
        expand = jnp.where(cc // SSD_HEADDIM + g * SSD_HPG == hh, 1.0, 0.0).astype(BF16)
        fac_x = _dot(fac, expand)
        et_x = _dot_exact_rhs(e_tot, expand, 2)[0:1]
        groups[w, g] = (fac_x, et_x, _dot_nt(d.cm[rows, ns], d.bm[rows, ns]))

    for w, g in units:
        d, c0 = work[w]
        rows, ns, cs = pl.ds(c0, CHUNK), slice(g * SSD_STATE, (g + 1) * SSD_STATE), slice(g * SSD_GW, (g + 1) * SSD_GW)
        cum, cum_t, dt_t = heads[w][0:3]
        fac_x, _, cb = groups[w, g]
        mask2 = (col2 >= row2) if d.reverse else (row2 >= col2)
        y_inter = _dot(d.cm[rows, ns], d.st[g].astype(BF16))
        cb2 = jnp.concatenate([cb, cb], axis=1)
        parts = []
        for p in range(SSD_HPG // 2):
            h0 = g * SSD_HPG + 2 * p
            ps = slice(h0 * SSD_HEADDIM, (h0 + 2) * SSD_HEADDIM)
            col = jnp.where(left, cum[:, h0:h0 + 1], cum[:, h0 + 1:h0 + 2])
            rowv = jnp.concatenate([cum_t[h0:h0 + 1, :], cum_t[h0 + 1:h0 + 2, :]], axis=1)
            dtv = jnp.concatenate([dt_t[h0:h0 + 1, :], dt_t[h0 + 1:h0 + 2, :]], axis=1)
            decay = jnp.exp(jnp.where(mask2, col - rowv, -jnp.inf))
            wgt = (cb2 * decay * dtv).astype(BF16)
            xp = d.xs[rows, ps]
            zero = jnp.zeros_like(xp)
            xbd = jnp.concatenate([jnp.where(left, xp, zero), jnp.where(left, zero, xp)], axis=0)
            parts.append(_dot(wgt, xbd))
        y = jnp.concatenate(parts, axis=1) + y_inter * fac_x[CHUNK:2 * CHUNK]
        d.y[rows, cs] = y.astype(d.y.dtype)

    def increment(w, g):
        d, c0 = work[w]
        rows, ns, cs = pl.ds(c0, CHUNK), slice(g * SSD_STATE, (g + 1) * SSD_STATE), slice(g * SSD_GW, (g + 1) * SSD_GW)
        return _dot_tn(d.bm[rows, ns], d.xs[rows, cs] * groups[w, g][0][0:CHUNK].astype(BF16))

    def update(w, g, inc):
        d = work[w][0]
        d.st[g] = d.st[g] * groups[w, g][1] + inc

    pending = None
    for w, g in units:
        inc = increment(w, g)
        if pending is not None:
            update(*pending)
        pending = (w, g, inc)
    update(*pending)


def _ssd_kernel(*refs, n_chunks, inner):
    ins, (dbf_r, dbb_r, alf_r, alb_r) = refs[:8 * inner], refs[8 * inner:8 * inner + 4]
    (yf_ref, yb_ref), states = refs[8 * inner + 4:8 * inner + 6], refs[8 * inner + 6:]

    @pl.when(pl.program_id(1) == 0)
    def _():
        for st in states:
            st[...] = jnp.zeros_like(st)

    fwd = [_SsdDir(*ins[8 * e:8 * e + 4], dbf_r, alf_r, yf_ref.at[e], states[2 * e], False, S_DTF, CHUNK - 1)
           for e in range(inner)]
    bwd = [_SsdDir(*ins[8 * e + 4:8 * e + 8], dbb_r, alb_r, yb_ref.at[e], states[2 * e + 1], True, S_DTB, 0)
           for e in range(inner)]

    def body(i, carry):
        c_f = pl.multiple_of(i * CHUNK, CHUNK)
        c_b = pl.multiple_of((n_chunks - 1 - i) * CHUNK, CHUNK)
        _ssd_chunks([(d, c_f) for d in fwd] + [(d, c_b) for d in bwd])
        return carry

    lax.fori_loop(0, n_chunks, body, 0)


def _ssd(xs_c, bm_c, cm_c, small, dtb_f, dtb_b, alog_f, alog_b, bsz, seq, lb):
    t = bsz * seq
    nb = seq // lb
    inner = SSD_INNER if bsz % SSD_INNER == 0 else 1

    def specs(e, reverse):
        idx = lambda p, n: (p * inner + e) * nb + (nb - 1 - n if reverse else n)
        return [pl.BlockSpec((lb, w), lambda p, n: (idx(p, n), 0)) for w in (SSD_DINNER, SSD_BC, SSD_BC, N_SMALL)]

    row = pl.BlockSpec((1, SSD_HEADS), lambda p, n: (0, 0))
    args = (xs_c, bm_c, cm_c, small)
    y_f, y_b = pl.pallas_call(
        functools.partial(_ssd_kernel, n_chunks=lb // CHUNK, inner=inner),
        grid=(bsz // inner, nb),
        in_specs=[s for e in range(inner) for rev in (False, True) for s in specs(e, rev)] + [row] * 4,
        out_specs=[pl.BlockSpec((None, inner, lb, SSD_DINNER), lambda p, n: (p, 0, n, 0)),
                   pl.BlockSpec((None, inner, lb, SSD_DINNER), lambda p, n: (p, 0, nb - 1 - n, 0))],
        out_shape=[jax.ShapeDtypeStruct((bsz // inner, inner, seq, SSD_DINNER), BF16)] * 2,
        scratch_shapes=[pltpu.VMEM((SSD_GROUPS, SSD_STATE, SSD_GW), F32)] * (2 * inner),
        compiler_params=pltpu.CompilerParams(
            dimension_semantics=("arbitrary", "arbitrary"), vmem_limit_bytes=VMEM_LIMIT),
        name="ssd_scan",
    )(*(args * (2 * inner)), dtb_f[None, :], dtb_b[None, :], alog_f[None, :], alog_b[None, :])
    return y_f.reshape(t, SSD_DINNER), y_b.reshape(t, SSD_DINNER)


def _merge_kernel(x_ref, of_ref, ob_ref, r_ref, gg_ref, yf_ref, yb_ref, xs_ref, z_ref, gs_ref,
                  gng_ref, dsk_ref, sng_ref, wug_ref, wus_ref, wo_ref, nfg_ref, wr_ref, br_ref,
                  h_ref, xp_ref, route_ref, gate_ref, cnt_out_ref, cnt_ref):
    @pl.when(pl.program_id(0) == 0)
    def _():
        cnt_ref[...] = jnp.zeros_like(cnt_ref)

    tm = x_ref.shape[0]
    subs = [slice(s * (tm // MERGE_SUBTILES), (s + 1) * (tm // MERGE_SUBTILES)) for s in range(MERGE_SUBTILES)]

    def gla_branch(rows):
        o = of_ref[rows, :].astype(F32) + ob_ref[rows, :].astype(F32)
        gng = gng_ref[...]
        o_parts = []
        for h in range(GLA_HEADS):
            oh = o[:, h * GLA_DV:(h + 1) * GLA_DV]
            oh = oh * lax.rsqrt(jnp.mean(oh * oh, axis=-1, keepdims=True) + EPS)
            o_parts.append(oh * gng)
        return (jnp.concatenate(o_parts, axis=1) * _silu(r_ref[rows, :]).astype(F32)).astype(BF16)

    def ssd_branch(rows):
        y = (yf_ref[rows, :].astype(F32) + yb_ref[rows, :].astype(F32)
             + dsk_ref[...] * xs_ref[rows, :].astype(F32))
        y = y * _silu(z_ref[rows, :]).astype(F32)
        sng = sng_ref[...]
        y_parts = []
        for g in range(SSD_GROUPS):
            yg = y[:, g * SSD_GW:(g + 1) * SSD_GW]
            yg = yg * lax.rsqrt(jnp.mean(yg * yg, axis=-1, keepdims=True) + EPS)
            y_parts.append(yg * sng[:, g * SSD_GW:(g + 1) * SSD_GW])
        return jnp.concatenate(y_parts, axis=1).astype(BF16)

    up_g = [_dot(gla_branch(rows), wug_ref[...]) for rows in subs]
    up_s = [_dot(ssd_branch(rows), wus_ref[...]) for rows in subs]
    mix = [(_sigmoid(gg_ref[rows, :]).astype(F32) * ug + _sigmoid(gs_ref[rows, :]).astype(F32) * us).astype(BF16)
           for rows, ug, us in zip(subs, up_g, up_s)]
    hs = [x_ref[rows, :] + _dot(m, wo_ref[...]) for rows, m in zip(subs, mix)]
    logit_parts = []
    for rows, h in zip(subs, hs):
        h_ref[rows, :] = h
        hn = (h * lax.rsqrt(jnp.mean(h * h, axis=-1, keepdims=True) + EPS)) * nfg_ref[...]
        words = _pack_bf16_pair(hn[:, :D_MODEL // 2], hn[:, D_MODEL // 2:])
        xp_ref[0, rows, :] = words[:, :SC_W]
        xp_ref[1, rows, :] = words[:, SC_W:]
        hn_hi, hn_lo = _split_bf16(hn, 2)
        logit_parts.append(_dot(hn_hi, wr_ref[0]) + (_dot(hn_hi, wr_ref[1]) + _dot(hn_lo, wr_ref[0])))
    logits = jnp.concatenate(logit_parts, axis=0) + br_ref[...]
    lane = lax.broadcasted_iota(I32, (tm, N_EXPERTS), 1).astype(F32)
    work = logits
    idxs, vals = [], []
    for _ in range(TOP_K):
        m = jnp.max(work, axis=-1, keepdims=True)
        idx = jnp.min(jnp.where(work == m, lane, float(N_EXPERTS)), axis=-1, keepdims=True)
        idxs.append(idx)
        vals.append(m)
        work = jnp.where(lane == idx, -jnp.inf, work)
    exps = [jnp.exp(v - vals[0]) for v in vals]
    denom = exps[0] + exps[1] + exps[2] + exps[3]
    gates = [e / denom for e in exps]
    sel = jnp.zeros((tm, N_EXPERTS), F32)
    for idx in idxs:
        sel = sel + jnp.where(lane == idx, 1.0, 0.0)
    rr = lax.broadcasted_iota(I32, (tm, tm), 0)
    cc = lax.broadcasted_iota(I32, (tm, tm), 1)
    strict = jnp.where(rr > cc, 1.0, 0.0).astype(BF16)
    pos = _dot(strict, sel.astype(BF16)) + cnt_ref[0:1, :]
    ranks = [jnp.sum(jnp.where(lane == idx, pos, 0.0), axis=-1, keepdims=True).astype(I32) for idx in idxs]
    cnt_new = cnt_ref[0:1, :] + jnp.sum(sel, axis=0, keepdims=True)
    cnt_ref[...] = jnp.broadcast_to(cnt_new, cnt_ref.shape)
    cnt_out_ref[...] = jnp.broadcast_to(cnt_new, cnt_ref.shape).astype(I32)
    lane128 = lax.broadcasted_iota(I32, (tm, 128), 1)
    route = jnp.zeros((tm, 128), I32)
    gate_o = jnp.zeros((tm, 128), F32)
    for k in range(TOP_K):
        route = jnp.where(lane128 == k, idxs[k].astype(I32), route)
        route = jnp.where(lane128 == TOP_K + k, ranks[k], route)
        gate_o = jnp.where(lane128 == k, gates[k], gate_o)
    route_ref[...] = route
    gate_ref[...] = gate_o


def _merge_params(gla_norm_g, ssd_d, ssd_norm_g, w_up_gla, w_up_ssd, w_out, norm_ffn_g, w_router, b_router):
    d_skip = jnp.repeat(ssd_d, SSD_HEADDIM)[None, :]
    return [gla_norm_g[None, :], d_skip, ssd_norm_g[None, :], w_up_gla.astype(BF16), w_up_ssd.astype(BF16),
            w_out.astype(BF16), norm_ffn_g[None, :], jnp.stack(_split_bf16(w_router, 2)), b_router[None, :]]


def _merge(x2, row0, o_f, o_b, proj, y_f, y_b, xs_c, params, tm):
    t = o_f.shape[0]
    blk0 = row0 // tm
    rowblk = lambda w, col=0: pl.BlockSpec((tm, w), lambda i: (i, col))
    const = lambda a: pl.BlockSpec(a.shape, lambda i: (0,) * a.ndim, pipeline_mode=pl.Buffered(1))
    return pl.pallas_call(
        _merge_kernel,
        grid=(t // tm,),
        in_specs=[pl.BlockSpec((tm, D_MODEL), lambda i: (blk0 + i, 0)),
                  rowblk(GLA_V), rowblk(GLA_V), rowblk(GLA_V, C_R // GLA_V),
                  rowblk(D_MODEL, C_GG // D_MODEL), rowblk(SSD_DINNER), rowblk(SSD_DINNER), rowblk(SSD_DINNER),
                  rowblk(SSD_DINNER, C_Z // SSD_DINNER), rowblk(D_MODEL, C_GS // D_MODEL)]
        + [const(p) for p in params],
        out_specs=[rowblk(D_MODEL), pl.BlockSpec((2, tm, SC_W), lambda i: (0, i, 0)), rowblk(128), rowblk(128),
                   pl.BlockSpec((8, N_EXPERTS), lambda i: (0, 0))],
        out_shape=[jax.ShapeDtypeStruct((t, D_MODEL), F32), jax.ShapeDtypeStruct((2, t, SC_W), U32),
                   jax.ShapeDtypeStruct((t, 128), I32), jax.ShapeDtypeStruct((t, 128), F32),
                   jax.ShapeDtypeStruct((8, N_EXPERTS), I32)],
        scratch_shapes=[pltpu.VMEM((8, N_EXPERTS), F32)],
        compiler_params=pltpu.CompilerParams(dimension_semantics=("arbitrary",), vmem_limit_bytes=VMEM_LIMIT),
        name="merge_router",
    )(x2, o_f, o_b, proj, proj, y_f, y_b, xs_c, proj, proj, *params)


def _sc_mesh():
    return plsc.VectorSubcoreMesh(core_axis_name="c", subcore_axis_name="s")


def _sc_scatter_rows(x, idx, n_out):
    n, m = x.shape[0], idx.shape[1]
    n_win = n // SC_WIN

    @functools.partial(pl.kernel, out_type=jax.ShapeDtypeStruct((n_out, SC_W), x.dtype), mesh=_sc_mesh())
    def scatter(x_hbm, i_hbm, o_hbm):
        def body(x_vmem, i_vmem):
            pltpu.sync_copy(x_vmem, o_hbm.at[i_vmem.at[0]])

        pltpu.emit_pipeline(
            body, grid=(m // SC_WIN,),
            in_specs=[pl.BlockSpec((SC_WIN, SC_W), lambda i: (i % n_win, 0)),
                      pl.BlockSpec((1, SC_WIN), lambda i: (0, i))],
            out_specs=[], core_axis_name=("c", "s"), dimension_semantics=(pltpu.PARALLEL,),
        )(x_hbm, i_hbm)

    return scatter(x, idx)


def _sc_gather_rows(table, idx):
    m = idx.shape[1]

    @functools.partial(pl.kernel, out_type=jax.ShapeDtypeStruct((m, SC_W), table.dtype), mesh=_sc_mesh())
    def gather(t_hbm, i_hbm, o_hbm):
        def body(i_vmem, o_vmem):
            pltpu.sync_copy(t_hbm.at[i_vmem.at[0]], o_vmem)

        pltpu.emit_pipeline(
            body, grid=(m // SC_WIN,),
            in_specs=[pl.BlockSpec((1, SC_WIN), lambda i: (0, i))],
            out_specs=[pl.BlockSpec((SC_WIN, SC_W), lambda i: (i, 0))],
            core_axis_name=("c", "s"), dimension_semantics=(pltpu.PARALLEL,),
        )(i_hbm, o_hbm)

    return gather(table, idx)


def _expert_kernel(blk_e_ref, nvalid_ref, nused_ref, x_ref, w1_ref, b1_ref, w2_ref, b2_ref, o_ref, w1_s, w2_s):
    del nused_ref
    i = pl.program_id(0)
    nvalid = nvalid_ref[i]

    @pl.when((nvalid > 0) & ((i == 0) | (blk_e_ref[i] != blk_e_ref[jnp.maximum(i - 1, 0)])))
    def _():
        w1_s[...] = w1_ref[0].astype(BF16)
        w2_s[...] = w2_ref[0].astype(BF16)

    @pl.when(nvalid > 0)
    def _():
        live = lax.broadcasted_iota(I32, (MOE_BM, SC_W), 0) < nvalid
        lo0, hi0 = _unpack_bf16_pair(jnp.where(live, x_ref[0], jnp.uint32(0)))
        lo1, hi1 = _unpack_bf16_pair(jnp.where(live, x_ref[1], jnp.uint32(0)))
        x = jnp.concatenate([lo0, lo1, hi0, hi1], axis=1).astype(BF16)
        hdn = _dot(x, w1_s[...]) + b1_ref[0]
        gate = jnp.minimum(hdn[:, :D_FF], SWIGLU_LIMIT)
        lin = jnp.clip(hdn[:, D_FF:], -SWIGLU_LIMIT, SWIGLU_LIMIT)
        act = gate * _sigmoid(SWIGLU_ALPHA * gate) * (lin + 1.0)
        y = _dot(act.astype(BF16), w2_s[...]) + b2_ref[0]
        words = _pack_bf16_pair(y[:, :D_MODEL // 2], y[:, D_MODEL // 2:])
        o_ref[0] = words[:, :SC_W]
        o_ref[1] = words[:, SC_W:]

    @pl.when(nvalid == 0)
    def _():
        o_ref[...] = jnp.zeros_like(o_ref)


def _experts(xb, blk_e, nvalid, n_used, w1, b1, w2, b2):
    n_rows = xb.shape[1]
    n_blocks = n_rows // MOE_BM
    xidx = lambda i, be, nv, nu: (0, jnp.minimum(i, nu[0] - 1), 0)
    eidx = lambda i, be, nv, nu: (be[i], 0, 0)
    return pl.pallas_call(
        _expert_kernel,
        grid_spec=pltpu.PrefetchScalarGridSpec(
            num_scalar_prefetch=3,
            grid=(n_blocks,),
            in_specs=[pl.BlockSpec((2, MOE_BM, SC_W), xidx),
                      pl.BlockSpec((1, D_MODEL, 2 * D_FF), eidx),
                      pl.BlockSpec((1, 1, 2 * D_FF), eidx),
                      pl.BlockSpec((1, D_FF, D_MODEL), eidx),
                      pl.BlockSpec((1, 1, D_MODEL), eidx)],
            out_specs=pl.BlockSpec((2, MOE_BM, SC_W), lambda i, be, nv, nu: (0, i, 0)),
            scratch_shapes=[pltpu.VMEM((D_MODEL, 2 * D_FF), BF16), pltpu.VMEM((D_FF, D_MODEL), BF16)],
        ),
        out_shape=jax.ShapeDtypeStruct((2, n_rows, SC_W), U32),
        compiler_params=pltpu.CompilerParams(dimension_semantics=("arbitrary",), vmem_limit_bytes=VMEM_LIMIT),
        name="moe_experts",
    )(blk_e, nvalid, n_used, xb, w1, b1, w2, b2)


def _combine_kernel(h_ref, gate_ref, y_ref, g_ref, *rest):
    o_ref = rest[-1]
    gates = gate_ref[...]
    acc = None
    for k in range(TOP_K):
        lo0, hi0 = _unpack_bf16_pair(y_ref[k, 0])
        lo1, hi1 = _unpack_bf16_pair(y_ref[k, 1])
        term = gates[:, k:k + 1] * jnp.concatenate([lo0, lo1, hi0, hi1], axis=1)
        acc = term if acc is None else acc + term
    h = h_ref[...] + acc
    o_ref[...] = (h * lax.rsqrt(jnp.mean(h * h, axis=-1, keepdims=True) + EPS)) * g_ref[...]


def _combine(h, gates, y_rows, norm_final_g, t_total, row0, prev, tm):
    t = h.shape[0]
    blk0 = row0 // tm
    in_specs = [pl.BlockSpec((tm, D_MODEL), lambda i: (i, 0)),
                pl.BlockSpec((tm, 128), lambda i: (i, 0)),
                pl.BlockSpec((TOP_K, 2, tm, SC_W), lambda i: (0, 0, i, 0)),
                pl.BlockSpec((1, D_MODEL), lambda i: (0, 0))]
    args = [h, gates, y_rows, norm_final_g[None, :]]
    if prev is not None:
        in_specs.append(pl.BlockSpec(memory_space=pl.ANY))
        args.append(prev)
    return pl.pallas_call(
        _combine_kernel,
        grid=(t // tm,),
        in_specs=in_specs,
        out_specs=pl.BlockSpec((tm, D_MODEL), lambda i: (blk0 + i, 0)),
        out_shape=jax.ShapeDtypeStruct((t_total, D_MODEL), F32),
        input_output_aliases={} if prev is None else {4: 0},
        compiler_params=pltpu.CompilerParams(dimension_semantics=("arbitrary",), vmem_limit_bytes=VMEM_LIMIT),
        name="moe_combine",
    )(*args)


def _pick(n, pref):
    b = min(pref, n)
    while n % b:
        b -= CHUNK
    return b


class _Tiles(NamedTuple):
    inproj_rows: int
    inproj_cols: int
    scan_rows: int
    conv_rows: int
    merge_rows: int
    combine_rows: int

    @classmethod
    def choose(cls, t, seq):
        return cls(inproj_rows=_pick(t, 2048), inproj_cols=1024, scan_rows=_pick(seq, 512),
                   conv_rows=_pick(seq, 512), merge_rows=_pick(t, 512), combine_rows=_pick(t, 512))


def _prep_w_in(w_in):
    widths = (GLA_QK, GLA_QK, GLA_V, GLA_V, GLA_RANK, GLA_RANK, SSD_DINNER, SSD_DINNER, SSD_BC, SSD_BC,
              2 * SSD_HEADS, D_MODEL, D_MODEL)
    pts, acc = [], 0
    for w in widths[:-1]:
        acc += w
        pts.append(acc)
    q, k, v, r, lrf, lrb, z, xs, bm, cm, dtr, gg, gs = jnp.split(w_in, pts, axis=1)
    main = jnp.concatenate([z, xs, q, k, v, r, gg, gs, bm, cm], axis=1).astype(BF16)
    pad = jnp.zeros((D_MODEL, N_SMALL - 2 * GLA_RANK - 2 * SSD_HEADS), w_in.dtype)
    small = jnp.concatenate([lrf, lrb, dtr, pad], axis=1).astype(BF16)
    return main, small


def _layer(h, norm_mix_g, w_in, gla_fw2_f, gla_fb_f, gla_fw2_b, gla_fb_b, gla_norm_g, conv_w, conv_b,
           dt_bias_f, dt_bias_b, a_log_f, a_log_b, ssd_d, ssd_norm_g, w_up_gla, w_up_ssd, w_out,
           norm_ffn_g, w_router, b_router, w1, b1, w2, b2, out_norm_g):
    bsz, seq, _ = h.shape
    t_total = bsz * seq
    x2 = h.reshape(t_total, D_MODEL)
    w_main, w_small = _prep_w_in(w_in)
    fw2f, fw2b = gla_fw2_f.astype(BF16), gla_fw2_b.astype(BF16)
    mparams = _merge_params(gla_norm_g, ssd_d, ssd_norm_g, w_up_gla, w_up_ssd, w_out, norm_ffn_g, w_router, b_router)
    n_groups = TOKEN_GROUPS if bsz % TOKEN_GROUPS == 0 else 1
    gb = bsz // n_groups
    t = gb * seq
    tiles = _Tiles.choose(t, seq)
    out = None
    for grp in range(n_groups):
        row0 = grp * t
        proj, small = _inproj(x2, norm_mix_g[None, :], w_main, w_small, row0, t,
                              tm=tiles.inproj_rows, tn=tiles.inproj_cols)
        o_f, o_b = _gla(proj, small, fw2f, gla_fb_f[None, :], fw2b, gla_fb_b[None, :], gb, seq, tiles.scan_rows)
        xs_c, bm_c, cm_c = _conv(proj, conv_w, conv_b, seq, tiles.conv_rows)
        y_f, y_b = _ssd(xs_c, bm_c, cm_c, small, dt_bias_f, dt_bias_b, a_log_f, a_log_b, gb, seq, tiles.scan_rows)
        hres, xp, route, gates, counts8 = _merge(x2, row0, o_f, o_b, proj, y_f, y_b, xs_c, mparams,
                                                 tm=tiles.merge_rows)
        counts = counts8[0]
        padded = (counts + MOE_BM - 1) // MOE_BM * MOE_BM
        pend = jnp.cumsum(padded)
        pstart = (pend - padded).astype(I32)
        n_rows = t * TOP_K + N_EXPERTS * MOE_BM
        blk_row = jnp.arange(n_rows // MOE_BM, dtype=I32) * MOE_BM
        blk_e = jnp.minimum(jnp.sum(pend[None, :] <= blk_row[:, None], axis=1), N_EXPERTS - 1).astype(I32)
        nvalid = jnp.clip(pstart[blk_e] + counts[blk_e] - blk_row, 0, MOE_BM).astype(I32)
        n_used = (pend[-1:] // MOE_BM).astype(I32)
        top_e, rank = route[:, :TOP_K], route[:, TOP_K:2 * TOP_K]
        dest = jnp.sum(jnp.where(top_e[:, :, None] == jnp.arange(N_EXPERTS, dtype=I32), pstart, 0), axis=-1) + rank
        idx = (dest.T[:, None, :] + (jnp.arange(2, dtype=I32) * n_rows)[None, :, None]).reshape(1, 2 * TOP_K * t)
        xb = _sc_scatter_rows(xp.reshape(2 * t, SC_W), idx, 2 * n_rows).reshape(2, n_rows, SC_W)
        yb = _experts(xb, blk_e, nvalid, n_used, w1, b1[:, None, :], w2, b2[:, None, :])
        y_rows = _sc_gather_rows(yb.reshape(2 * n_rows, SC_W), idx).reshape(TOP_K, 2, t, SC_W)
        out = _combine(hres, gates, y_rows, out_norm_g, t_total, row0, out, tm=tiles.combine_rows)
    return out.reshape(bsz, seq, D_MODEL)


def kernel(x, norm_mix_g, w_in, gla_fw2_f, gla_fb_f, gla_fw2_b, gla_fb_b, gla_norm_g, conv_w, conv_b, dt_bias_f,
           dt_bias_b, a_log_f, a_log_b, ssd_d, ssd_norm_g, w_up_gla, w_up_ssd, w_out, norm_ffn_g, w_router,
           b_router, w1, b1, w2, b2, norm_final_g):
    assert x.shape[-1] == D_MODEL and norm_mix_g.shape[0] == 1
    return _layer(x, norm_mix_g[0], w_in[0], gla_fw2_f[0], gla_fb_f[0], gla_fw2_b[0], gla_fb_b[0], gla_norm_g[0],
                  conv_w[0], conv_b[0], dt_bias_f[0], dt_bias_b[0], a_log_f[0], a_log_b[0], ssd_d[0],
                  ssd_norm_g[0], w_up_gla[0], w_up_ssd[0], w_out[0], norm_ffn_g[0], w_router[0], b_router[0],
                  w1[0], b1[0], w2[0], b2[0], norm_final_g)
```

```python
import functools
from typing import NamedTuple

import jax
import jax.numpy as jnp
from jax import lax
from jax.experimental import pallas as pl
from jax.experimental.pallas import tpu as pltpu
from jax.experimental.pallas import tpu_sc as plsc

F32 = jnp.float32
BF16 = jnp.bfloat16
I32 = jnp.int32
U32 = jnp.uint32

D_MODEL = 1024
EPS = 1e-5
GLA_HEADS = 4
GLA_DK = 128
GLA_DV = 256
GLA_RANK = 16
GLA_TAU = 16.0
GLA_QK = GLA_HEADS * GLA_DK
GLA_V = GLA_HEADS * GLA_DV
SSD_DINNER = 2048
SSD_HEADDIM = 64
SSD_HEADS = 32
SSD_GROUPS = 4
SSD_HPG = 8
SSD_STATE = 128
SSD_CONV = 4
SSD_BC = SSD_GROUPS * SSD_STATE
SSD_GW = SSD_HPG * SSD_HEADDIM
N_EXPERTS = 32
TOP_K = 4
D_FF = 1024
SWIGLU_LIMIT = 7.0
SWIGLU_ALPHA = 1.702
CHUNK = 64
TOKEN_GROUPS = 2
MERGE_SUBTILES = 4
SSD_INNER = 2
MXU_LAG = 6

C_Z, C_XS, C_Q, C_K, C_V, C_R, C_GG, C_GS, C_B, C_C = 0, 2048, 4096, 4608, 5120, 6144, 7168, 8192, 9216, 9728
N_MAIN = 10240
N_SMALL = 128
S_LRF, S_LRB, S_DTF, S_DTB = 0, 16, 32, 64

VMEM_LIMIT = 56 * 1024 * 1024
MOE_BM = 512
SC_WIN = 128
SC_W = D_MODEL // 4


def _dot(a, b):
    return jnp.dot(a, b, preferred_element_type=F32)


def _dot_nt(a, b):
    return lax.dot_general(a, b, (((1,), (1,)), ((), ())), preferred_element_type=F32)


def _dot_tn(a, b):
    return lax.dot_general(a, b, (((0,), (0,)), ((), ())), preferred_element_type=F32)


def _split_bf16(x, n):
    parts = []
    r = x
    for _ in range(n):
        p = r.astype(BF16)
        parts.append(p)
        r = r - p.astype(F32)
    return parts


def _dot_exact_lhs(m_bf16, x, n):
    acc = None
    for p in _split_bf16(x, n):
        t = _dot(m_bf16, p)
        acc = t if acc is None else acc + t
    return acc


def _dot_exact_rhs(x, m_bf16, n):
    acc = None
    for p in _split_bf16(x, n):
        t = _dot(p, m_bf16)
        acc = t if acc is None else acc + t
    return acc


def _sigmoid(x):
    return 1.0 / (1.0 + jnp.exp2(x * (-1.4426950408889634)))


def _silu(x):
    return x * _sigmoid(x)


def _pack_bf16_pair(lo, hi):
    lo_b = lax.bitcast_convert_type(lo.astype(BF16).astype(F32), U32)
    hi_b = lax.bitcast_convert_type(hi.astype(BF16).astype(F32), U32)
    return (lo_b >> 16) | (hi_b & jnp.uint32(0xFFFF0000))


def _unpack_bf16_pair(u):
    lo = lax.bitcast_convert_type(u << 16, F32)
    hi = lax.bitcast_convert_type(u & jnp.uint32(0xFFFF0000), F32)
    return lo, hi


def _inproj_kernel(x_ref, g_ref, w_ref, ws_ref, o_ref, os_ref, xn_ref):
    @pl.when(pl.program_id(1) == 0)
    def _():
        x = x_ref[...]
        ms = jnp.mean(x * x, axis=-1, keepdims=True)
        xb = ((x * lax.rsqrt(ms + EPS)) * g_ref[...]).astype(BF16)
        xn_ref[...] = xb
        os_ref[...] = _dot(xb, ws_ref[...])

    o_ref[...] = _dot(xn_ref[...], w_ref[...]).astype(BF16)


def _inproj(x2, g, w_main, w_small, row0, t, tm, tn):
    blk0 = row0 // tm
    return pl.pallas_call(
        _inproj_kernel,
        grid=(t // tm, N_MAIN // tn),
        in_specs=[
            pl.BlockSpec((tm, D_MODEL), lambda i, j: (blk0 + i, 0)),
            pl.BlockSpec((1, D_MODEL), lambda i, j: (0, 0)),
            pl.BlockSpec((D_MODEL, tn), lambda i, j: (0, j)),
            pl.BlockSpec((D_MODEL, N_SMALL), lambda i, j: (0, 0)),
        ],
        out_specs=[
            pl.BlockSpec((tm, tn), lambda i, j: (i, j)),
            pl.BlockSpec((tm, N_SMALL), lambda i, j: (i, 0)),
        ],
        out_shape=[
            jax.ShapeDtypeStruct((t, N_MAIN), BF16),
            jax.ShapeDtypeStruct((t, N_SMALL), F32),
        ],
        scratch_shapes=[pltpu.VMEM((tm, D_MODEL), BF16)],
        compiler_params=pltpu.CompilerParams(
            dimension_semantics=("arbitrary", "arbitrary"), vmem_limit_bytes=VMEM_LIMIT),
        name="inproj",
    )(x2, g, w_main, w_small)


def _tri_masks(n):
    r = lax.broadcasted_iota(I32, (n, n), 0)
    c = lax.broadcasted_iota(I32, (n, n), 1)
    return r >= c, c >= r


class _GlaDir:
    def __init__(self, ins, outs, scratch, mask, lr_off, mid_row, last_row):
        self.q, self.k, self.v, self.sm, self.fw2, self.fb = ins
        self.o, self.st = outs
        self.b_s, self.qs_s, self.ks_s, self.kd_s, self.qe_s, self.p_s, self.u_s, self.el_s = scratch
        self.mask, self.lr_off, self.mid_row, self.last_row = mask, lr_off, mid_row, last_row


def _gla_decay(d, n_chunks):
    tri = jnp.where(d.mask, 1.0, 0.0).astype(BF16)
    lr = d.sm[:, d.lr_off:d.lr_off + GLA_RANK].astype(BF16)
    xg = _dot(lr, d.fw2[...]) + d.fb[...]
    d.b_s[...] = (jnp.minimum(xg, 0.0) - jnp.log(1.0 + jnp.exp(-jnp.abs(xg)))) * (1.0 / GLA_TAU)
    for c in range(n_chunks):
        rows = slice(c * CHUNK, (c + 1) * CHUNK)
        d.b_s[rows, :] = _dot_exact_lhs(tri, d.b_s[rows, :], 2)


def _gla_scale(d, c):
    rows = slice(c * CHUNK, (c + 1) * CHUNK)
    b = d.b_s[rows, :]
    b_mid = b[d.mid_row:d.mid_row + 1, :]
    b_last = b[d.last_row:d.last_row + 1, :]
    q = d.q[rows, :].astype(F32) * (GLA_DK ** -0.5)
    k = d.k[rows, :].astype(F32)
    d.qs_s[rows, :] = (q * jnp.exp(b - b_mid)).astype(BF16)
    d.ks_s[rows, :] = (k * jnp.exp(b_mid - b)).astype(BF16)
    d.kd_s[rows, :] = (k * jnp.exp(b_last - b)).astype(BF16)
    d.qe_s[rows, :] = (q * jnp.exp(b)).astype(BF16)
    d.el_s[c:c + 1, :] = jnp.exp(b_last)


def _gla_local(units):
    def score(u):
        d, c, h = u
        rows, ks_ = slice(c * CHUNK, (c + 1) * CHUNK), slice(h * GLA_DK, (h + 1) * GLA_DK)
        return _dot_nt(d.qs_s[rows, ks_], d.ks_s[rows, ks_])

    def finish(u, s):
        d, c, h = u
        rows, ks_ = slice(c * CHUNK, (c + 1) * CHUNK), slice(h * GLA_DK, (h + 1) * GLA_DK)
        vs_ = slice(h * GLA_DV, (h + 1) * GLA_DV)
        d.p_s[c, h] = jnp.where(d.mask, s, 0.0).astype(BF16)
        d.u_s[c, h] = _dot_tn(d.kd_s[rows, ks_], d.v[rows, vs_])

    pending = []
    for u in units:
        pending.append((u, score(u)))
        if len(pending) > MXU_LAG:
            finish(*pending.pop(0))
    for item in pending:
        finish(*item)


def _gla_carry(d, c):
    rows = slice(c * CHUNK, (c + 1) * CHUNK)
    for h in range(GLA_HEADS):
        ks_ = slice(h * GLA_DK, (h + 1) * GLA_DK)
        vs_ = slice(h * GLA_DV, (h + 1) * GLA_DV)
        st = d.st[h]
        o = _dot(d.p_s[c, h], d.v[rows, vs_]) + _dot(d.qe_s[rows, ks_], st.astype(BF16))
        d.o[rows, vs_] = o.astype(d.o.dtype)
        e_col = jnp.transpose(jnp.broadcast_to(d.el_s[c:c + 1, ks_], (8, GLA_DK)))[:, 0:1]
        d.st[h] = st * e_col + d.u_s[c, h]


def _gla_kernel(*refs, n_chunks):
    ins_f, ins_b, (fw2f, fbf, fw2b, fbb), (of_ref, ob_ref, stf, stb) = refs[0:4], refs[4:8], refs[8:12], refs[12:16]
    scr_f, scr_b = refs[16:24], refs[24:32]

    @pl.when(pl.program_id(1) == 0)
    def _():
        stf[...] = jnp.zeros_like(stf)
        stb[...] = jnp.zeros_like(stb)

    lower, upper = _tri_masks(CHUNK)
    fwd = _GlaDir((*ins_f, fw2f, fbf), (of_ref, stf), scr_f, lower, S_LRF, CHUNK // 2, CHUNK - 1)
    bwd = _GlaDir((*ins_b, fw2b, fbb), (ob_ref, stb), scr_b, upper, S_LRB, CHUNK // 2 - 1, 0)
    _gla_decay(fwd, n_chunks)
    _gla_decay(bwd, n_chunks)
    for c in range(n_chunks):
        _gla_scale(fwd, c)
        _gla_scale(bwd, c)
    _gla_local([(d, c, h) for c in range(n_chunks) for d in (fwd, bwd) for h in range(GLA_HEADS)])
    for i in range(n_chunks):
        _gla_carry(fwd, i)
        _gla_carry(bwd, n_chunks - 1 - i)


def _gla(proj, small, fw2f, fbf, fw2b, fbb, bsz, seq, lb):
    t = bsz * seq
    nb = seq // lb

    def fwd(w, col):
        return pl.BlockSpec((lb, w), lambda b, n: (b * nb + n, col))

    def bwd(w, col):
        return pl.BlockSpec((lb, w), lambda b, n: (b * nb + nb - 1 - n, col))

    const = lambda shape: pl.BlockSpec(shape, lambda b, n: (0, 0))
    per_dir_scratch = ([pltpu.VMEM((lb, GLA_QK), F32)] + [pltpu.VMEM((lb, GLA_QK), BF16)] * 4
                       + [pltpu.VMEM((lb // CHUNK, GLA_HEADS, CHUNK, CHUNK), BF16),
                          pltpu.VMEM((lb // CHUNK, GLA_HEADS, GLA_DK, GLA_DV), F32),
                          pltpu.VMEM((lb // CHUNK, GLA_QK), F32)])
    return pl.pallas_call(
        functools.partial(_gla_kernel, n_chunks=lb // CHUNK),
        grid=(bsz, nb),
        in_specs=[
            fwd(GLA_QK, C_Q // GLA_QK), fwd(GLA_QK, C_K // GLA_QK), fwd(GLA_V, C_V // GLA_V), fwd(N_SMALL, 0),
            bwd(GLA_QK, C_Q // GLA_QK), bwd(GLA_QK, C_K // GLA_QK), bwd(GLA_V, C_V // GLA_V), bwd(N_SMALL, 0),
            const((GLA_RANK, GLA_QK)), const((1, GLA_QK)), const((GLA_RANK, GLA_QK)), const((1, GLA_QK)),
        ],
        out_specs=[fwd(GLA_V, 0), bwd(GLA_V, 0)],
        out_shape=[jax.ShapeDtypeStruct((t, GLA_V), BF16)] * 2,
        scratch_shapes=[pltpu.VMEM((GLA_HEADS, GLA_DK, GLA_DV), F32)] * 2 + per_dir_scratch * 2,
        compiler_params=pltpu.CompilerParams(
            dimension_semantics=("arbitrary", "arbitrary"), vmem_limit_bytes=VMEM_LIMIT),
        name="gla_scan",
    )(proj, proj, proj, small, proj, proj, proj, small, fw2f, fbf, fw2b, fbb)


HALO = 16


CONV_SUB = 256
CONV_COLS = 512


def _conv_taps(x, xm1, xp1, xp2, w, b):
    return _silu(xm1 * w[0:1, :] + x * w[1:2, :] + xp1 * w[2:3, :] + xp2 * w[3:4, :] + b)


def _shift_matrix(n):
    r = lax.broadcasted_iota(I32, (3 * n, n), 0)
    c = lax.broadcasted_iota(I32, (3 * n, n), 1)
    src = jnp.where(r < n, r - 1, jnp.where(r < 2 * n, r - n + 1, r - 2 * n + 2))
    return jnp.where(c == src, 1.0, 0.0).astype(BF16)


def _conv_one(x_ref, p_ref, n_ref, w_ref, b_ref, o_ref, shift, has_prev, has_next):
    rb, wd = x_ref.shape
    w, b = w_ref[...], b_ref[...]
    for s in range(rb // CONV_SUB):
        rows = slice(s * CONV_SUB, (s + 1) * CONV_SUB)
        for c0 in range(0, wd, CONV_COLS):
            cols = slice(c0, c0 + CONV_COLS)
            xb = x_ref[rows, cols]
            sx = _dot(shift, xb)
            y = _conv_taps(xb.astype(F32), sx[0:CONV_SUB], sx[CONV_SUB:2 * CONV_SUB], sx[2 * CONV_SUB:],
                           w[:, cols], b[:, cols])
            o_ref[rows, cols] = y.astype(o_ref.dtype)
    prev = jnp.where(has_prev, p_ref[HALO - 1:HALO, :].astype(F32), 0.0)
    nxt = jnp.where(has_next, n_ref[0:2, :].astype(F32), 0.0)
    row = lax.broadcasted_iota(I32, (HALO, wd), 0)
    head = x_ref[0:2 * HALO, :].astype(F32)
    xm1 = jnp.where(row == 0, prev, pltpu.roll(head, 1, 0)[0:HALO])
    o_ref[0:HALO, :] = _conv_taps(head[0:HALO], xm1, pltpu.roll(head, 2 * HALO - 1, 0)[0:HALO],
                                  pltpu.roll(head, 2 * HALO - 2, 0)[0:HALO], w, b).astype(o_ref.dtype)
    tail = x_ref[rb - 2 * HALO:rb, :].astype(F32)
    xp1 = jnp.where(row == HALO - 1, nxt[0:1, :], pltpu.roll(tail, 2 * HALO - 1, 0)[HALO:])
    xp2 = jnp.where(row == HALO - 2, nxt[0:1, :],
                    jnp.where(row == HALO - 1, nxt[1:2, :], pltpu.roll(tail, 2 * HALO - 2, 0)[HALO:]))
    o_ref[rb - HALO:rb, :] = _conv_taps(tail[HALO:], pltpu.roll(tail, 1, 0)[HALO:], xp1, xp2, w, b).astype(o_ref.dtype)
    for s in range(1, rb // CONV_SUB):
        e = s * CONV_SUB
        win = x_ref[e - 2 * HALO:e + 2 * HALO, :].astype(F32)
        mid = slice(HALO, 3 * HALO)
        o_ref[e - HALO:e + HALO, :] = _conv_taps(
            win[mid], pltpu.roll(win, 1, 0)[mid], pltpu.roll(win, 4 * HALO - 1, 0)[mid],
            pltpu.roll(win, 4 * HALO - 2, 0)[mid], w, b).astype(o_ref.dtype)


def _conv_kernel(xs, xsp, xsn, bm, bmp, bmn, cm, cmp_, cmn, wx, bx, wb, bb, wc, bc, oxs, obm, ocm, *, rb, seq):
    t0 = pl.program_id(0) * rb
    has_prev = (t0 % seq) != 0
    has_next = ((t0 + rb) % seq) != 0
    shift = _shift_matrix(CONV_SUB)
    _conv_one(xs, xsp, xsn, wx, bx, oxs, shift, has_prev, has_next)
    _conv_one(bm, bmp, bmn, wb, bb, obm, shift, has_prev, has_next)
    _conv_one(cm, cmp_, cmn, wc, bc, ocm, shift, has_prev, has_next)


def _conv(proj, conv_w, conv_b, seq, rb):
    t = proj.shape[0]
    nh = t // HALO
    per = rb // HALO

    def trio(w, col):
        cb = col // w
        return [
            pl.BlockSpec((rb, w), lambda i: (i, cb)),
            pl.BlockSpec((HALO, w), lambda i: (jnp.maximum(i * per - 1, 0), cb)),
            pl.BlockSpec((HALO, w), lambda i: (jnp.minimum((i + 1) * per, nh - 1), cb)),
        ]

    def wspecs(w):
        return [pl.BlockSpec((SSD_CONV, w), lambda i: (0, 0)), pl.BlockSpec((1, w), lambda i: (0, 0))]

    wx, wb, wc = conv_w[:, :SSD_DINNER], conv_w[:, SSD_DINNER:SSD_DINNER + SSD_BC], conv_w[:, SSD_DINNER + SSD_BC:]
    bx, bb, bc = (conv_b[None, :SSD_DINNER], conv_b[None, SSD_DINNER:SSD_DINNER + SSD_BC],
                  conv_b[None, SSD_DINNER + SSD_BC:])
    return pl.pallas_call(
        functools.partial(_conv_kernel, rb=rb, seq=seq),
        grid=(t // rb,),
        in_specs=trio(SSD_DINNER, C_XS) + trio(SSD_BC, C_B) + trio(SSD_BC, C_C)
        + wspecs(SSD_DINNER) + wspecs(SSD_BC) + wspecs(SSD_BC),
        out_specs=[pl.BlockSpec((rb, SSD_DINNER), lambda i: (i, 0)),
                   pl.BlockSpec((rb, SSD_BC), lambda i: (i, 0)),
                   pl.BlockSpec((rb, SSD_BC), lambda i: (i, 0))],
        out_shape=[jax.ShapeDtypeStruct((t, SSD_DINNER), BF16),
                   jax.ShapeDtypeStruct((t, SSD_BC), BF16),
                   jax.ShapeDtypeStruct((t, SSD_BC), BF16)],
        compiler_params=pltpu.CompilerParams(dimension_semantics=("arbitrary",), vmem_limit_bytes=VMEM_LIMIT),
        name="ssd_conv",
    )(proj, proj, proj, proj, proj, proj, proj, proj, proj, wx, bx, wb, bb, wc, bc)


def _softplus(x):
    return jnp.maximum(x, 0.0) + jnp.log(1.0 + jnp.exp(-jnp.abs(x)))


class _SsdDir:
    def __init__(self, xs, bm, cm, sm, dtb_row, alog_row, y, st, reverse, dt_off, last_row):
        self.xs, self.bm, self.cm, self.sm, self.dtb_row, self.alog_row = xs, bm, cm, sm, dtb_row, alog_row
        self.y, self.st, self.reverse, self.dt_off, self.last_row = y, st, reverse, dt_off, last_row


def _ssd_chunks(work):
    hh = lax.broadcasted_iota(I32, (SSD_HEADS, SSD_GW), 0)
    cc = lax.broadcasted_iota(I32, (SSD_HEADS, SSD_GW), 1)
    lane = lax.broadcasted_iota(I32, (CHUNK, 2 * SSD_HEADDIM), 1)
    row2 = lax.broadcasted_iota(I32, (CHUNK, 2 * SSD_HEADDIM), 0)
    left = lane < SSD_HEADDIM
    col2 = jnp.where(left, lane, lane - SSD_HEADDIM)
    lower, upper = _tri_masks(CHUNK)

    pre = []
    for d, c0 in work:
        rows = pl.ds(c0, CHUNK)
        tri = jnp.where(upper if d.reverse else lower, 1.0, 0.0).astype(BF16)
        a_row = -jnp.exp(d.alog_row[...])
        dt = _softplus(d.sm[rows, d.dt_off:d.dt_off + SSD_HEADS] + d.dtb_row[...])
        pre.append((dt, _dot_exact_lhs(tri, dt * a_row, 3)))
    heads = []
    for (d, c0), (dt, cum) in zip(work, pre):
        total = cum[d.last_row:d.last_row + 1, :]
        to_end = jnp.exp(total - cum) * dt
        fac = jnp.concatenate([to_end, jnp.exp(cum)], axis=0).astype(BF16)
        e_tot = jnp.broadcast_to(jnp.exp(total), (8, SSD_HEADS))
        heads.append((cum, jnp.transpose(cum), jnp.transpose(dt), fac, e_tot))

    units = [(w, g) for w in range(len(work)) for g in range(SSD_GROUPS)]
    groups = {}
    for w, g in units:
        d, c0 = work[w]
        rows, ns = pl.ds(c0, CHUNK), slice(g * SSD_STATE, (g + 1) * SSD_STATE)
        fac, e_tot = heads[w][3], heads[w][4]
        expand = jnp.where(cc // SSD_HEADDIM + g * SSD_HPG == hh, 1.0, 0.0).astype(BF16)
        fac_x = _dot(fac, expand)
        et_x = _dot_exact_rhs(e_tot, expand, 2)[0:1]
        groups[w, g] = (fac_x, et_x, _dot_nt(d.cm[rows, ns], d.bm[rows, ns]))

    for w, g in units:
        d, c0 = work[w]
        rows, ns, cs = pl.ds(c0, CHUNK), slice(g * SSD_STATE, (g + 1) * SSD_STATE), slice(g * SSD_GW, (g + 1) * SSD_GW)
        cum, cum_t, dt_t = heads[w][0:3]
        fac_x, _, cb = groups[w, g]
        mask2 = (col2 >= row2) if d.reverse else (row2 >= col2)
        y_inter = _dot(d.cm[rows, ns], d.st[g].astype(BF16))
        cb2 = jnp.concatenate([cb, cb], axis=1)
        parts = []
        for p in range(SSD_HPG // 2):
            h0 = g * SSD_HPG + 2 * p
            ps = slice(h0 * SSD_HEADDIM, (h0 + 2) * SSD_HEADDIM)
            col = jnp.where(left, cum[:, h0:h0 + 1], cum[:, h0 + 1:h0 + 2])
            rowv = jnp.concatenate([cum_t[h0:h0 + 1, :], cum_t[h0 + 1:h0 + 2, :]], axis=1)
            dtv = jnp.concatenate([dt_t[h0:h0 + 1, :], dt_t[h0 + 1:h0 + 2, :]], axis=1)
            decay = jnp.exp(jnp.where(mask2, col - rowv, -jnp.inf))
            wgt = (cb2 * decay * dtv).astype(BF16)
            xp = d.xs[rows, ps]
            zero = jnp.zeros_like(xp)
            xbd = jnp.concatenate([jnp.where(left, xp, zero), jnp.where(left, zero, xp)], axis=0)
            parts.append(_dot(wgt, xbd))
        y = jnp.concatenate(parts, axis=1) + y_inter * fac_x[CHUNK:2 * CHUNK]
        d.y[rows, cs] = y.astype(d.y.dtype)

    def increment(w, g):
        d, c0 = work[w]
        rows, ns, cs = pl.ds(c0, CHUNK), slice(g * SSD_STATE, (g + 1) * SSD_STATE), slice(g * SSD_GW, (g + 1) * SSD_GW)
        return _dot_tn(d.bm[rows, ns], d.xs[rows, cs] * groups[w, g][0][0:CHUNK].astype(BF16))

    def update(w, g, inc):
        d = work[w][0]
        d.st[g] = d.st[g] * groups[w, g][1] + inc

    pending = None
    for w, g in units:
        inc = increment(w, g)
        if pending is not None:
            update(*pending)
        pending = (w, g, inc)
    update(*pending)


def _ssd_kernel(*refs, n_chunks, inner):
    ins, (dbf_r, dbb_r, alf_r, alb_r) = refs[:8 * inner], refs[8 * inner:8 * inner + 4]
    (yf_ref, yb_ref), states = refs[8 * inner + 4:8 * inner + 6], refs[8 * inner + 6:]

    @pl.when(pl.program_id(1) == 0)
    def _():
        for st in states:
            st[...] = jnp.zeros_like(st)

    fwd = [_SsdDir(*ins[8 * e:8 * e + 4], dbf_r, alf_r, yf_ref.at[e], states[2 * e], False, S_DTF, CHUNK - 1)
           for e in range(inner)]
    bwd = [_SsdDir(*ins[8 * e + 4:8 * e + 8], dbb_r, alb_r, yb_ref.at[e], states[2 * e + 1], True, S_DTB, 0)
           for e in range(inner)]

    def body(i, carry):
        c_f = pl.multiple_of(i * CHUNK, CHUNK)
        c_b = pl.multiple_of((n_chunks - 1 - i) * CHUNK, CHUNK)
        _ssd_chunks([(d, c_f) for d in fwd] + [(d, c_b) for d in bwd])
        return carry

    lax.fori_loop(0, n_chunks, body, 0)


def _ssd(xs_c, bm_c, cm_c, small, dtb_f, dtb_b, alog_f, alog_b, bsz, seq, lb):
    t = bsz * seq
    nb = seq // lb
    inner = SSD_INNER if bsz % SSD_INNER == 0 else 1

    def specs(e, reverse):
        idx = lambda p, n: (p * inner + e) * nb + (nb - 1 - n if reverse else n)
        return [pl.BlockSpec((lb, w), lambda p, n: (idx(p, n), 0)) for w in (SSD_DINNER, SSD_BC, SSD_BC, N_SMALL)]

    row = pl.BlockSpec((1, SSD_HEADS), lambda p, n: (0, 0))
    args = (xs_c, bm_c, cm_c, small)
    y_f, y_b = pl.pallas_call(
        functools.partial(_ssd_kernel, n_chunks=lb // CHUNK, inner=inner),
        grid=(bsz // inner, nb),
        in_specs=[s for e in range(inner) for rev in (False, True) for s in specs(e, rev)] + [row] * 4,
        out_specs=[pl.BlockSpec((None, inner, lb, SSD_DINNER), lambda p, n: (p, 0, n, 0)),
                   pl.BlockSpec((None, inner, lb, SSD_DINNER), lambda p, n: (p, 0, nb - 1 - n, 0))],
        out_shape=[jax.ShapeDtypeStruct((bsz // inner, inner, seq, SSD_DINNER), BF16)] * 2,
        scratch_shapes=[pltpu.VMEM((SSD_GROUPS, SSD_STATE, SSD_GW), F32)] * (2 * inner),
        compiler_params=pltpu.CompilerParams(
            dimension_semantics=("arbitrary", "arbitrary"), vmem_limit_bytes=VMEM_LIMIT),
        name="ssd_scan",
    )(*(args * (2 * inner)), dtb_f[None, :], dtb_b[None, :], alog_f[None, :], alog_b[None, :])
    return y_f.reshape(t, SSD_DINNER), y_b.reshape(t, SSD_DINNER)


def _merge_kernel(x_ref, of_ref, ob_ref, r_ref, gg_ref, yf_ref, yb_ref, xs_ref, z_ref, gs_ref,
                  gng_ref, dsk_ref, sng_ref, wug_ref, wus_ref, wo_ref, nfg_ref, wr_ref, br_ref,
                  h_ref, xp_ref, route_ref, gate_ref, cnt_out_ref, cnt_ref):
    @pl.when(pl.program_id(0) == 0)
    def _():
        cnt_ref[...] = jnp.zeros_like(cnt_ref)

    tm = x_ref.shape[0]
    subs = [slice(s * (tm // MERGE_SUBTILES), (s + 1) * (tm // MERGE_SUBTILES)) for s in range(MERGE_SUBTILES)]

    def gla_branch(rows):
        o = of_ref[rows, :].astype(F32) + ob_ref[rows, :].astype(F32)
        gng = gng_ref[...]
        o_parts = []
        for h in range(GLA_HEADS):
            oh = o[:, h * GLA_DV:(h + 1) * GLA_DV]
            oh = oh * lax.rsqrt(jnp.mean(oh * oh, axis=-1, keepdims=True) + EPS)
            o_parts.append(oh * gng)
        return (jnp.concatenate(o_parts, axis=1) * _silu(r_ref[rows, :]).astype(F32)).astype(BF16)

    def ssd_branch(rows):
        y = (yf_ref[rows, :].astype(F32) + yb_ref[rows, :].astype(F32)
             + dsk_ref[...] * xs_ref[rows, :].astype(F32))
        y = y * _silu(z_ref[rows, :]).astype(F32)
        sng = sng_ref[...]
        y_parts = []
        for g in range(SSD_GROUPS):
            yg = y[:, g * SSD_GW:(g + 1) * SSD_GW]
            yg = yg * lax.rsqrt(jnp.mean(yg * yg, axis=-1, keepdims=True) + EPS)
            y_parts.append(yg * sng[:, g * SSD_GW:(g + 1) * SSD_GW])
        return jnp.concatenate(y_parts, axis=1).astype(BF16)

    up_g = [_dot(gla_branch(rows), wug_ref[...]) for rows in subs]
    up_s = [_dot(ssd_branch(rows), wus_ref[...]) for rows in subs]
    mix = [(_sigmoid(gg_ref[rows, :]).astype(F32) * ug + _sigmoid(gs_ref[rows, :]).astype(F32) * us).astype(BF16)
           for rows, ug, us in zip(subs, up_g, up_s)]
    hs = [x_ref[rows, :] + _dot(m, wo_ref[...]) for rows, m in zip(subs, mix)]
    logit_parts = []
    for rows, h in zip(subs, hs):
        h_ref[rows, :] = h
        hn = (h * lax.rsqrt(jnp.mean(h * h, axis=-1, keepdims=True) + EPS)) * nfg_ref[...]
        words = _pack_bf16_pair(hn[:, :D_MODEL // 2], hn[:, D_MODEL // 2:])
        xp_ref[0, rows, :] = words[:, :SC_W]
        xp_ref[1, rows, :] = words[:, SC_W:]
        hn_hi, hn_lo = _split_bf16(hn, 2)
        logit_parts.append(_dot(hn_hi, wr_ref[0]) + (_dot(hn_hi, wr_ref[1]) + _dot(hn_lo, wr_ref[0])))
    logits = jnp.concatenate(logit_parts, axis=0) + br_ref[...]
    lane = lax.broadcasted_iota(I32, (tm, N_EXPERTS), 1).astype(F32)
    work = logits
    idxs, vals = [], []
    for _ in range(TOP_K):
        m = jnp.max(work, axis=-1, keepdims=True)
        idx = jnp.min(jnp.where(work == m, lane, float(N_EXPERTS)), axis=-1, keepdims=True)
        idxs.append(idx)
        vals.append(m)
        work = jnp.where(lane == idx, -jnp.inf, work)
    exps = [jnp.exp(v - vals[0]) for v in vals]
    denom = exps[0] + exps[1] + exps[2] + exps[3]
    gates = [e / denom for e in exps]
    sel = jnp.zeros((tm, N_EXPERTS), F32)
    for idx in idxs:
        sel = sel + jnp.where(lane == idx, 1.0, 0.0)
    rr = lax.broadcasted_iota(I32, (tm, tm), 0)
    cc = lax.broadcasted_iota(I32, (tm, tm), 1)
    strict = jnp.where(rr > cc, 1.0, 0.0).astype(BF16)
    pos = _dot(strict, sel.astype(BF16)) + cnt_ref[0:1, :]
    ranks = [jnp.sum(jnp.where(lane == idx, pos, 0.0), axis=-1, keepdims=True).astype(I32) for idx in idxs]
    cnt_new = cnt_ref[0:1, :] + jnp.sum(sel, axis=0, keepdims=True)
    cnt_ref[...] = jnp.broadcast_to(cnt_new, cnt_ref.shape)
    cnt_out_ref[...] = jnp.broadcast_to(cnt_new, cnt_ref.shape).astype(I32)
    lane128 = lax.broadcasted_iota(I32, (tm, 128), 1)
    route = jnp.zeros((tm, 128), I32)
    gate_o = jnp.zeros((tm, 128), F32)
    for k in range(TOP_K):
        route = jnp.where(lane128 == k, idxs[k].astype(I32), route)
        route = jnp.where(lane128 == TOP_K + k, ranks[k], route)
        gate_o = jnp.where(lane128 == k, gates[k], gate_o)
    route_ref[...] = route
    gate_ref[...] = gate_o


def _merge_params(gla_norm_g, ssd_d, ssd_norm_g, w_up_gla, w_up_ssd, w_out, norm_ffn_g, w_router, b_router):
    d_skip = jnp.repeat(ssd_d, SSD_HEADDIM)[None, :]
    return [gla_norm_g[None, :], d_skip, ssd_norm_g[None, :], w_up_gla.astype(BF16), w_up_ssd.astype(BF16),
            w_out.astype(BF16), norm_ffn_g[None, :], jnp.stack(_split_bf16(w_router, 2)), b_router[None, :]]


def _merge(x2, row0, o_f, o_b, proj, y_f, y_b, xs_c, params, tm):
    t = o_f.shape[0]
    blk0 = row0 // tm
    rowblk = lambda w, col=0: pl.BlockSpec((tm, w), lambda i: (i, col))
    const = lambda a: pl.BlockSpec(a.shape, lambda i: (0,) * a.ndim, pipeline_mode=pl.Buffered(1))
    return pl.pallas_call(
        _merge_kernel,
        grid=(t // tm,),
        in_specs=[pl.BlockSpec((tm, D_MODEL), lambda i: (blk0 + i, 0)),
                  rowblk(GLA_V), rowblk(GLA_V), rowblk(GLA_V, C_R // GLA_V),
                  rowblk(D_MODEL, C_GG // D_MODEL), rowblk(SSD_DINNER), rowblk(SSD_DINNER), rowblk(SSD_DINNER),
                  rowblk(SSD_DINNER, C_Z // SSD_DINNER), rowblk(D_MODEL, C_GS // D_MODEL)]
        + [const(p) for p in params],
        out_specs=[rowblk(D_MODEL), pl.BlockSpec((2, tm, SC_W), lambda i: (0, i, 0)), rowblk(128), rowblk(128),
                   pl.BlockSpec((8, N_EXPERTS), lambda i: (0, 0))],
        out_shape=[jax.ShapeDtypeStruct((t, D_MODEL), F32), jax.ShapeDtypeStruct((2, t, SC_W), U32),
                   jax.ShapeDtypeStruct((t, 128), I32), jax.ShapeDtypeStruct((t, 128), F32),
                   jax.ShapeDtypeStruct((8, N_EXPERTS), I32)],
        scratch_shapes=[pltpu.VMEM((8, N_EXPERTS), F32)],
        compiler_params=pltpu.CompilerParams(dimension_semantics=("arbitrary",), vmem_limit_bytes=VMEM_LIMIT),
        name="merge_router",
    )(x2, o_f, o_b, proj, proj, y_f, y_b, xs_c, proj, proj, *params)


def _sc_mesh():
    return plsc.VectorSubcoreMesh(core_axis_name="c", subcore_axis_name="s")


def _sc_scatter_rows(x, idx, n_out):
    n, m = x.shape[0], idx.shape[1]
    n_win = n // SC_WIN

    @functools.partial(pl.kernel, out_type=jax.ShapeDtypeStruct((n_out, SC_W), x.dtype), mesh=_sc_mesh())
    def scatter(x_hbm, i_hbm, o_hbm):
        def body(x_vmem, i_vmem):
            pltpu.sync_copy(x_vmem, o_hbm.at[i_vmem.at[0]])

        pltpu.emit_pipeline(
            body, grid=(m // SC_WIN,),
            in_specs=[pl.BlockSpec((SC_WIN, SC_W), lambda i: (i % n_win, 0)),
                      pl.BlockSpec((1, SC_WIN), lambda i: (0, i))],
            out_specs=[], core_axis_name=("c", "s"), dimension_semantics=(pltpu.PARALLEL,),
        )(x_hbm, i_hbm)

    return scatter(x, idx)


def _sc_gather_rows(table, idx):
    m = idx.shape[1]

    @functools.partial(pl.kernel, out_type=jax.ShapeDtypeStruct((m, SC_W), table.dtype), mesh=_sc_mesh())
    def gather(t_hbm, i_hbm, o_hbm):
        def body(i_vmem, o_vmem):
            pltpu.sync_copy(t_hbm.at[i_vmem.at[0]], o_vmem)

        pltpu.emit_pipeline(
            body, grid=(m // SC_WIN,),
            in_specs=[pl.BlockSpec((1, SC_WIN), lambda i: (0, i))],
            out_specs=[pl.BlockSpec((SC_WIN, SC_W), lambda i: (i, 0))],
            core_axis_name=("c", "s"), dimension_semantics=(pltpu.PARALLEL,),
        )(i_hbm, o_hbm)

    return gather(table, idx)


def _expert_kernel(blk_e_ref, nvalid_ref, nused_ref, x_ref, w1_ref, b1_ref, w2_ref, b2_ref, o_ref, w1_s, w2_s):
    del nused_ref
    i = pl.program_id(0)
    nvalid = nvalid_ref[i]

    @pl.when((nvalid > 0) & ((i == 0) | (blk_e_ref[i] != blk_e_ref[jnp.maximum(i - 1, 0)])))
    def _():
        w1_s[...] = w1_ref[0].astype(BF16)
        w2_s[...] = w2_ref[0].astype(BF16)

    @pl.when(nvalid > 0)
    def _():
        live = lax.broadcasted_iota(I32, (MOE_BM, SC_W), 0) < nvalid
        lo0, hi0 = _unpack_bf16_pair(jnp.where(live, x_ref[0], jnp.uint32(0)))
        lo1, hi1 = _unpack_bf16_pair(jnp.where(live, x_ref[1], jnp.uint32(0)))
        x = jnp.concatenate([lo0, lo1, hi0, hi1], axis=1).astype(BF16)
        hdn = _dot(x, w1_s[...]) + b1_ref[0]
        gate = jnp.minimum(hdn[:, :D_FF], SWIGLU_LIMIT)
        lin = jnp.clip(hdn[:, D_FF:], -SWIGLU_LIMIT, SWIGLU_LIMIT)
        act = gate * _sigmoid(SWIGLU_ALPHA * gate) * (lin + 1.0)
        y = _dot(act.astype(BF16), w2_s[...]) + b2_ref[0]
        words = _pack_bf16_pair(y[:, :D_MODEL // 2], y[:, D_MODEL // 2:])
        o_ref[0] = words[:, :SC_W]
        o_ref[1] = words[:, SC_W:]

    @pl.when(nvalid == 0)
    def _():
        o_ref[...] = jnp.zeros_like(o_ref)


def _experts(xb, blk_e, nvalid, n_used, w1, b1, w2, b2):
    n_rows = xb.shape[1]
    n_blocks = n_rows // MOE_BM
    xidx = lambda i, be, nv, nu: (0, jnp.minimum(i, nu[0] - 1), 0)
    eidx = lambda i, be, nv, nu: (be[i], 0, 0)
    return pl.pallas_call(
        _expert_kernel,
        grid_spec=pltpu.PrefetchScalarGridSpec(
            num_scalar_prefetch=3,
            grid=(n_blocks,),
            in_specs=[pl.BlockSpec((2, MOE_BM, SC_W), xidx),
                      pl.BlockSpec((1, D_MODEL, 2 * D_FF), eidx),
                      pl.BlockSpec((1, 1, 2 * D_FF), eidx),
                      pl.BlockSpec((1, D_FF, D_MODEL), eidx),
                      pl.BlockSpec((1, 1, D_MODEL), eidx)],
            out_specs=pl.BlockSpec((2, MOE_BM, SC_W), lambda i, be, nv, nu: (0, i, 0)),
            scratch_shapes=[pltpu.VMEM((D_MODEL, 2 * D_FF), BF16), pltpu.VMEM((D_FF, D_MODEL), BF16)],
        ),
        out_shape=jax.ShapeDtypeStruct((2, n_rows, SC_W), U32),
        compiler_params=pltpu.CompilerParams(dimension_semantics=("arbitrary",), vmem_limit_bytes=VMEM_LIMIT),
        name="moe_experts",
    )(blk_e, nvalid, n_used, xb, w1, b1, w2, b2)


def _combine_kernel(h_ref, gate_ref, y_ref, g_ref, *rest):
    o_ref = rest[-1]
    gates = gate_ref[...]
    acc = None
    for k in range(TOP_K):
        lo0, hi0 = _unpack_bf16_pair(y_ref[k, 0])
        lo1, hi1 = _unpack_bf16_pair(y_ref[k, 1])
        term = gates[:, k:k + 1] * jnp.concatenate([lo0, lo1, hi0, hi1], axis=1)
        acc = term if acc is None else acc + term
    h = h_ref[...] + acc
    o_ref[...] = (h * lax.rsqrt(jnp.mean(h * h, axis=-1, keepdims=True) + EPS)) * g_ref[...]


def _combine(h, gates, y_rows, norm_final_g, t_total, row0, prev, tm):
    t = h.shape[0]
    blk0 = row0 // tm
    in_specs = [pl.BlockSpec((tm, D_MODEL), lambda i: (i, 0)),
                pl.BlockSpec((tm, 128), lambda i: (i, 0)),
                pl.BlockSpec((TOP_K, 2, tm, SC_W), lambda i: (0, 0, i, 0)),
                pl.BlockSpec((1, D_MODEL), lambda i: (0, 0))]
    args = [h, gates, y_rows, norm_final_g[None, :]]
    if prev is not None:
        in_specs.append(pl.BlockSpec(memory_space=pl.ANY))
        args.append(prev)
    return pl.pallas_call(
        _combine_kernel,
        grid=(t // tm,),
        in_specs=in_specs,
        out_specs=pl.BlockSpec((tm, D_MODEL), lambda i: (blk0 + i, 0)),
        out_shape=jax.ShapeDtypeStruct((t_total, D_MODEL), F32),
        input_output_aliases={} if prev is None else {4: 0},
        compiler_params=pltpu.CompilerParams(dimension_semantics=("arbitrary",), vmem_limit_bytes=VMEM_LIMIT),
        name="moe_combine",
    )(*args)


def _pick(n, pref):
    b = min(pref, n)
    while n % b:
        b -= CHUNK
    return b


class _Tiles(NamedTuple):
    inproj_rows: int
    inproj_cols: int
    scan_rows: int
    conv_rows: int
    merge_rows: int
    combine_rows: int

    @classmethod
    def choose(cls, t, seq):
        return cls(inproj_rows=_pick(t, 2048), inproj_cols=1280, scan_rows=_pick(seq, 512),
                   conv_rows=_pick(seq, 512), merge_rows=_pick(t, 512), combine_rows=_pick(t, 1024))


def _prep_w_in(w_in):
    widths = (GLA_QK, GLA_QK, GLA_V, GLA_V, GLA_RANK, GLA_RANK, SSD_DINNER, SSD_DINNER, SSD_BC, SSD_BC,
              2 * SSD_HEADS, D_MODEL, D_MODEL)
    pts, acc = [], 0
    for w in widths[:-1]:
        acc += w
        pts.append(acc)
    q, k, v, r, lrf, lrb, z, xs, bm, cm, dtr, gg, gs = jnp.split(w_in, pts, axis=1)
    main = jnp.concatenate([z, xs, q, k, v, r, gg, gs, bm, cm], axis=1).astype(BF16)
    pad = jnp.zeros((D_MODEL, N_SMALL - 2 * GLA_RANK - 2 * SSD_HEADS), w_in.dtype)
    small = jnp.concatenate([lrf, lrb, dtr, pad], axis=1).astype(BF16)
    return main, small


def _layer(h, norm_mix_g, w_in, gla_fw2_f, gla_fb_f, gla_fw2_b, gla_fb_b, gla_norm_g, conv_w, conv_b,
           dt_bias_f, dt_bias_b, a_log_f, a_log_b, ssd_d, ssd_norm_g, w_up_gla, w_up_ssd, w_out,
           norm_ffn_g, w_router, b_router, w1, b1, w2, b2, out_norm_g):
    bsz, seq, _ = h.shape
    t_total = bsz * seq
    x2 = h.reshape(t_total, D_MODEL)
    w_main, w_small = _prep_w_in(w_in)
    fw2f, fw2b = gla_fw2_f.astype(BF16), gla_fw2_b.astype(BF16)
    mparams = _merge_params(gla_norm_g, ssd_d, ssd_norm_g, w_up_gla, w_up_ssd, w_out, norm_ffn_g, w_router, b_router)
    n_groups = TOKEN_GROUPS if bsz % TOKEN_GROUPS == 0 else 1
    gb = bsz // n_groups
    t = gb * seq
    tiles = _Tiles.choose(t, seq)
    out = None
    for grp in range(n_groups):
        row0 = grp * t
        proj, small = _inproj(x2, norm_mix_g[None, :], w_main, w_small, row0, t,
                              tm=tiles.inproj_rows, tn=tiles.inproj_cols)
        o_f, o_b = _gla(proj, small, fw2f, gla_fb_f[None, :], fw2b, gla_fb_b[None, :], gb, seq, tiles.scan_rows)
        xs_c, bm_c, cm_c = _conv(proj, conv_w, conv_b, seq, tiles.conv_rows)
        y_f, y_b = _ssd(xs_c, bm_c, cm_c, small, dt_bias_f, dt_bias_b, a_log_f, a_log_b, gb, seq, tiles.scan_rows)
        hres, xp, route, gates, counts8 = _merge(x2, row0, o_f, o_b, proj, y_f, y_b, xs_c, mparams,
                                                 tm=tiles.merge_rows)
        counts = counts8[0]
        padded = (counts + MOE_BM - 1) // MOE_BM * MOE_BM
        pend = jnp.cumsum(padded)
        pstart = (pend - padded).astype(I32)
        n_rows = t * TOP_K + N_EXPERTS * MOE_BM
        blk_row = jnp.arange(n_rows // MOE_BM, dtype=I32) * MOE_BM
        blk_e = jnp.minimum(jnp.sum(pend[None, :] <= blk_row[:, None], axis=1), N_EXPERTS - 1).astype(I32)
        nvalid = jnp.clip(pstart[blk_e] + counts[blk_e] - blk_row, 0, MOE_BM).astype(I32)
        n_used = (pend[-1:] // MOE_BM).astype(I32)
        top_e, rank = route[:, :TOP_K], route[:, TOP_K:2 * TOP_K]
        dest = jnp.sum(jnp.where(top_e[:, :, None] == jnp.arange(N_EXPERTS, dtype=I32), pstart, 0), axis=-1) + rank
        idx = (dest.T[:, None, :] + (jnp.arange(2, dtype=I32) * n_rows)[None, :, None]).reshape(1, 2 * TOP_K * t)
        xb = _sc_scatter_rows(xp.reshape(2 * t, SC_W), idx, 2 * n_rows).reshape(2, n_rows, SC_W)
        yb = _experts(xb, blk_e, nvalid, n_used, w1, b1[:, None, :], w2, b2[:, None, :])
        y_rows = _sc_gather_rows(yb.reshape(2 * n_rows, SC_W), idx).reshape(TOP_K, 2, t, SC_W)
        out = _combine(hres, gates, y_rows, out_norm_g, t_total, row0, out, tm=tiles.combine_rows)
    return out.reshape(bsz, seq, D_MODEL)


def kernel(x, norm_mix_g, w_in, gla_fw2_f, gla_fb_f, gla_fw2_b, gla_fb_b, gla_norm_g, conv_w, conv_b, dt_bias_f,
           dt_bias_b, a_log_f, a_log_b, ssd_d, ssd_norm_g, w_up_gla, w_up_ssd, w_out, norm_ffn_g, w_router,
           b_router, w1, b1, w2, b2, norm_final_g):
    assert x.shape[-1] == D_MODEL and norm_mix_g.shape[0] == 1
    return _layer(x, norm_mix_g[0], w_in[0], gla_fw2_f[0], gla_fb_f[0], gla_fw2_b[0], gla_fb_b[0], gla_norm_g[0],
                  conv_w[0], conv_b[0], dt_bias_f[0], dt_bias_b[0], a_log_f[0], a_log_b[0], ssd_d[0],
                  ssd_norm_g[0], w_up_gla[0], w_up_ssd[0], w_out[0], norm_ffn_g[0], w_router[0], b_router[0],
                  w1[0], b1[0], w2[0], b2[0], norm_final_g)
```

```python
import functools
from typing import NamedTuple

import jax
import jax.numpy as jnp
from jax import lax
from jax.experimental import pallas as pl
from jax.experimental.pallas import tpu as pltpu
from jax.experimental.pallas import tpu_sc as plsc

F32 = jnp.float32
BF16 = jnp.bfloat16
I32 = jnp.int32
U32 = jnp.uint32

D_MODEL = 1024
EPS = 1e-5
GLA_HEADS = 4
GLA_DK = 128
GLA_DV = 256
GLA_RANK = 16
GLA_TAU = 16.0
GLA_QK = GLA_HEADS * GLA_DK
GLA_V = GLA_HEADS * GLA_DV
SSD_DINNER = 2048
SSD_HEADDIM = 64
SSD_HEADS = 32
SSD_GROUPS = 4
SSD_HPG = 8
SSD_STATE = 128
SSD_CONV = 4
SSD_BC = SSD_GROUPS * SSD_STATE
SSD_GW = SSD_HPG * SSD_HEADDIM
N_EXPERTS = 32
TOP_K = 4
D_FF = 1024
SWIGLU_LIMIT = 7.0
SWIGLU_ALPHA = 1.702
CHUNK = 64
TOKEN_GROUPS = 2
MERGE_SUBTILES = 4
SSD_INNER = 2
MXU_LAG = 6

C_Z, C_XS, C_Q, C_K, C_V, C_R, C_GG, C_GS, C_B, C_C = 0, 2048, 4096, 4608, 5120, 6144, 7168, 8192, 9216, 9728
N_MAIN = 10240
N_SMALL = 128
S_LRF, S_LRB, S_DTF, S_DTB = 0, 16, 32, 64

VMEM_LIMIT = 56 * 1024 * 1024
MOE_BM = 512
SC_WIN = 128
SC_W = D_MODEL // 4


def _dot(a, b):
    return jnp.dot(a, b, preferred_element_type=F32)


def _dot_nt(a, b):
    return lax.dot_general(a, b, (((1,), (1,)), ((), ())), preferred_element_type=F32)


def _dot_tn(a, b):
    return lax.dot_general(a, b, (((0,), (0,)), ((), ())), preferred_element_type=F32)


def _split_bf16(x, n):
    parts = []
    r = x
    for _ in range(n):
        p = r.astype(BF16)
        parts.append(p)
        r = r - p.astype(F32)
    return parts


def _dot_exact_lhs(m_bf16, x, n):
    acc = None
    for p in _split_bf16(x, n):
        t = _dot(m_bf16, p)
        acc = t if acc is None else acc + t
    return acc


def _dot_exact_rhs(x, m_bf16, n):
    acc = None
    for p in _split_bf16(x, n):
        t = _dot(p, m_bf16)
        acc = t if acc is None else acc + t
    return acc


def _sigmoid(x):
    return 1.0 / (1.0 + jnp.exp2(x * (-1.4426950408889634)))


def _silu(x):
    return x * _sigmoid(x)


def _pack_bf16_pair(lo, hi):
    lo_b = lax.bitcast_convert_type(lo.astype(BF16).astype(F32), U32)
    hi_b = lax.bitcast_convert_type(hi.astype(BF16).astype(F32), U32)
    return (lo_b >> 16) | (hi_b & jnp.uint32(0xFFFF0000))


def _unpack_bf16_pair(u):
    lo = lax.bitcast_convert_type(u << 16, F32)
    hi = lax.bitcast_convert_type(u & jnp.uint32(0xFFFF0000), F32)
    return lo, hi


def _inproj_kernel(x_ref, g_ref, w_ref, ws_ref, o_ref, os_ref, xn_ref):
    @pl.when(pl.program_id(1) == 0)
    def _():
        x = x_ref[...]
        ms = jnp.mean(x * x, axis=-1, keepdims=True)
        xb = ((x * lax.rsqrt(ms + EPS)) * g_ref[...]).astype(BF16)
        xn_ref[...] = xb
        os_ref[...] = _dot(xb, ws_ref[...])

    o_ref[...] = _dot(xn_ref[...], w_ref[...]).astype(BF16)


def _inproj(x2, g, w_main, w_small, row0, t, tm, tn):
    blk0 = row0 // tm
    return pl.pallas_call(
        _inproj_kernel,
        grid=(t // tm, N_MAIN // tn),
        in_specs=[
            pl.BlockSpec((tm, D_MODEL), lambda i, j: (blk0 + i, 0)),
            pl.BlockSpec((1, D_MODEL), lambda i, j: (0, 0)),
            pl.BlockSpec((D_MODEL, tn), lambda i, j: (0, j)),
            pl.BlockSpec((D_MODEL, N_SMALL), lambda i, j: (0, 0)),
        ],
        out_specs=[
            pl.BlockSpec((tm, tn), lambda i, j: (i, j)),
            pl.BlockSpec((tm, N_SMALL), lambda i, j: (i, 0)),
        ],
        out_shape=[
            jax.ShapeDtypeStruct((t, N_MAIN), BF16),
            jax.ShapeDtypeStruct((t, N_SMALL), F32),
        ],
        scratch_shapes=[pltpu.VMEM((tm, D_MODEL), BF16)],
        compiler_params=pltpu.CompilerParams(
            dimension_semantics=("arbitrary", "arbitrary"), vmem_limit_bytes=VMEM_LIMIT),
        name="inproj",
    )(x2, g, w_main, w_small)


def _tri_masks(n):
    r = lax.broadcasted_iota(I32, (n, n), 0)
    c = lax.broadcasted_iota(I32, (n, n), 1)
    return r >= c, c >= r


class _GlaDir:
    def __init__(self, ins, outs, scratch, mask, lr_off, mid_row, last_row):
        self.q, self.k, self.v, self.sm, self.fw2, self.fb = ins
        self.o, self.st = outs
        self.b_s, self.qs_s, self.ks_s, self.kd_s, self.qe_s, self.p_s, self.u_s, self.el_s = scratch
        self.mask, self.lr_off, self.mid_row, self.last_row = mask, lr_off, mid_row, last_row


def _gla_decay(d, n_chunks):
    tri = jnp.where(d.mask, 1.0, 0.0).astype(BF16)
    lr = d.sm[:, d.lr_off:d.lr_off + GLA_RANK].astype(BF16)
    xg = _dot(lr, d.fw2[...]) + d.fb[...]
    d.b_s[...] = (jnp.minimum(xg, 0.0) - jnp.log(1.0 + jnp.exp(-jnp.abs(xg)))) * (1.0 / GLA_TAU)
    for c in range(n_chunks):
        rows = slice(c * CHUNK, (c + 1) * CHUNK)
        d.b_s[rows, :] = _dot_exact_lhs(tri, d.b_s[rows, :], 2)


def _gla_scale(d, c):
    rows = slice(c * CHUNK, (c + 1) * CHUNK)
    b = d.b_s[rows, :]
    b_mid = b[d.mid_row:d.mid_row + 1, :]
    b_last = b[d.last_row:d.last_row + 1, :]
    q = d.q[rows, :].astype(F32) * (GLA_DK ** -0.5)
    k = d.k[rows, :].astype(F32)
    d.qs_s[rows, :] = (q * jnp.exp(b - b_mid)).astype(BF16)
    d.ks_s[rows, :] = (k * jnp.exp(b_mid - b)).astype(BF16)
    d.kd_s[rows, :] = (k * jnp.exp(b_last - b)).astype(BF16)
    d.qe_s[rows, :] = (q * jnp.exp(b)).astype(BF16)
    d.el_s[c:c + 1, :] = jnp.exp(b_last)


def _gla_local(units):
    def score(u):
        d, c, h = u
        rows, ks_ = slice(c * CHUNK, (c + 1) * CHUNK), slice(h * GLA_DK, (h + 1) * GLA_DK)
        return _dot_nt(d.qs_s[rows, ks_], d.ks_s[rows, ks_])

    def finish(u, s):
        d, c, h = u
        rows, ks_ = slice(c * CHUNK, (c + 1) * CHUNK), slice(h * GLA_DK, (h + 1) * GLA_DK)
        vs_ = slice(h * GLA_DV, (h + 1) * GLA_DV)
        d.p_s[c, h] = jnp.where(d.mask, s, 0.0).astype(BF16)
        d.u_s[c, h] = _dot_tn(d.kd_s[rows, ks_], d.v[rows, vs_])

    pending = []
    for u in units:
        pending.append((u, score(u)))
        if len(pending) > MXU_LAG:
            finish(*pending.pop(0))
    for item in pending:
        finish(*item)


def _gla_carry(d, c):
    rows = slice(c * CHUNK, (c + 1) * CHUNK)
    for h in range(GLA_HEADS):
        ks_ = slice(h * GLA_DK, (h + 1) * GLA_DK)
        vs_ = slice(h * GLA_DV, (h + 1) * GLA_DV)
        st = d.st[h]
        o = _dot(d.p_s[c, h], d.v[rows, vs_]) + _dot(d.qe_s[rows, ks_], st.astype(BF16))
        d.o[rows, vs_] = o.astype(d.o.dtype)
        e_col = jnp.transpose(jnp.broadcast_to(d.el_s[c:c + 1, ks_], (8, GLA_DK)))[:, 0:1]
        d.st[h] = st * e_col + d.u_s[c, h]


def _gla_kernel(*refs, n_chunks):
    ins_f, ins_b, (fw2f, fbf, fw2b, fbb), (of_ref, ob_ref, stf, stb) = refs[0:4], refs[4:8], refs[8:12], refs[12:16]
    scr_f, scr_b = refs[16:24], refs[24:32]

    @pl.when(pl.program_id(1) == 0)
    def _():
        stf[...] = jnp.zeros_like(stf)
        stb[...] = jnp.zeros_like(stb)

    lower, upper = _tri_masks(CHUNK)
    fwd = _GlaDir((*ins_f, fw2f, fbf), (of_ref, stf), scr_f, lower, S_LRF, CHUNK // 2, CHUNK - 1)
    bwd = _GlaDir((*ins_b, fw2b, fbb), (ob_ref, stb), scr_b, upper, S_LRB, CHUNK // 2 - 1, 0)
    _gla_decay(fwd, n_chunks)
    _gla_decay(bwd, n_chunks)
    for c in range(n_chunks):
        _gla_scale(fwd, c)
        _gla_scale(bwd, c)
    _gla_local([(d, c, h) for c in range(n_chunks) for d in (fwd, bwd) for h in range(GLA_HEADS)])
    for i in range(n_chunks):
        _gla_carry(fwd, i)
        _gla_carry(bwd, n_chunks - 1 - i)


def _gla(proj, small, fw2f, fbf, fw2b, fbb, bsz, seq, lb):
    t = bsz * seq
    nb = seq // lb

    def fwd(w, col):
        return pl.BlockSpec((lb, w), lambda b, n: (b * nb + n, col))

    def bwd(w, col):
        return pl.BlockSpec((lb, w), lambda b, n: (b * nb + nb - 1 - n, col))

    const = lambda shape: pl.BlockSpec(shape, lambda b, n: (0, 0))
    per_dir_scratch = ([pltpu.VMEM((lb, GLA_QK), F32)] + [pltpu.VMEM((lb, GLA_QK), BF16)] * 4
                       + [pltpu.VMEM((lb // CHUNK, GLA_HEADS, CHUNK, CHUNK), BF16),
                          pltpu.VMEM((lb // CHUNK, GLA_HEADS, GLA_DK, GLA_DV), F32),
                          pltpu.VMEM((lb // CHUNK, GLA_QK), F32)])
    return pl.pallas_call(
        functools.partial(_gla_kernel, n_chunks=lb // CHUNK),
        grid=(bsz, nb),
        in_specs=[
            fwd(GLA_QK, C_Q // GLA_QK), fwd(GLA_QK, C_K // GLA_QK), fwd(GLA_V, C_V // GLA_V), fwd(N_SMALL, 0),
            bwd(GLA_QK, C_Q // GLA_QK), bwd(GLA_QK, C_K // GLA_QK), bwd(GLA_V, C_V // GLA_V), bwd(N_SMALL, 0),
            const((GLA_RANK, GLA_QK)), const((1, GLA_QK)), const((GLA_RANK, GLA_QK)), const((1, GLA_QK)),
        ],
        out_specs=[fwd(GLA_V, 0), bwd(GLA_V, 0)],
        out_shape=[jax.ShapeDtypeStruct((t, GLA_V), BF16)] * 2,
        scratch_shapes=[pltpu.VMEM((GLA_HEADS, GLA_DK, GLA_DV), F32)] * 2 + per_dir_scratch * 2,
        compiler_params=pltpu.CompilerParams(
            dimension_semantics=("arbitrary", "arbitrary"), vmem_limit_bytes=VMEM_LIMIT),
        name="gla_scan",
    )(proj, proj, proj, small, proj, proj, proj, small, fw2f, fbf, fw2b, fbb)


HALO = 16


CONV_SUB = 256
CONV_COLS = 512


def _conv_taps(x, xm1, xp1, xp2, w, b):
    return _silu(xm1 * w[0:1, :] + x * w[1:2, :] + xp1 * w[2:3, :] + xp2 * w[3:4, :] + b)


def _shift_matrix(n):
    r = lax.broadcasted_iota(I32, (3 * n, n), 0)
    c = lax.broadcasted_iota(I32, (3 * n, n), 1)
    src = jnp.where(r < n, r - 1, jnp.where(r < 2 * n, r - n + 1, r - 2 * n + 2))
    return jnp.where(c == src, 1.0, 0.0).astype(BF16)


def _conv_one(x_ref, p_ref, n_ref, w_ref, b_ref, o_ref, shift, has_prev, has_next):
    rb, wd = x_ref.shape
    w, b = w_ref[...], b_ref[...]
    for s in range(rb // CONV_SUB):
        rows = slice(s * CONV_SUB, (s + 1) * CONV_SUB)
        for c0 in range(0, wd, CONV_COLS):
            cols = slice(c0, c0 + CONV_COLS)
            xb = x_ref[rows, cols]
            sx = _dot(shift, xb)
            y = _conv_taps(xb.astype(F32), sx[0:CONV_SUB], sx[CONV_SUB:2 * CONV_SUB], sx[2 * CONV_SUB:],
                           w[:, cols], b[:, cols])
            o_ref[rows, cols] = y.astype(o_ref.dtype)
    prev = jnp.where(has_prev, p_ref[HALO - 1:HALO, :].astype(F32), 0.0)
    nxt = jnp.where(has_next, n_ref[0:2, :].astype(F32), 0.0)
    row = lax.broadcasted_iota(I32, (HALO, wd), 0)
    head = x_ref[0:2 * HALO, :].astype(F32)
    xm1 = jnp.where(row == 0, prev, pltpu.roll(head, 1, 0)[0:HALO])
    o_ref[0:HALO, :] = _conv_taps(head[0:HALO], xm1, pltpu.roll(head, 2 * HALO - 1, 0)[0:HALO],
                                  pltpu.roll(head, 2 * HALO - 2, 0)[0:HALO], w, b).astype(o_ref.dtype)
    tail = x_ref[rb - 2 * HALO:rb, :].astype(F32)
    xp1 = jnp.where(row == HALO - 1, nxt[0:1, :], pltpu.roll(tail, 2 * HALO - 1, 0)[HALO:])
    xp2 = jnp.where(row == HALO - 2, nxt[0:1, :],
                    jnp.where(row == HALO - 1, nxt[1:2, :], pltpu.roll(tail, 2 * HALO - 2, 0)[HALO:]))
    o_ref[rb - HALO:rb, :] = _conv_taps(tail[HALO:], pltpu.roll(tail, 1, 0)[HALO:], xp1, xp2, w, b).astype(o_ref.dtype)
    for s in range(1, rb // CONV_SUB):
        e = s * CONV_SUB
        win = x_ref[e - 2 * HALO:e + 2 * HALO, :].astype(F32)
        mid = slice(HALO, 3 * HALO)
        o_ref[e - HALO:e + HALO, :] = _conv_taps(
            win[mid], pltpu.roll(win, 1, 0)[mid], pltpu.roll(win, 4 * HALO - 1, 0)[mid],
            pltpu.roll(win, 4 * HALO - 2, 0)[mid], w, b).astype(o_ref.dtype)


def _conv_kernel(xs, xsp, xsn, bm, bmp, bmn, cm, cmp_, cmn, wx, bx, wb, bb, wc, bc, oxs, obm, ocm, *, rb, seq):
    t0 = pl.program_id(0) * rb
    has_prev = (t0 % seq) != 0
    has_next = ((t0 + rb) % seq) != 0
    shift = _shift_matrix(CONV_SUB)
    _conv_one(xs, xsp, xsn, wx, bx, oxs, shift, has_prev, has_next)
    _conv_one(bm, bmp, bmn, wb, bb, obm, shift, has_prev, has_next)
    _conv_one(cm, cmp_, cmn, wc, bc, ocm, shift, has_prev, has_next)


def _conv(proj, conv_w, conv_b, seq, rb):
    t = proj.shape[0]
    nh = t // HALO
    per = rb // HALO

    def trio(w, col):
        cb = col // w
        return [
            pl.BlockSpec((rb, w), lambda i: (i, cb)),
            pl.BlockSpec((HALO, w), lambda i: (jnp.maximum(i * per - 1, 0), cb)),
            pl.BlockSpec((HALO, w), lambda i: (jnp.minimum((i + 1) * per, nh - 1), cb)),
        ]

    def wspecs(w):
        return [pl.BlockSpec((SSD_CONV, w), lambda i: (0, 0)), pl.BlockSpec((1, w), lambda i: (0, 0))]

    wx, wb, wc = conv_w[:, :SSD_DINNER], conv_w[:, SSD_DINNER:SSD_DINNER + SSD_BC], conv_w[:, SSD_DINNER + SSD_BC:]
    bx, bb, bc = (conv_b[None, :SSD_DINNER], conv_b[None, SSD_DINNER:SSD_DINNER + SSD_BC],
                  conv_b[None, SSD_DINNER + SSD_BC:])
    return pl.pallas_call(
        functools.partial(_conv_kernel, rb=rb, seq=seq),
        grid=(t // rb,),
        in_specs=trio(SSD_DINNER, C_XS) + trio(SSD_BC, C_B) + trio(SSD_BC, C_C)
        + wspecs(SSD_DINNER) + wspecs(SSD_BC) + wspecs(SSD_BC),
        out_specs=[pl.BlockSpec((rb, SSD_DINNER), lambda i: (i, 0)),
                   pl.BlockSpec((rb, SSD_BC), lambda i: (i, 0)),
                   pl.BlockSpec((rb, SSD_BC), lambda i: (i, 0))],
        out_shape=[jax.ShapeDtypeStruct((t, SSD_DINNER), BF16),
                   jax.ShapeDtypeStruct((t, SSD_BC), BF16),
                   jax.ShapeDtypeStruct((t, SSD_BC), BF16)],
        compiler_params=pltpu.CompilerParams(dimension_semantics=("arbitrary",), vmem_limit_bytes=VMEM_LIMIT),
        name="ssd_conv",
    )(proj, proj, proj, proj, proj, proj, proj, proj, proj, wx, bx, wb, bb, wc, bc)


def _softplus(x):
    return jnp.maximum(x, 0.0) + jnp.log(1.0 + jnp.exp(-jnp.abs(x)))


class _SsdDir:
    def __init__(self, xs, bm, cm, sm, dtb_row, alog_row, y, st, reverse, dt_off, last_row):
        self.xs, self.bm, self.cm, self.sm, self.dtb_row, self.alog_row = xs, bm, cm, sm, dtb_row, alog_row
        self.y, self.st, self.reverse, self.dt_off, self.last_row = y, st, reverse, dt_off, last_row


def _ssd_chunks(work):
    hh = lax.broadcasted_iota(I32, (SSD_HEADS, SSD_GW), 0)
    cc = lax.broadcasted_iota(I32, (SSD_HEADS, SSD_GW), 1)
    lane = lax.broadcasted_iota(I32, (CHUNK, 2 * SSD_HEADDIM), 1)
    row2 = lax.broadcasted_iota(I32, (CHUNK, 2 * SSD_HEADDIM), 0)
    left = lane < SSD_HEADDIM
    col2 = jnp.where(left, lane, lane - SSD_HEADDIM)
    lower, upper = _tri_masks(CHUNK)

    pre = []
    for d, c0 in work:
        rows = pl.ds(c0, CHUNK)
        tri = jnp.where(upper if d.reverse else lower, 1.0, 0.0).astype(BF16)
        a_row = -jnp.exp(d.alog_row[...])
        dt = _softplus(d.sm[rows, d.dt_off:d.dt_off + SSD_HEADS] + d.dtb_row[...])
        pre.append((dt, _dot_exact_lhs(tri, dt * a_row, 3)))
    heads = []
    for (d, c0), (dt, cum) in zip(work, pre):
        total = cum[d.last_row:d.last_row + 1, :]
        to_end = jnp.exp(total - cum) * dt
        fac = jnp.concatenate([to_end, jnp.exp(cum)], axis=0).astype(BF16)
        e_tot = jnp.broadcast_to(jnp.exp(total), (8, SSD_HEADS))
        heads.append((cum, jnp.transpose(cum), jnp.transpose(dt), fac, e_tot))

    units = [(w, g) for w in range(len(work)) for g in range(SSD_GROUPS)]
    groups = {}
    for w, g in units:
        d, c0 = work[w]
        rows, ns = pl.ds(c0, CHUNK), slice(g * SSD_STATE, (g + 1) * SSD_STATE)
        fac, e_tot = heads[w][3], heads[w][4]
        expand = jnp.where(cc // SSD_HEADDIM + g * SSD_HPG == hh, 1.0, 0.0).astype(BF16)
        fac_x = _dot(fac, expand)
        et_x = _dot_exact_rhs(e_tot, expand, 2)[0:1]
        groups[w, g] = (fac_x, et_x, _dot_nt(d.cm[rows, ns], d.bm[rows, ns]))

    for w, g in units:
        d, c0 = work[w]
        rows, ns, cs = pl.ds(c0, CHUNK), slice(g * SSD_STATE, (g + 1) * SSD_STATE), slice(g * SSD_GW, (g + 1) * SSD_GW)
        cum, cum_t, dt_t = heads[w][0:3]
        fac_x, _, cb = groups[w, g]
        mask2 = (col2 >= row2) if d.reverse else (row2 >= col2)
        y_inter = _dot(d.cm[rows, ns], d.st[g].astype(BF16))
        cb2 = jnp.concatenate([cb, cb], axis=1)
        parts = []
        for p in range(SSD_HPG // 2):
            h0 = g * SSD_HPG + 2 * p
            ps = slice(h0 * SSD_HEADDIM, (h0 + 2) * SSD_HEADDIM)
            col = jnp.where(left, cum[:, h0:h0 + 1], cum[:, h0 + 1:h0 + 2])
            rowv = jnp.concatenate([cum_t[h0:h0 + 1, :], cum_t[h0 + 1:h0 + 2, :]], axis=1)
            dtv = jnp.concatenate([dt_t[h0:h0 + 1, :], dt_t[h0 + 1:h0 + 2, :]], axis=1)
            decay = jnp.exp(jnp.where(mask2, col - rowv, -jnp.inf))
            wgt = (cb2 * decay * dtv).astype(BF16)
            xp = d.xs[rows, ps]
            zero = jnp.zeros_like(xp)
            xbd = jnp.concatenate([jnp.where(left, xp, zero), jnp.where(left, zero, xp)], axis=0)
            parts.append(_dot(wgt, xbd))
        y = jnp.concatenate(parts, axis=1) + y_inter * fac_x[CHUNK:2 * CHUNK]
        d.y[rows, cs] = y.astype(d.y.dtype)

    def increment(w, g):
        d, c0 = work[w]
        rows, ns, cs = pl.ds(c0, CHUNK), slice(g * SSD_STATE, (g + 1) * SSD_STATE), slice(g * SSD_GW, (g + 1) * SSD_GW)
        return _dot_tn(d.bm[rows, ns], d.xs[rows, cs] * groups[w, g][0][0:CHUNK].astype(BF16))

    def update(w, g, inc):
        d = work[w][0]
        d.st[g] = d.st[g] * groups[w, g][1] + inc

    pending = None
    for w, g in units:
        inc = increment(w, g)
        if pending is not None:
            update(*pending)
        pending = (w, g, inc)
    update(*pending)


def _ssd_kernel(*refs, n_chunks, inner):
    ins, (dbf_r, dbb_r, alf_r, alb_r) = refs[:8 * inner], refs[8 * inner:8 * inner + 4]
    (yf_ref, yb_ref), states = refs[8 * inner + 4:8 * inner + 6], refs[8 * inner + 6:]

    @pl.when(pl.program_id(1) == 0)
    def _():
        for st in states:
            st[...] = jnp.zeros_like(st)

    fwd = [_SsdDir(*ins[8 * e:8 * e + 4], dbf_r, alf_r, yf_ref.at[e], states[2 * e], False, S_DTF, CHUNK - 1)
           for e in range(inner)]
    bwd = [_SsdDir(*ins[8 * e + 4:8 * e + 8], dbb_r, alb_r, yb_ref.at[e], states[2 * e + 1], True, S_DTB, 0)
           for e in range(inner)]

    def body(i, carry):
        c_f = pl.multiple_of(i * CHUNK, CHUNK)
        c_b = pl.multiple_of((n_chunks - 1 - i) * CHUNK, CHUNK)
        _ssd_chunks([(d, c_f) for d in fwd] + [(d, c_b) for d in bwd])
        return carry

    lax.fori_loop(0, n_chunks, body, 0)


def _ssd(xs_c, bm_c, cm_c, small, dtb_f, dtb_b, alog_f, alog_b, bsz, seq, lb):
    t = bsz * seq
    nb = seq // lb
    inner = SSD_INNER if bsz % SSD_INNER == 0 else 1

    def specs(e, reverse):
        idx = lambda p, n: (p * inner + e) * nb + (nb - 1 - n if reverse else n)
        return [pl.BlockSpec((lb, w), lambda p, n: (idx(p, n), 0)) for w in (SSD_DINNER, SSD_BC, SSD_BC, N_SMALL)]

    row = pl.BlockSpec((1, SSD_HEADS), lambda p, n: (0, 0))
    args = (xs_c, bm_c, cm_c, small)
    y_f, y_b = pl.pallas_call(
        functools.partial(_ssd_kernel, n_chunks=lb // CHUNK, inner=inner),
        grid=(bsz // inner, nb),
        in_specs=[s for e in range(inner) for rev in (False, True) for s in specs(e, rev)] + [row] * 4,
        out_specs=[pl.BlockSpec((None, inner, lb, SSD_DINNER), lambda p, n: (p, 0, n, 0)),
                   pl.BlockSpec((None, inner, lb, SSD_DINNER), lambda p, n: (p, 0, nb - 1 - n, 0))],
        out_shape=[jax.ShapeDtypeStruct((bsz // inner, inner, seq, SSD_DINNER), BF16)] * 2,
        scratch_shapes=[pltpu.VMEM((SSD_GROUPS, SSD_STATE, SSD_GW), F32)] * (2 * inner),
        compiler_params=pltpu.CompilerParams(
            dimension_semantics=("arbitrary", "arbitrary"), vmem_limit_bytes=VMEM_LIMIT),
        name="ssd_scan",
    )(*(args * (2 * inner)), dtb_f[None, :], dtb_b[None, :], alog_f[None, :], alog_b[None, :])
    return y_f.reshape(t, SSD_DINNER), y_b.reshape(t, SSD_DINNER)


def _merge_kernel(x_ref, of_ref, ob_ref, r_ref, gg_ref, yf_ref, yb_ref, xs_ref, z_ref, gs_ref,
                  gng_ref, dsk_ref, sng_ref, wug_ref, wus_ref, wo_ref, nfg_ref, wr_ref, br_ref,
                  h_ref, xp_ref, route_ref, gate_ref, cnt_out_ref, cnt_ref):
    @pl.when(pl.program_id(0) == 0)
    def _():
        cnt_ref[...] = jnp.zeros_like(cnt_ref)

    tm = x_ref.shape[0]
    subs = [slice(s * (tm // MERGE_SUBTILES), (s + 1) * (tm // MERGE_SUBTILES)) for s in range(MERGE_SUBTILES)]

    def gla_branch(rows):
        o = of_ref[rows, :].astype(F32) + ob_ref[rows, :].astype(F32)
        gng = gng_ref[...]
        o_parts = []
        for h in range(GLA_HEADS):
            oh = o[:, h * GLA_DV:(h + 1) * GLA_DV]
            oh = oh * lax.rsqrt(jnp.mean(oh * oh, axis=-1, keepdims=True) + EPS)
            o_parts.append(oh * gng)
        return (jnp.concatenate(o_parts, axis=1) * _silu(r_ref[rows, :]).astype(F32)).astype(BF16)

    def ssd_branch(rows):
        y = (yf_ref[rows, :].astype(F32) + yb_ref[rows, :].astype(F32)
             + dsk_ref[...] * xs_ref[rows, :].astype(F32))
        y = y * _silu(z_ref[rows, :]).astype(F32)
        sng = sng_ref[...]
        y_parts = []
        for g in range(SSD_GROUPS):
            yg = y[:, g * SSD_GW:(g + 1) * SSD_GW]
            yg = yg * lax.rsqrt(jnp.mean(yg * yg, axis=-1, keepdims=True) + EPS)
            y_parts.append(yg * sng[:, g * SSD_GW:(g + 1) * SSD_GW])
        return jnp.concatenate(y_parts, axis=1).astype(BF16)

    up_g = [_dot(gla_branch(rows), wug_ref[...]) for rows in subs]
    up_s = [_dot(ssd_branch(rows), wus_ref[...]) for rows in subs]
    mix = [(_sigmoid(gg_ref[rows, :]).astype(F32) * ug + _sigmoid(gs_ref[rows, :]).astype(F32) * us).astype(BF16)
           for rows, ug, us in zip(subs, up_g, up_s)]
    hs = [x_ref[rows, :] + _dot(m, wo_ref[...]) for rows, m in zip(subs, mix)]
    logit_parts = []
    for rows, h in zip(subs, hs):
        h_ref[rows, :] = h
        hn = (h * lax.rsqrt(jnp.mean(h * h, axis=-1, keepdims=True) + EPS)) * nfg_ref[...]
        words = _pack_bf16_pair(hn[:, :D_MODEL // 2], hn[:, D_MODEL // 2:])
        xp_ref[0, rows, :] = words[:, :SC_W]
        xp_ref[1, rows, :] = words[:, SC_W:]
        hn_hi, hn_lo = _split_bf16(hn, 2)
        logit_parts.append(_dot(hn_hi, wr_ref[0]) + (_dot(hn_hi, wr_ref[1]) + _dot(hn_lo, wr_ref[0])))
    logits = jnp.concatenate(logit_parts, axis=0) + br_ref[...]
    lane = lax.broadcasted_iota(I32, (tm, N_EXPERTS), 1).astype(F32)
    work = logits
    idxs, vals = [], []
    for _ in range(TOP_K):
        m = jnp.max(work, axis=-1, keepdims=True)
        idx = jnp.min(jnp.where(work == m, lane, float(N_EXPERTS)), axis=-1, keepdims=True)
        idxs.append(idx)
        vals.append(m)
        work = jnp.where(lane == idx, -jnp.inf, work)
    exps = [jnp.exp(v - vals[0]) for v in vals]
    denom = exps[0] + exps[1] + exps[2] + exps[3]
    gates = [e / denom for e in exps]
    sel = jnp.zeros((tm, N_EXPERTS), F32)
    for idx in idxs:
        sel = sel + jnp.where(lane == idx, 1.0, 0.0)
    rr = lax.broadcasted_iota(I32, (tm, tm), 0)
    cc = lax.broadcasted_iota(I32, (tm, tm), 1)
    strict = jnp.where(rr > cc, 1.0, 0.0).astype(BF16)
    pos = _dot(strict, sel.astype(BF16)) + cnt_ref[0:1, :]
    ranks = [jnp.sum(jnp.where(lane == idx, pos, 0.0), axis=-1, keepdims=True).astype(I32) for idx in idxs]
    cnt_new = cnt_ref[0:1, :] + jnp.sum(sel, axis=0, keepdims=True)
    cnt_ref[...] = jnp.broadcast_to(cnt_new, cnt_ref.shape)
    cnt_out_ref[...] = jnp.broadcast_to(cnt_new, cnt_ref.shape).astype(I32)
    lane128 = lax.broadcasted_iota(I32, (tm, 128), 1)
    route = jnp.zeros((tm, 128), I32)
    gate_o = jnp.zeros((tm, 128), F32)
    for k in range(TOP_K):
        route = jnp.where(lane128 == k, idxs[k].astype(I32), route)
        route = jnp.where(lane128 == TOP_K + k, ranks[k], route)
        gate_o = jnp.where(lane128 == k, gates[k], gate_o)
    route_ref[...] = route
    gate_ref[...] = gate_o


def _merge_params(gla_norm_g, ssd_d, ssd_norm_g, w_up_gla, w_up_ssd, w_out, norm_ffn_g, w_router, b_router):
    d_skip = jnp.repeat(ssd_d, SSD_HEADDIM)[None, :]
    return [gla_norm_g[None, :], d_skip, ssd_norm_g[None, :], w_up_gla.astype(BF16), w_up_ssd.astype(BF16),
            w_out.astype(BF16), norm_ffn_g[None, :], jnp.stack(_split_bf16(w_router, 2)), b_router[None, :]]


def _merge(x2, row0, t, o_f, o_b, proj, y_f, y_b, xs_c, params, tm):
    blk0 = row0 // tm
    rowblk = lambda w, col=0: pl.BlockSpec((tm, w), lambda i: (blk0 + i, col))
    outblk = lambda w: pl.BlockSpec((tm, w), lambda i: (i, 0))
    const = lambda a: pl.BlockSpec(a.shape, lambda i: (0,) * a.ndim, pipeline_mode=pl.Buffered(1))
    return pl.pallas_call(
        _merge_kernel,
        grid=(t // tm,),
        in_specs=[rowblk(D_MODEL),
                  rowblk(GLA_V), rowblk(GLA_V), rowblk(GLA_V, C_R // GLA_V),
                  rowblk(D_MODEL, C_GG // D_MODEL), rowblk(SSD_DINNER), rowblk(SSD_DINNER), rowblk(SSD_DINNER),
                  rowblk(SSD_DINNER, C_Z // SSD_DINNER), rowblk(D_MODEL, C_GS // D_MODEL)]
        + [const(p) for p in params],
        out_specs=[outblk(D_MODEL), pl.BlockSpec((2, tm, SC_W), lambda i: (0, i, 0)), outblk(128), outblk(128),
                   pl.BlockSpec((8, N_EXPERTS), lambda i: (0, 0))],
        out_shape=[jax.ShapeDtypeStruct((t, D_MODEL), F32), jax.ShapeDtypeStruct((2, t, SC_W), U32),
                   jax.ShapeDtypeStruct((t, 128), I32), jax.ShapeDtypeStruct((t, 128), F32),
                   jax.ShapeDtypeStruct((8, N_EXPERTS), I32)],
        scratch_shapes=[pltpu.VMEM((8, N_EXPERTS), F32)],
        compiler_params=pltpu.CompilerParams(dimension_semantics=("arbitrary",), vmem_limit_bytes=VMEM_LIMIT),
        name="merge_router",
    )(x2, o_f, o_b, proj, proj, y_f, y_b, xs_c, proj, proj, *params)


def _sc_mesh():
    return plsc.VectorSubcoreMesh(core_axis_name="c", subcore_axis_name="s")


def _sc_scatter_rows(x, idx, n_out):
    n, m = x.shape[0], idx.shape[1]
    n_win = n // SC_WIN

    @functools.partial(pl.kernel, out_type=jax.ShapeDtypeStruct((n_out, SC_W), x.dtype), mesh=_sc_mesh())
    def scatter(x_hbm, i_hbm, o_hbm):
        def body(x_vmem, i_vmem):
            pltpu.sync_copy(x_vmem, o_hbm.at[i_vmem.at[0]])

        pltpu.emit_pipeline(
            body, grid=(m // SC_WIN,),
            in_specs=[pl.BlockSpec((SC_WIN, SC_W), lambda i: (i % n_win, 0)),
                      pl.BlockSpec((1, SC_WIN), lambda i: (0, i))],
            out_specs=[], core_axis_name=("c", "s"), dimension_semantics=(pltpu.PARALLEL,),
        )(x_hbm, i_hbm)

    return scatter(x, idx)


def _sc_gather_rows(table, idx):
    m = idx.shape[1]

    @functools.partial(pl.kernel, out_type=jax.ShapeDtypeStruct((m, SC_W), table.dtype), mesh=_sc_mesh())
    def gather(t_hbm, i_hbm, o_hbm):
        def body(i_vmem, o_vmem):
            pltpu.sync_copy(t_hbm.at[i_vmem.at[0]], o_vmem)

        pltpu.emit_pipeline(
            body, grid=(m // SC_WIN,),
            in_specs=[pl.BlockSpec((1, SC_WIN), lambda i: (0, i))],
            out_specs=[pl.BlockSpec((SC_WIN, SC_W), lambda i: (i, 0))],
            core_axis_name=("c", "s"), dimension_semantics=(pltpu.PARALLEL,),
        )(i_hbm, o_hbm)

    return gather(table, idx)


def _expert_kernel(blk_e_ref, nvalid_ref, nused_ref, x_ref, w1_ref, b1_ref, w2_ref, b2_ref, o_ref, w1_s, w2_s):
    del nused_ref
    i = pl.program_id(0)
    nvalid = nvalid_ref[i]

    @pl.when((nvalid > 0) & ((i == 0) | (blk_e_ref[i] != blk_e_ref[jnp.maximum(i - 1, 0)])))
    def _():
        w1_s[...] = w1_ref[0].astype(BF16)
        w2_s[...] = w2_ref[0].astype(BF16)

    @pl.when(nvalid > 0)
    def _():
        live = lax.broadcasted_iota(I32, (MOE_BM, SC_W), 0) < nvalid
        lo0, hi0 = _unpack_bf16_pair(jnp.where(live, x_ref[0], jnp.uint32(0)))
        lo1, hi1 = _unpack_bf16_pair(jnp.where(live, x_ref[1], jnp.uint32(0)))
        x = jnp.concatenate([lo0, lo1, hi0, hi1], axis=1).astype(BF16)
        hdn = _dot(x, w1_s[...]) + b1_ref[0]
        gate = jnp.minimum(hdn[:, :D_FF], SWIGLU_LIMIT)
        lin = jnp.clip(hdn[:, D_FF:], -SWIGLU_LIMIT, SWIGLU_LIMIT)
        act = gate * _sigmoid(SWIGLU_ALPHA * gate) * (lin + 1.0)
        y = _dot(act.astype(BF16), w2_s[...]) + b2_ref[0]
        words = _pack_bf16_pair(y[:, :D_MODEL // 2], y[:, D_MODEL // 2:])
        o_ref[0] = words[:, :SC_W]
        o_ref[1] = words[:, SC_W:]

    @pl.when(nvalid == 0)
    def _():
        o_ref[...] = jnp.zeros_like(o_ref)


def _experts(xb, blk_e, nvalid, n_used, w1, b1, w2, b2):
    n_rows = xb.shape[1]
    n_blocks = n_rows // MOE_BM
    xidx = lambda i, be, nv, nu: (0, jnp.minimum(i, nu[0] - 1), 0)
    eidx = lambda i, be, nv, nu: (be[i], 0, 0)
    return pl.pallas_call(
        _expert_kernel,
        grid_spec=pltpu.PrefetchScalarGridSpec(
            num_scalar_prefetch=3,
            grid=(n_blocks,),
            in_specs=[pl.BlockSpec((2, MOE_BM, SC_W), xidx),
                      pl.BlockSpec((1, D_MODEL, 2 * D_FF), eidx),
                      pl.BlockSpec((1, 1, 2 * D_FF), eidx),
                      pl.BlockSpec((1, D_FF, D_MODEL), eidx),
                      pl.BlockSpec((1, 1, D_MODEL), eidx)],
            out_specs=pl.BlockSpec((2, MOE_BM, SC_W), lambda i, be, nv, nu: (0, i, 0)),
            scratch_shapes=[pltpu.VMEM((D_MODEL, 2 * D_FF), BF16), pltpu.VMEM((D_FF, D_MODEL), BF16)],
        ),
        out_shape=jax.ShapeDtypeStruct((2, n_rows, SC_W), U32),
        compiler_params=pltpu.CompilerParams(dimension_semantics=("arbitrary",), vmem_limit_bytes=VMEM_LIMIT),
        name="moe_experts",
    )(blk_e, nvalid, n_used, xb, w1, b1, w2, b2)


def _combine_kernel(h_ref, gate_ref, y_ref, g_ref, *rest):
    o_ref = rest[-1]
    gates = gate_ref[...]
    acc = None
    for k in range(TOP_K):
        lo0, hi0 = _unpack_bf16_pair(y_ref[k, 0])
        lo1, hi1 = _unpack_bf16_pair(y_ref[k, 1])
        term = gates[:, k:k + 1] * jnp.concatenate([lo0, lo1, hi0, hi1], axis=1)
        acc = term if acc is None else acc + term
    h = h_ref[...] + acc
    o_ref[...] = (h * lax.rsqrt(jnp.mean(h * h, axis=-1, keepdims=True) + EPS)) * g_ref[...]


def _combine(h, gates, y_rows, norm_final_g, t_total, row0, prev, tm):
    t = h.shape[0]
    blk0 = row0 // tm
    in_specs = [pl.BlockSpec((tm, D_MODEL), lambda i: (i, 0)),
                pl.BlockSpec((tm, 128), lambda i: (i, 0)),
                pl.BlockSpec((TOP_K, 2, tm, SC_W), lambda i: (0, 0, i, 0)),
                pl.BlockSpec((1, D_MODEL), lambda i: (0, 0))]
    args = [h, gates, y_rows, norm_final_g[None, :]]
    if prev is not None:
        in_specs.append(pl.BlockSpec(memory_space=pl.ANY))
        args.append(prev)
    return pl.pallas_call(
        _combine_kernel,
        grid=(t // tm,),
        in_specs=in_specs,
        out_specs=pl.BlockSpec((tm, D_MODEL), lambda i: (blk0 + i, 0)),
        out_shape=jax.ShapeDtypeStruct((t_total, D_MODEL), F32),
        input_output_aliases={} if prev is None else {4: 0},
        compiler_params=pltpu.CompilerParams(dimension_semantics=("arbitrary",), vmem_limit_bytes=VMEM_LIMIT),
        name="moe_combine",
    )(*args)


def _pick(n, pref):
    b = min(pref, n)
    while n % b:
        b -= CHUNK
    return b


class _Tiles(NamedTuple):
    inproj_rows: int
    inproj_cols: int
    scan_rows: int
    conv_rows: int
    merge_rows: int
    combine_rows: int

    @classmethod
    def choose(cls, t, seq):
        return cls(inproj_rows=_pick(t, 2048), inproj_cols=1280, scan_rows=_pick(seq, 512),
                   conv_rows=_pick(seq, 512), merge_rows=_pick(t, 512), combine_rows=_pick(t, 1024))


def _prep_w_in(w_in):
    widths = (GLA_QK, GLA_QK, GLA_V, GLA_V, GLA_RANK, GLA_RANK, SSD_DINNER, SSD_DINNER, SSD_BC, SSD_BC,
              2 * SSD_HEADS, D_MODEL, D_MODEL)
    pts, acc = [], 0
    for w in widths[:-1]:
        acc += w
        pts.append(acc)
    q, k, v, r, lrf, lrb, z, xs, bm, cm, dtr, gg, gs = jnp.split(w_in, pts, axis=1)
    main = jnp.concatenate([z, xs, q, k, v, r, gg, gs, bm, cm], axis=1).astype(BF16)
    pad = jnp.zeros((D_MODEL, N_SMALL - 2 * GLA_RANK - 2 * SSD_HEADS), w_in.dtype)
    small = jnp.concatenate([lrf, lrb, dtr, pad], axis=1).astype(BF16)
    return main, small


def _layer(h, norm_mix_g, w_in, gla_fw2_f, gla_fb_f, gla_fw2_b, gla_fb_b, gla_norm_g, conv_w, conv_b,
           dt_bias_f, dt_bias_b, a_log_f, a_log_b, ssd_d, ssd_norm_g, w_up_gla, w_up_ssd, w_out,
           norm_ffn_g, w_router, b_router, w1, b1, w2, b2, out_norm_g):
    bsz, seq, _ = h.shape
    t_total = bsz * seq
    x2 = h.reshape(t_total, D_MODEL)
    w_main, w_small = _prep_w_in(w_in)
    fw2f, fw2b = gla_fw2_f.astype(BF16), gla_fw2_b.astype(BF16)
    mparams = _merge_params(gla_norm_g, ssd_d, ssd_norm_g, w_up_gla, w_up_ssd, w_out, norm_ffn_g, w_router, b_router)
    n_groups = TOKEN_GROUPS if bsz % TOKEN_GROUPS == 0 else 1
    gb = bsz // n_groups
    t = gb * seq
    tiles = _Tiles.choose(t, seq)
    proj, small = _inproj(x2, norm_mix_g[None, :], w_main, w_small, 0, t_total,
                          tm=tiles.inproj_rows, tn=tiles.inproj_cols)
    o_f, o_b = _gla(proj, small, fw2f, gla_fb_f[None, :], fw2b, gla_fb_b[None, :], bsz, seq, tiles.scan_rows)
    xs_c, bm_c, cm_c = _conv(proj, conv_w, conv_b, seq, tiles.conv_rows)
    y_f, y_b = _ssd(xs_c, bm_c, cm_c, small, dt_bias_f, dt_bias_b, a_log_f, a_log_b, bsz, seq, tiles.scan_rows)
    out = None
    for grp in range(n_groups):
        row0 = grp * t
        hres, xp, route, gates, counts8 = _merge(x2, row0, t, o_f, o_b, proj, y_f, y_b, xs_c, mparams,
                                                 tm=tiles.merge_rows)
        counts = counts8[0]
        padded = (counts + MOE_BM - 1) // MOE_BM * MOE_BM
        pend = jnp.cumsum(padded)
        pstart = (pend - padded).astype(I32)
        n_rows = t * TOP_K + N_EXPERTS * MOE_BM
        blk_row = jnp.arange(n_rows // MOE_BM, dtype=I32) * MOE_BM
        blk_e = jnp.minimum(jnp.sum(pend[None, :] <= blk_row[:, None], axis=1), N_EXPERTS - 1).astype(I32)
        nvalid = jnp.clip(pstart[blk_e] + counts[blk_e] - blk_row, 0, MOE_BM).astype(I32)
        n_used = (pend[-1:] // MOE_BM).astype(I32)
        top_e, rank = route[:, :TOP_K], route[:, TOP_K:2 * TOP_K]
        dest = jnp.sum(jnp.where(top_e[:, :, None] == jnp.arange(N_EXPERTS, dtype=I32), pstart, 0), axis=-1) + rank
        idx = (dest.T[:, None, :] + (jnp.arange(2, dtype=I32) * n_rows)[None, :, None]).reshape(1, 2 * TOP_K * t)
        xb = _sc_scatter_rows(xp.reshape(2 * t, SC_W), idx, 2 * n_rows).reshape(2, n_rows, SC_W)
        yb = _experts(xb, blk_e, nvalid, n_used, w1, b1[:, None, :], w2, b2[:, None, :])
        y_rows = _sc_gather_rows(yb.reshape(2 * n_rows, SC_W), idx).reshape(TOP_K, 2, t, SC_W)
        out = _combine(hres, gates, y_rows, out_norm_g, t_total, row0, out, tm=tiles.combine_rows)
    return out.reshape(bsz, seq, D_MODEL)


def kernel(x, norm_mix_g, w_in, gla_fw2_f, gla_fb_f, gla_fw2_b, gla_fb_b, gla_norm_g, conv_w, conv_b, dt_bias_f,
           dt_bias_b, a_log_f, a_log_b, ssd_d, ssd_norm_g, w_up_gla, w_up_ssd, w_out, norm_ffn_g, w_router,
           b_router, w1, b1, w2, b2, norm_final_g):
    assert x.shape[-1] == D_MODEL and norm_mix_g.shape[0] == 1
    return _layer(x, norm_mix_g[0], w_in[0], gla_fw2_f[0], gla_fb_f[0], gla_fw2_b[0], gla_fb_b[0], gla_norm_g[0],
                  conv_w[0], conv_b[0], dt_bias_f[0], dt_bias_b[0], a_log_f[0], a_log_b[0], ssd_d[0],
                  ssd_norm_g[0], w_up_gla[0], w_up_ssd[0], w_out[0], norm_ffn_g[0], w_router[0], b_router[0],
                  w1[0], b1[0], w2[0], b2[0], norm_final_g)
```
